```python
import jax, jax.numpy as jnp
from jax import lax
import numpy as np

D_MODEL = 1024
BATCH = 16
SEQ = 2048
DEPTH = 2

CHUNK = 64
D_MIX = D_MODEL
RWKV_HEAD = 64
D_RWKV = D_MIX // 2
RWKV_HEADS = D_RWKV // RWKV_HEAD
W_LORA = 64
A_LORA = 64
V_LORA = 32
G_LORA = 160
D_RWKV_IN = 3 * D_RWKV + W_LORA + A_LORA + G_LORA
RWKV_GN_EPS = 64e-5
D_GLA = D_MIX - D_RWKV
GLA_HEADS = 4
GLA_DV = D_GLA // GLA_HEADS
GLA_DK = GLA_DV // 2
GLA_GATE_RANK = 16
GLA_GATE_NORMALIZER = 16.0
D_GLA_IN = 2 * GLA_HEADS * GLA_DK + D_GLA + GLA_GATE_RANK + D_GLA
D_IN = D_RWKV_IN + D_GLA_IN
RMS_EPS = 1e-6
N_EXPERTS = 32
TOP_K = 4
D_FF = D_MODEL
SWIGLU_LIMIT = 7.0
SWIGLU_ALPHA = 1.702
MOE_BLOCK = 256
DEEPNORM_ALPHA = (2 * DEPTH) ** 0.25
DEEPNORM_BETA = (8 * DEPTH) ** -0.25
LN_EPS = 1e-5

kernel_name = 'hybrid_rwkv7_gla_moe_deepnorm'


def layer_norm(x, w, b):
    xf = x.astype(jnp.float32)
    mu = xf.mean(-1, keepdims=True)
    var = jnp.square(xf - mu).mean(-1, keepdims=True)
    y = (xf - mu) * lax.rsqrt(var + LN_EPS) * w.astype(jnp.float32) + b.astype(jnp.float32)
    return y.astype(x.dtype)


def rwkv7_recurrence(r, decay, k, v, kk, a):
    B, S, H, N = r.shape

    def step(state, inp):
        r_t, w_t, k_t, v_t, kk_t, a_t = inp
        sa = jnp.einsum('bhvk,bhk->bhv', state, -kk_t)
        state = (state * w_t[:, :, None, :]
                 + sa[..., None] * (kk_t * a_t)[:, :, None, :]
                 + v_t[..., None] * k_t[:, :, None, :])
        return state, jnp.einsum('bhvk,bhk->bhv', state, r_t)

    xs = tuple(jnp.moveaxis(t, 1, 0) for t in (r, decay, k, v, kk, a))
    _, y = lax.scan(step, jnp.zeros((B, H, N, N), jnp.float32), xs)
    return jnp.moveaxis(y, 0, 1)


def rwkv7_mixer(p, mu, w0, w_up, a0, a_up, g_up, k_k, k_a, r_k, gn_w, gn_b, v_first, vres):
    B, S, _ = p.shape
    dtype = p.dtype
    prev = jnp.pad(p, ((0, 0), (1, 0), (0, 0)))[:, :S]
    p = p + (prev - p) * mu
    cut = [D_RWKV, 2 * D_RWKV, 3 * D_RWKV, 3 * D_RWKV + W_LORA, 3 * D_RWKV + W_LORA + A_LORA]
    r, k, v, wd, ad, gd = jnp.split(p, cut, axis=-1)
    w = -jax.nn.softplus(-(w0 + jnp.tanh(wd) @ w_up).astype(jnp.float32)) - 0.5
    decay = jnp.exp(-jnp.exp(w))
    a = jax.nn.sigmoid(a0 + ad @ a_up)
    g = jax.nn.sigmoid(gd) @ g_up
    if vres is None:
        v_first = v
    else:
        v0, v_down, v_up = vres
        v = v + (v_first - v) * jax.nn.sigmoid(v0 + (v @ v_down) @ v_up)

    def heads(t):
        return t.reshape(B, S, RWKV_HEADS, RWKV_HEAD).astype(jnp.float32)

    def hp(t):
        return t.reshape(RWKV_HEADS, RWKV_HEAD).astype(jnp.float32)

    r, k, v, decay, a = heads(r), heads(k), heads(v), heads(decay), heads(a)
    kk = k * hp(k_k)
    kk = kk / jnp.maximum(jnp.sqrt(jnp.sum(kk * kk, -1, keepdims=True)), 1e-12)
    k = k * (1.0 + (a - 1.0) * hp(k_a))
    y = rwkv7_recurrence(r, decay, k, v, kk, a)
    m = y.mean(-1, keepdims=True)
    var = jnp.square(y - m).mean(-1, keepdims=True)
    y = (y - m) * lax.rsqrt(var + RWKV_GN_EPS) * hp(gn_w) + hp(gn_b)
    y = y + jnp.sum(r * k * hp(r_k), -1, keepdims=True) * v
    return y.reshape(B, S, D_RWKV).astype(dtype) * g, v_first


def gla_chunked(q, k, v, gk):
    B, S, H, DK = q.shape
    DV = v.shape[-1]
    NC = S // CHUNK

    def blocks(t):
        return t.reshape(B, NC, CHUNK, H, t.shape[-1]).transpose(0, 3, 1, 2, 4)

    q, k, v, gk = blocks(q), blocks(k), blocks(v), blocks(gk)
    b = jnp.cumsum(gk, axis=3)
    b_last = b[:, :, :, -1:, :]
    q_e = q * jnp.exp(b) * (DK ** -0.5)
    k_e = k * jnp.exp(-b)
    k_end = k * jnp.exp(b_last - b)
    causal = jnp.tril(jnp.ones((CHUNK, CHUNK), dtype=bool))
    scores = jnp.where(causal, jnp.einsum('bhncd,bhnsd->bhncs', q_e, k_e), 0.0)
    o_intra = jnp.einsum('bhncs,bhnsv->bhncv', scores, v)
    kv_chunk = jnp.einsum('bhncd,bhncv->bhndv', k_end, v)
    decay_chunk = jnp.exp(b_last[:, :, :, 0, :])

    def step(state, inp):
        kv_n, dec_n = inp
        return state * dec_n[..., None] + kv_n, state

    _, s_prev = lax.scan(step, jnp.zeros((B, H, DK, DV), jnp.float32),
                         (jnp.moveaxis(kv_chunk, 2, 0), jnp.moveaxis(decay_chunk, 2, 0)))
    s_prev = jnp.moveaxis(s_prev, 0, 2)
    o = o_intra + jnp.einsum('bhncd,bhndv->bhncv', q_e, s_prev)
    return o.transpose(0, 2, 3, 1, 4).reshape(B, S, H, DV)


def gla_mixer(p, gk_up, gk_b, norm_w):
    B, S, _ = p.shape
    hk = GLA_HEADS * GLA_DK
    q, k, v, gkd, g = jnp.split(p, [hk, 2 * hk, 2 * hk + D_GLA, 2 * hk + D_GLA + GLA_GATE_RANK], axis=-1)
    gk = jax.nn.log_sigmoid((gkd @ gk_up + gk_b).astype(jnp.float32)) / GLA_GATE_NORMALIZER

    def heads(t, d):
        return t.reshape(B, S, GLA_HEADS, d).astype(jnp.float32)

    o = gla_chunked(heads(q, GLA_DK), heads(k, GLA_DK), heads(v, GLA_DV), heads(gk, GLA_DK))
    o = o * lax.rsqrt(jnp.mean(o * o, -1, keepdims=True) + RMS_EPS) * norm_w.astype(jnp.float32)
    return o.reshape(B, S, D_GLA).astype(p.dtype) * jax.nn.silu(g)


def moe_ffn(x, router_w, router_b, w1, b1, w2, b2):
    B, S, D = x.shape
    T = B * S
    xt = x.reshape(T, D)
    logits = (xt @ router_w + router_b).astype(jnp.float32)
    top_val, top_idx = lax.top_k(logits, TOP_K)
    gate = jax.nn.softmax(top_val, axis=-1)
    M = T * TOP_K
    flat_e = top_idx.reshape(M)
    flat_tok = jnp.arange(M, dtype=jnp.int32) // TOP_K
    flat_gate = gate.reshape(M)
    order = jnp.argsort(flat_e)
    e_sorted = flat_e[order]
    counts = jnp.bincount(flat_e, length=N_EXPERTS)
    padded = ((counts + MOE_BLOCK - 1) // MOE_BLOCK) * MOE_BLOCK
    start = jnp.cumsum(counts) - counts
    pend = jnp.cumsum(padded)
    pstart = pend - padded
    dest = pstart[e_sorted] + jnp.arange(M, dtype=jnp.int32) - start[e_sorted]
    n_blocks = -(-M // MOE_BLOCK) + N_EXPERTS
    m_pad = n_blocks * MOE_BLOCK
    row_tok = jnp.zeros((m_pad,), jnp.int32).at[dest].set(flat_tok[order])
    row_gate = jnp.zeros((m_pad,), jnp.float32).at[dest].set(flat_gate[order])
    block_e = jnp.minimum(
        jnp.searchsorted(pend, jnp.arange(n_blocks, dtype=jnp.int32) * MOE_BLOCK, side='right'),
        N_EXPERTS - 1)

    def body(y, inp):
        tok, g, e = inp
        h = xt[tok] @ w1[e] + b1[e]
        gt = jnp.minimum(h[:, :D_FF], SWIGLU_LIMIT)
        up = jnp.clip(h[:, D_FF:], -SWIGLU_LIMIT, SWIGLU_LIMIT)
        act = (up + 1.0) * gt * jax.nn.sigmoid(SWIGLU_ALPHA * gt)
        o = act @ w2[e] + b2[e]
        return y.at[tok].add(o * g[:, None].astype(o.dtype)), None

    y, _ = lax.scan(body, jnp.zeros_like(xt),
                    (row_tok.reshape(n_blocks, MOE_BLOCK), row_gate.reshape(n_blocks, MOE_BLOCK), block_e))
    return y.reshape(B, S, D)


def setup_inputs(seed: int = 0) -> dict:
    key = jax.random.key(seed)
    ks = iter(jax.random.split(key, 48))

    def nrm(shape, scale):
        return scale * jax.random.normal(next(ks), shape, jnp.float32)

    def uni(shape, lo, hi):
        return jax.random.uniform(next(ks), shape, jnp.float32, lo, hi)

    L = DEPTH
    LV = DEPTH - 1
    return {
        'x': nrm((BATCH, SEQ, D_MODEL), 1.0),
        'ln_in_w': 1.0 + nrm((D_MODEL,), 0.02),
        'ln_in_b': nrm((D_MODEL,), 0.02),
        'w_in': nrm((L, D_MODEL, D_IN), D_MODEL ** -0.5),
        'rwkv_mu': uni((L, D_RWKV_IN), 0.0, 1.0),
        'rwkv_w0': uni((L, D_RWKV), -6.0, 1.0),
        'rwkv_w_up': nrm((L, W_LORA, D_RWKV), W_LORA ** -0.5),
        'rwkv_a0': nrm((L, D_RWKV), 0.1),
        'rwkv_a_up': nrm((L, A_LORA, D_RWKV), A_LORA ** -0.5),
        'rwkv_g_up': nrm((L, G_LORA, D_RWKV), G_LORA ** -0.5),
        'rwkv_k_k': 0.85 + nrm((L, D_RWKV), 0.05),
        'rwkv_k_a': 1.0 + nrm((L, D_RWKV), 0.05),
        'rwkv_r_k': nrm((L, D_RWKV), 0.1),
        'rwkv_gn_w': 1.0 + nrm((L, D_RWKV), 0.02),
        'rwkv_gn_b': nrm((L, D_RWKV), 0.02),
        'rwkv_v0': nrm((LV, D_RWKV), 0.5),
        'rwkv_v_down': nrm((LV, D_RWKV, V_LORA), D_RWKV ** -0.5),
        'rwkv_v_up': nrm((LV, V_LORA, D_RWKV), V_LORA ** -0.5),
        'gla_gk_up': nrm((L, GLA_GATE_RANK, GLA_HEADS * GLA_DK), GLA_GATE_RANK ** -0.5),
        'gla_gk_b': nrm((L, GLA_HEADS * GLA_DK), 0.5),
        'gla_norm_w': 1.0 + nrm((L, GLA_DV), 0.02),
        'w_out': nrm((L, D_MIX, D_MODEL), DEEPNORM_BETA * D_MIX ** -0.5),
        'ln1_w': 1.0 + nrm((L, D_MODEL), 0.02),
        'ln1_b': nrm((L, D_MODEL), 0.02),
        'router_w': nrm((L, D_MODEL, N_EXPERTS), D_MODEL ** -0.5),
        'router_b': nrm((L, N_EXPERTS), 0.01),
        'exp_w1': nrm((L, N_EXPERTS, D_MODEL, 2 * D_FF), D_MODEL ** -0.5),
        'exp_b1': nrm((L, N_EXPERTS, 2 * D_FF), 0.02),
        'exp_w2': nrm((L, N_EXPERTS, D_FF, D_MODEL), DEEPNORM_BETA * D_FF ** -0.5),
        'exp_b2': nrm((L, N_EXPERTS, D_MODEL), 0.02),
        'ln2_w': 1.0 + nrm((L, D_MODEL), 0.02),
        'ln2_b': nrm((L, D_MODEL), 0.02),
    }


def reference(x, ln_in_w, ln_in_b, w_in, rwkv_mu, rwkv_w0, rwkv_w_up, rwkv_a0, rwkv_a_up, rwkv_g_up,
              rwkv_k_k, rwkv_k_a, rwkv_r_k, rwkv_gn_w, rwkv_gn_b, rwkv_v0, rwkv_v_down, rwkv_v_up,
              gla_gk_up, gla_gk_b, gla_norm_w, w_out, ln1_w, ln1_b, router_w, router_b,
              exp_w1, exp_b1, exp_w2, exp_b2, ln2_w, ln2_b):
    x = layer_norm(x, ln_in_w, ln_in_b)
    v_first = None
    for l in range(DEPTH):
        p = x @ w_in[l]
        p_rwkv = p[..., :D_RWKV_IN]
        p_gla = p[..., D_RWKV_IN:]
        vres = None if l == 0 else (rwkv_v0[l - 1], rwkv_v_down[l - 1], rwkv_v_up[l - 1])
        y_rwkv, v_first = rwkv7_mixer(p_rwkv, rwkv_mu[l], rwkv_w0[l], rwkv_w_up[l], rwkv_a0[l],
                                      rwkv_a_up[l], rwkv_g_up[l], rwkv_k_k[l], rwkv_k_a[l],
                                      rwkv_r_k[l], rwkv_gn_w[l], rwkv_gn_b[l], v_first, vres)
        y_gla = gla_mixer(p_gla, gla_gk_up[l], gla_gk_b[l], gla_norm_w[l])
        mix = jnp.concatenate([y_rwkv, y_gla], axis=-1) @ w_out[l]
        x = layer_norm(DEEPNORM_ALPHA * x + mix, ln1_w[l], ln1_b[l])
        ffn = moe_ffn(x, router_w[l], router_b[l], exp_w1[l], exp_b1[l], exp_w2[l], exp_b2[l])
        x = layer_norm(DEEPNORM_ALPHA * x + ffn, ln2_w[l], ln2_b[l])
    return x
```

```python
import functools

import jax
import jax.numpy as jnp
from jax import lax
from jax.experimental import pallas as pl
from jax.experimental.pallas import tpu as pltpu

F32 = jnp.float32
BF16 = jnp.bfloat16
HIGHEST = lax.Precision.HIGHEST

D_MODEL = 1024
DEPTH = 2
CHUNK = 64
RWKV_HEAD = 64
D_RWKV = 512
RWKV_HEADS = 8
W_LORA = 64
A_LORA = 64
V_LORA = 32
G_LORA = 160
RWKV_GN_EPS = 64e-5
D_GLA = 512
GLA_HEADS = 4
GLA_DV = 128
GLA_DK = 64
GLA_GATE_RANK = 16
GLA_GATE_NORMALIZER = 16.0
RMS_EPS = 1e-6
N_EXPERTS = 32
TOP_K = 4
D_FF = 1024
SWIGLU_LIMIT = 7.0
SWIGLU_ALPHA = 1.702
MOE_BLOCK = 256
DEEPNORM_ALPHA = (2 * DEPTH) ** 0.25
LN_EPS = 1e-5

LANE = 128
RW_WD = 3 * D_RWKV
RW_AD = RW_WD + LANE
RW_GD = RW_AD + LANE
RW_WIDTH = RW_GD + 2 * LANE
GL_K = GLA_HEADS * GLA_DK
GL_V = 2 * GL_K
GL_GK = GL_V + D_GLA
GL_G = GL_GK + LANE
GL_WIDTH = GL_G + D_GLA

VMEM_LIMIT = 56 * 1024 * 1024


def _cparams(n_axes):
    return pltpu.CompilerParams(dimension_semantics=("arbitrary",) * n_axes,
                                vmem_limit_bytes=VMEM_LIMIT)


def _dot(a, b):
    return jnp.dot(a, b, preferred_element_type=F32)


def _dot_nt(a, b):
    return lax.dot_general(a, b, (((1,), (1,)), ((), ())), preferred_element_type=F32)


def _dot_tn(a, b):
    return lax.dot_general(a, b, (((0,), (0,)), ((), ())), preferred_element_type=F32)


def _split_bf16(x):
    hi = x.astype(BF16)
    lo = (x - hi.astype(F32)).astype(BF16)
    return hi, lo


def _layer_norm(z, w, b):
    mu = jnp.mean(z, axis=-1, keepdims=True)
    zc = z - mu
    var = jnp.mean(zc * zc, axis=-1, keepdims=True)
    return zc * lax.rsqrt(var + LN_EPS) * w + b


def _sigmoid(x):
    return 1.0 / (1.0 + jnp.exp(-x))


def _softplus(x):
    return jnp.maximum(x, 0.0) + jnp.log(1.0 + jnp.exp(-jnp.abs(x)))


def _tril_mask(n, strict):
    r = lax.broadcasted_iota(jnp.int32, (n, n), 0)
    c = lax.broadcasted_iota(jnp.int32, (n, n), 1)
    return (c < r) if strict else (c <= r)


def _inproj_kernel(apply_ln, x_ref, lnw_ref, lnb_ref, wr_ref, wg_ref, *out_refs):
    x = x_ref[...]
    if apply_ln:
        x0_ref, pr_ref, pg_ref = out_refs
        x = _layer_norm(x, lnw_ref[...], lnb_ref[...])
        x0_ref[...] = x
    else:
        pr_ref, pg_ref = out_refs
    xb = x.astype(BF16)
    pr_ref[...] = _dot(xb, wr_ref[...]).astype(BF16)
    pg_ref[...] = _dot(xb, wg_ref[...]).astype(BF16)


def _inproj(x, lnw, lnb, wr, wg, apply_ln):
    t = x.shape[0]
    tm = min(512, t)
    row = lambda i: (i, 0)
    const = lambda i: (0, 0)
    out_shape = [jax.ShapeDtypeStruct((t, RW_WIDTH), BF16), jax.ShapeDtypeStruct((t, GL_WIDTH), BF16)]
    out_specs = [pl.BlockSpec((tm, RW_WIDTH), row), pl.BlockSpec((tm, GL_WIDTH), row)]
    if apply_ln:
        out_shape = [jax.ShapeDtypeStruct((t, D_MODEL), F32)] + out_shape
        out_specs = [pl.BlockSpec((tm, D_MODEL), row)] + out_specs
    return pl.pallas_call(
        functools.partial(_inproj_kernel, apply_ln),
        grid=(t // tm,),
        in_specs=[pl.BlockSpec((tm, D_MODEL), row),
                  pl.BlockSpec((1, D_MODEL), const), pl.BlockSpec((1, D_MODEL), const),
                  pl.BlockSpec((D_MODEL, RW_WIDTH), const), pl.BlockSpec((D_MODEL, GL_WIDTH), const)],
        out_specs=out_specs,
        out_shape=out_shape,
        compiler_params=_cparams(1),
        name="inproj_ln" if apply_ln else "inproj",
    )(x, lnw, lnb, wr, wg)


def _head_sum(x, ones_blk):
    hi, lo = _split_bf16(x)
    return _dot(hi, ones_blk) + _dot(lo, ones_blk)


def _unit_lower_inverse(a_strict):
    n = a_strict.shape[0]
    r = lax.broadcasted_iota(jnp.int32, (n, n), 0)
    c = lax.broadcasted_iota(jnp.int32, (n, n), 1)
    eye = (r == c).astype(F32)
    d = eye + jnp.where((r // 2) == (c // 2), a_strict, 0.0)
    size = 2
    while size < n:
        off = jnp.where(((r // (2 * size)) == (c // (2 * size))) & ((r // size) != (c // size)), a_strict, 0.0)
        db = d.astype(BF16)
        m = _dot(off.astype(BF16), db)
        d = d + _dot(db, m.astype(BF16))
        size *= 2
    return d


def _rwkv_kernel(has_vres, *refs):
    if has_vres:
        (p_ref, vfirst_ref, mu_ref, w0_ref, wup_ref, a0_ref, aup_ref, gup_ref, kk_ref, ka_ref, rk_ref,
         gnw_ref, gnb_ref, v0_ref, vdown_ref, vup_ref, y_ref, state_ref, prev_ref) = refs
    else:
        (p_ref, mu_ref, w0_ref, wup_ref, a0_ref, aup_ref, gup_ref, kk_ref, ka_ref, rk_ref,
         gnw_ref, gnb_ref, y_ref, vfirst_out_ref, state_ref, prev_ref) = refs

    @pl.when(pl.program_id(1) == 0)
    def _():
        state_ref[...] = jnp.zeros_like(state_ref)
        prev_ref[...] = jnp.zeros_like(prev_ref)

    p = p_ref[0].astype(F32)
    n_tok = p.shape[0]
    row = lax.broadcasted_iota(jnp.int32, (n_tok, 1), 0)
    prev = jnp.where(row == 0, prev_ref[...], pltpu.roll(p, 1, axis=0))
    prev_ref[...] = p[n_tok - 1:n_tok, :]
    xs = p + (prev - p) * mu_ref[...]
    r = xs[:, 0:D_RWKV]
    k = xs[:, D_RWKV:2 * D_RWKV]
    v = xs[:, 2 * D_RWKV:3 * D_RWKV]
    wd = xs[:, RW_WD:RW_AD]
    ad = xs[:, RW_AD:RW_GD]
    gd = xs[:, RW_GD:RW_WIDTH]

    w = w0_ref[...] + _dot(jnp.tanh(wd).astype(BF16), wup_ref[...])
    w = -_softplus(-w) - 0.5
    logw = -jnp.exp(w)
    a = _sigmoid(a0_ref[...] + _dot(ad.astype(BF16), aup_ref[...]))
    g = _dot(_sigmoid(gd).astype(BF16), gup_ref[...])
    if has_vres:
        vmix = _dot(_dot(v.astype(BF16), vdown_ref[...]).astype(BF16), vup_ref[...])
        v = v + (vfirst_ref[0] - v) * _sigmoid(v0_ref[...] + vmix)
    else:
        vfirst_out_ref[0] = v

    hr = lax.broadcasted_iota(jnp.int32, (D_RWKV, D_RWKV), 0) // RWKV_HEAD
    hc = lax.broadcasted_iota(jnp.int32, (D_RWKV, D_RWKV), 1) // RWKV_HEAD
    ones_blk = (hr == hc).astype(BF16)

    kk = k * kk_ref[...]
    kk = kk * lax.rsqrt(jnp.maximum(_head_sum(kk * kk, ones_blk), 1e-24))
    k = k * (1.0 + (a - 1.0) * ka_ref[...])
    bonus = _head_sum(r * k * rk_ref[...], ones_blk)

    lower = _tril_mask(CHUNK, strict=False)
    strict = _tril_mask(CHUNK, strict=True)
    tril_f = lower.astype(F32)

    y_chunks = []
    for c in range(n_tok // CHUNK):
        sl = slice(c * CHUNK, (c + 1) * CHUNK)
        lw = logw[sl]
        b = jnp.dot(tril_f, lw, precision=HIGHEST, preferred_element_type=F32)
        eb = jnp.exp(b)
        enb = jnp.exp(-b)
        ebx = jnp.exp(b - lw)
        rh = (r[sl] * eb).astype(BF16)
        kh = (k[sl] * enb).astype(BF16)
        ah = (-kk[sl] * ebx).astype(BF16)
        bh = (kk[sl] * a[sl] * enb).astype(BF16)
        vb = v[sl].astype(BF16)
        g_end = eb[CHUNK - 1:CHUNK, :]
        y_heads = []
        for h in range(RWKV_HEADS):
            hs = slice(h * RWKV_HEAD, (h + 1) * RWKV_HEAD)
            ah_h, bh_h, rh_h, kh_h, v_h = ah[:, hs], bh[:, hs], rh[:, hs], kh[:, hs], vb[:, hs]
            ar = jnp.concatenate([ah_h, rh_h], axis=0)
            s_b = _dot_nt(ar, bh_h)
            s_k = _dot_nt(ar, kh_h)
            a_ab = jnp.where(strict, s_b[:CHUNK], 0.0)
            a_ak = jnp.where(strict, s_k[:CHUNK], 0.0)
            a_rb = jnp.where(lower, s_b[CHUNK:], 0.0)
            a_rk = jnp.where(lower, s_k[CHUNK:], 0.0)
            tinv = _unit_lower_inverse(a_ab).astype(BF16)
            akv = _dot(a_ak.astype(BF16), v_h)
            wmat = _dot(tinv, ah_h)
            ut = _dot(tinv, akv.astype(BF16))
            s_old = state_ref[h]
            ws = _dot_nt(jnp.concatenate([wmat.astype(BF16), rh_h], axis=0), s_old.astype(BF16))
            u = ws[:CHUNK] + ut
            ub = u.astype(BF16)
            y_h = ws[CHUNK:] + _dot(a_rb.astype(BF16), ub) + _dot(a_rk.astype(BF16), v_h)
            s_new = (s_old + _dot_tn(ub, bh_h) + _dot_tn(v_h, kh_h)) * g_end[:, hs]
            state_ref[h] = s_new
            y_heads.append(y_h)
        y_chunks.append(jnp.concatenate(y_heads, axis=1))
    y = jnp.concatenate(y_chunks, axis=0) if len(y_chunks) > 1 else y_chunks[0]

    inv_n = 1.0 / RWKV_HEAD
    m = _head_sum(y, ones_blk) * inv_n
    yc = y - m
    var = _head_sum(yc * yc, ones_blk) * inv_n
    y = yc * lax.rsqrt(var + RWKV_GN_EPS) * gnw_ref[...] + gnb_ref[...]
    y = y + bonus * v
    y_ref[0] = (y * g).astype(BF16)


def _rwkv_mixer(p_r, vfirst, prm, has_vres):
    bsz, seq, _ = p_r.shape
    n_tok = min(128, seq)
    tok = lambda b, i: (b, i, 0)
    const = lambda b, i: (0, 0)
    vec = pl.BlockSpec((1, D_RWKV), const)
    in_specs = [pl.BlockSpec((1, n_tok, RW_WIDTH), tok)]
    args = [p_r]
    if has_vres:
        in_specs.append(pl.BlockSpec((1, n_tok, D_RWKV), tok))
        args.append(vfirst)
    in_specs += [pl.BlockSpec((1, RW_WIDTH), const), vec, pl.BlockSpec((LANE, D_RWKV), const), vec,
                 pl.BlockSpec((LANE, D_RWKV), const), pl.BlockSpec((2 * LANE, D_RWKV), const),
                 vec, vec, vec, vec, vec]
    args += [prm["mu"], prm["w0"], prm["w_up"], prm["a0"], prm["a_up"], prm["g_up"],
             prm["k_k"], prm["k_a"], prm["r_k"], prm["gn_w"], prm["gn_b"]]
    out_shape = [jax.ShapeDtypeStruct((bsz, seq, D_RWKV), BF16)]
    out_specs = [pl.BlockSpec((1, n_tok, D_RWKV), tok)]
    if has_vres:
        in_specs += [vec, pl.BlockSpec((D_RWKV, LANE), const), pl.BlockSpec((LANE, D_RWKV), const)]
        args += [prm["v0"], prm["v_down"], prm["v_up"]]
    else:
        out_shape.append(jax.ShapeDtypeStruct((bsz, seq, D_RWKV), F32))
        out_specs.append(pl.BlockSpec((1, n_tok, D_RWKV), tok))
    res = pl.pallas_call(
        functools.partial(_rwkv_kernel, has_vres),
        grid=(bsz, seq // n_tok),
        in_specs=in_specs,
        out_specs=out_specs,
        out_shape=out_shape,
        scratch_shapes=[pltpu.VMEM((RWKV_HEADS, RWKV_HEAD, RWKV_HEAD), F32),
                        pltpu.VMEM((1, RW_WIDTH), F32)],
        compiler_params=_cparams(2),
        name="rwkv7_vres" if has_vres else "rwkv7",
    )(*args)
    if has_vres:
        return res[0], vfirst
    return res[0], res[1]


def _gla_kernel(p_ref, gkup_ref, gkb_ref, nw_ref, y_ref, state_ref):
    @pl.when(pl.program_id(1) == 0)
    def _():
        state_ref[...] = jnp.zeros_like(state_ref)

    p = p_ref[0].astype(F32)
    n_tok = p.shape[0]
    q = p[:, 0:GL_K]
    k = p[:, GL_K:GL_V]
    v = p[:, GL_V:GL_GK]
    gkd = p[:, GL_GK:GL_G]
    g = p[:, GL_G:GL_WIDTH]
    z = _dot(gkd.astype(BF16), gkup_ref[...]) + gkb_ref[...]
    gk = -_softplus(-z) * (1.0 / GLA_GATE_NORMALIZER)

    lower = _tril_mask(CHUNK, strict=False)
    tril_f = lower.astype(F32)
    o_chunks = []
    for c in range(n_tok // CHUNK):
        sl = slice(c * CHUNK, (c + 1) * CHUNK)
        b = jnp.dot(tril_f, gk[sl], precision=HIGHEST, preferred_element_type=F32)
        b_last = b[CHUNK - 1:CHUNK, :]
        q_e = (q[sl] * jnp.exp(b) * (GLA_DK ** -0.5)).astype(BF16)
        k_e = (k[sl] * jnp.exp(-b)).astype(BF16)
        k_end = (k[sl] * jnp.exp(b_last - b)).astype(BF16)
        dec = jnp.exp(b_last)
        vb = v[sl].astype(BF16)
        o_heads = []
        for h in range(GLA_HEADS):
            ks = slice(h * GLA_DK, (h + 1) * GLA_DK)
            vs = slice(h * GLA_DV, (h + 1) * GLA_DV)
            scores = jnp.where(lower, _dot_nt(q_e[:, ks], k_e[:, ks]), 0.0)
            s_old = state_ref[h]
            o_h = _dot(scores.astype(BF16), vb[:, vs]) + _dot_nt(q_e[:, ks], s_old.astype(BF16))
            state_ref[h] = s_old * dec[:, ks] + _dot_tn(vb[:, vs], k_end[:, ks])
            o_h = o_h * lax.rsqrt(jnp.mean(o_h * o_h, axis=-1, keepdims=True) + RMS_EPS)
            o_heads.append(o_h)
        o_chunks.append(jnp.concatenate(o_heads, axis=1))
    o = jnp.concatenate(o_chunks, axis=0) if len(o_chunks) > 1 else o_chunks[0]
    silu_g = g * _sigmoid(g)
    y_ref[0] = (o * nw_ref[...] * silu_g).astype(BF16)


def _gla_mixer(p_g, prm):
    bsz, seq, _ = p_g.shape
    n_tok = min(128, seq)
    tok = lambda b, i: (b, i, 0)
    const = lambda b, i: (0, 0)
    return pl.pallas_call(
        _gla_kernel,
        grid=(bsz, seq // n_tok),
        in_specs=[pl.BlockSpec((1, n_tok, GL_WIDTH), tok),
                  pl.BlockSpec((LANE, GL_K), const), pl.BlockSpec((1, GL_K), const),
                  pl.BlockSpec((1, D_GLA), const)],
        out_specs=pl.BlockSpec((1, n_tok, D_GLA), tok),
        out_shape=jax.ShapeDtypeStruct((bsz, seq, D_GLA), BF16),
        scratch_shapes=[pltpu.VMEM((GLA_HEADS, GLA_DV, GLA_DK), F32)],
        compiler_params=_cparams(2),
        name="gla",
    )(p_g, prm["gk_up"], prm["gk_b"], prm["norm_w"])


def _outproj_router_kernel(yr_ref, yg_ref, x_ref, wo_ref, lnw_ref, lnb_ref, rw_ref, rb_ref,
                           x1_ref, idx_ref, gate_ref, rank_ref, cnt_ref, carry_ref):
    @pl.when(pl.program_id(0) == 0)
    def _():
        carry_ref[...] = jnp.zeros_like(carry_ref)

    mix = _dot(yr_ref[...], wo_ref[0:D_RWKV, :]) + _dot(yg_ref[...], wo_ref[D_RWKV:, :])
    x1 = _layer_norm(DEEPNORM_ALPHA * x_ref[...] + mix, lnw_ref[...], lnb_ref[...])
    x1_ref[...] = x1
    tm = x1.shape[0]

    xh, xl = _split_bf16(x1)
    rw = rw_ref[...]
    wh, wl = _split_bf16(rw)
    logits = _dot_nt(wh, xh) + _dot_nt(wh, xl) + _dot_nt(wl, xh) + rb_ref[...]

    e_iota = lax.broadcasted_iota(jnp.int32, (N_EXPERTS, tm), 0)
    work = logits
    vals, idxs = [], []
    member = jnp.zeros((N_EXPERTS, tm), F32)
    for _ in range(TOP_K):
        mx = jnp.max(work, axis=0, keepdims=True)
        ix = jnp.min(jnp.where(work == mx, e_iota, N_EXPERTS), axis=0, keepdims=True)
        sel = e_iota == ix
        work = jnp.where(sel, -jnp.inf, work)
        member = jnp.where(sel, 1.0, member)
        vals.append(mx)
        idxs.append(ix)
    exps = [jnp.exp(vv - vals[0]) for vv in vals]
    inv_den = 1.0 / (exps[0] + exps[1] + exps[2] + exps[3])

    tr = lax.broadcasted_iota(jnp.int32, (tm, tm), 0)
    tc = lax.broadcasted_iota(jnp.int32, (tm, tm), 1)
    before = (tr < tc).astype(BF16)
    cex = _dot(member.astype(BF16), before) + carry_ref[...][:, 0:1]
    for kq in range(TOP_K):
        sel = e_iota == idxs[kq]
        idx_ref[kq:kq + 1, :] = idxs[kq]
        gate_ref[kq:kq + 1, :] = exps[kq] * inv_den
        rank_ref[kq:kq + 1, :] = jnp.sum(jnp.where(sel, cex, 0.0), axis=0, keepdims=True).astype(jnp.int32)
    carry_ref[...] = carry_ref[...] + jnp.sum(member, axis=1, keepdims=True)
    cnt_ref[...] = carry_ref[...].astype(jnp.int32)


def _outproj_router(yr, yg, x, wo, lnw, lnb, rw_t, rb):
    t = x.shape[0]
    tm = min(512, t)
    row = lambda i: (i, 0)
    col = lambda i: (0, i)
    const = lambda i: (0, 0)
    return pl.pallas_call(
        _outproj_router_kernel,
        grid=(t // tm,),
        in_specs=[pl.BlockSpec((tm, D_RWKV), row), pl.BlockSpec((tm, D_GLA), row),
                  pl.BlockSpec((tm, D_MODEL), row), pl.BlockSpec((D_MODEL, D_MODEL), const),
                  pl.BlockSpec((1, D_MODEL), const), pl.BlockSpec((1, D_MODEL), const),
                  pl.BlockSpec((N_EXPERTS, D_MODEL), const), pl.BlockSpec((N_EXPERTS, 1), const)],
        out_specs=[pl.BlockSpec((tm, D_MODEL), row),
                   pl.BlockSpec((TOP_K, tm), col), pl.BlockSpec((TOP_K, tm), col),
                   pl.BlockSpec((TOP_K, tm), col), pl.BlockSpec((N_EXPERTS, LANE), const)],
        out_shape=[jax.ShapeDtypeStruct((t, D_MODEL), F32),
                   jax.ShapeDtypeStruct((TOP_K, t), jnp.int32), jax.ShapeDtypeStruct((TOP_K, t), F32),
                   jax.ShapeDtypeStruct((TOP_K, t), jnp.int32),
                   jax.ShapeDtypeStruct((N_EXPERTS, LANE), jnp.int32)],
        scratch_shapes=[pltpu.VMEM((N_EXPERTS, LANE), F32)],
        compiler_params=_cparams(1),
        name="outproj_router",
    )(yr, yg, x, wo, lnw, lnb, rw_t, rb)


def _dispatch_kernel(dest_ref, gaps_ref, x_ref, xs_ref, zero_ref, sem, zsem):
    te = x_ref.shape[0]
    base = pl.program_id(0) * (TOP_K * te)

    @pl.when(pl.program_id(0) == 0)
    def _():
        zero_ref[...] = jnp.zeros_like(zero_ref)

        def zero_copy(d):
            return pltpu.make_async_copy(zero_ref, xs_ref.at[pl.ds(d, 1), :], zsem)

        for g in range(N_EXPERTS + 1):
            lo = gaps_ref[2 * g]
            hi = gaps_ref[2 * g + 1]
            lax.fori_loop(lo, hi, lambda d, c: (zero_copy(d).start(), c)[1], 0)
        for g in range(N_EXPERTS + 1):
            lo = gaps_ref[2 * g]
            hi = gaps_ref[2 * g + 1]
            lax.fori_loop(lo, hi, lambda d, c: (zero_copy(d).wait(), c)[1], 0)

    def row_copy(r, kq):
        d = dest_ref[base + kq * te + r]
        return pltpu.make_async_copy(x_ref.at[pl.ds(r, 1), :], xs_ref.at[pl.ds(d, 1), :], sem)

    def issue(r, carry):
        for kq in range(TOP_K):
            row_copy(r, kq).start()
        return carry

    def drain(r, carry):
        for kq in range(TOP_K):
            row_copy(r, kq).wait()
        return carry

    lax.fori_loop(0, te, issue, 0)
    lax.fori_loop(0, te, drain, 0)


def _dispatch(x1, dest_tiles, gaps, m_pad, te):
    t = x1.shape[0]
    return pl.pallas_call(
        _dispatch_kernel,
        grid_spec=pltpu.PrefetchScalarGridSpec(
            num_scalar_prefetch=2,
            grid=(t // te,),
            in_specs=[pl.BlockSpec((te, D_MODEL), lambda i, d, g: (i, 0))],
            out_specs=pl.BlockSpec(memory_space=pl.ANY),
            scratch_shapes=[pltpu.VMEM((1, D_MODEL), F32), pltpu.SemaphoreType.DMA(()),
                            pltpu.SemaphoreType.DMA(())],
        ),
        out_shape=jax.ShapeDtypeStruct((m_pad, D_MODEL), F32),
        compiler_params=_cparams(1),
        name="moe_dispatch",
    )(dest_tiles, gaps, x1)


def _expert_kernel(be_ref, nv_ref, xs_ref, w1_ref, b1_ref, w2_ref, b2_ref, ys_ref, act_ref):
    j = pl.program_id(0)
    n_valid = nv_ref[j]

    @pl.when(n_valid == 0)
    def _():
        ys_ref[...] = jnp.zeros_like(ys_ref)

    @pl.when(n_valid > 0)
    def _():
        xb = xs_ref[...].astype(BF16)
        n_chunk = 256
        for c in range(D_FF // n_chunk):
            cs = slice(c * n_chunk, (c + 1) * n_chunk)
            us = slice(D_FF + c * n_chunk, D_FF + (c + 1) * n_chunk)
            gt = _dot(xb, w1_ref[0, :, cs]) + b1_ref[0, :, cs]
            up = _dot(xb, w1_ref[0, :, us]) + b1_ref[0, :, us]
            gt = jnp.minimum(gt, SWIGLU_LIMIT)
            up = jnp.clip(up, -SWIGLU_LIMIT, SWIGLU_LIMIT)
            act_ref[:, cs] = ((up + 1.0) * gt * _sigmoid(SWIGLU_ALPHA * gt)).astype(BF16)
        ys_ref[...] = _dot(act_ref[...], w2_ref[0]) + b2_ref[0]


def _experts(xs, block_e, n_valid, w1, b1, w2, b2):
    m_pad = xs.shape[0]
    n_blocks = m_pad // MOE_BLOCK
    return pl.pallas_call(
        _expert_kernel,
        grid_spec=pltpu.PrefetchScalarGridSpec(
            num_scalar_prefetch=2,
            grid=(n_blocks,),
            in_specs=[pl.BlockSpec((MOE_BLOCK, D_MODEL), lambda j, be, nv: (j, 0)),
                      pl.BlockSpec((1, D_MODEL, 2 * D_FF), lambda j, be, nv: (be[j], 0, 0)),
                      pl.BlockSpec((1, 1, 2 * D_FF), lambda j, be, nv: (be[j], 0, 0)),
                      pl.BlockSpec((1, D_FF, D_MODEL), lambda j, be, nv: (be[j], 0, 0)),
                      pl.BlockSpec((1, 1, D_MODEL), lambda j, be, nv: (be[j], 0, 0))],
            out_specs=pl.BlockSpec((MOE_BLOCK, D_MODEL), lambda j, be, nv: (j, 0)),
            scratch_shapes=[pltpu.VMEM((MOE_BLOCK, D_FF), BF16)],
        ),
        out_shape=jax.ShapeDtypeStruct((m_pad, D_MODEL), F32),
        compiler_params=_cparams(1),
        name="moe_experts",
    )(block_e, n_valid, xs, w1, b1, w2, b2)


def _combine_kernel(dest_ref, ys_ref, x1_ref, gate_ref, lnw_ref, lnb_ref, out_ref, buf_ref, sem):
    te = x1_ref.shape[0]
    base = pl.program_id(0) * (TOP_K * te)

    def row_copy(r, kq):
        d = dest_ref[base + kq * te + r]
        return pltpu.make_async_copy(ys_ref.at[pl.ds(d, 1), :], buf_ref.at[kq, pl.ds(r, 1), :], sem)

    def issue(r, carry):
        for kq in range(TOP_K):
            row_copy(r, kq).start()
        return carry

    def drain(r, carry):
        for kq in range(TOP_K):
            row_copy(r, kq).wait()
        return carry

    lax.fori_loop(0, te, issue, 0)
    lax.fori_loop(0, te, drain, 0)

    gates = gate_ref[...]
    z = DEEPNORM_ALPHA * x1_ref[...]
    for kq in range(TOP_K):
        z = z + buf_ref[kq] * gates[:, kq:kq + 1]
    out_ref[...] = _layer_norm(z, lnw_ref[...], lnb_ref[...])


def _combine(ys, dest_tiles, x1, gates_t, lnw, lnb, te):
    t = x1.shape[0]
    return pl.pallas_call(
        _combine_kernel,
        grid_spec=pltpu.PrefetchScalarGridSpec(
            num_scalar_prefetch=1,
            grid=(t // te,),
            in_specs=[pl.BlockSpec(memory_space=pl.ANY),
                      pl.BlockSpec((te, D_MODEL), lambda i, d: (i, 0)),
                      pl.BlockSpec((te, TOP_K), lambda i, d: (i, 0)),
                      pl.BlockSpec((1, D_MODEL), lambda i, d: (0, 0)),
                      pl.BlockSpec((1, D_MODEL), lambda i, d: (0, 0))],
            out_specs=pl.BlockSpec((te, D_MODEL), lambda i, d: (i, 0)),
            scratch_shapes=[pltpu.VMEM((TOP_K, te, D_MODEL), F32), pltpu.SemaphoreType.DMA(())],
        ),
        out_shape=jax.ShapeDtypeStruct((t, D_MODEL), F32),
        compiler_params=_cparams(1),
        name="moe_combine",
    )(dest_tiles, ys, x1, gates_t, lnw, lnb)


def _moe(x1, idx, gate, rank, counts, w1, b1, w2, b2, lnw, lnb):
    t = x1.shape[0]
    te = min(256, t)
    n_blocks = -(-(t * TOP_K) // MOE_BLOCK) + N_EXPERTS
    m_pad = n_blocks * MOE_BLOCK
    padded = ((counts + MOE_BLOCK - 1) // MOE_BLOCK) * MOE_BLOCK
    pend = jnp.cumsum(padded)
    pstart = pend - padded
    dest = pstart[idx] + rank
    dest_tiles = dest.reshape(TOP_K, t // te, te).transpose(1, 0, 2).reshape(-1)
    blk_start = jnp.arange(n_blocks, dtype=jnp.int32) * MOE_BLOCK
    block_e = jnp.minimum(jnp.searchsorted(pend, blk_start, side="right"), N_EXPERTS - 1).astype(jnp.int32)
    n_valid = jnp.clip(pstart[block_e] + counts[block_e] - blk_start, 0, MOE_BLOCK).astype(jnp.int32)

    gap_lo = jnp.concatenate([pstart + counts, pend[-1:]])
    gap_hi = jnp.concatenate([pend, jnp.full((1,), m_pad, pend.dtype)])
    gaps = jnp.stack([gap_lo, gap_hi], axis=1).reshape(-1).astype(jnp.int32)

    xs = _dispatch(x1, dest_tiles, gaps, m_pad, te)
    ys = _experts(xs, block_e, n_valid, w1, b1, w2, b2)
    return _combine(ys, dest_tiles, x1, gate.T, lnw, lnb, te)


def _pad_cols(w, width):
    return jnp.pad(w, ((0, 0), (0, width - w.shape[1])))


def _pad_rows(w, height):
    return jnp.pad(w, ((0, height - w.shape[0]), (0, 0)))


def _pack_inproj(w_in_l, mu_l):
    d3 = 3 * D_RWKV
    o_ad = d3 + W_LORA
    o_gd = o_ad + A_LORA
    o_gla = o_gd + G_LORA
    wr = jnp.concatenate([w_in_l[:, :d3],
                          _pad_cols(w_in_l[:, d3:o_ad], LANE),
                          _pad_cols(w_in_l[:, o_ad:o_gd], LANE),
                          _pad_cols(w_in_l[:, o_gd:o_gla], 2 * LANE)], axis=1)
    mu = mu_l[None, :]
    mu_p = jnp.concatenate([mu[:, :d3], _pad_cols(mu[:, d3:o_ad], LANE), _pad_cols(mu[:, o_ad:o_gd], LANE),
                            _pad_cols(mu[:, o_gd:o_gla], 2 * LANE)], axis=1)
    g = w_in_l[:, o_gla:]
    wg = jnp.concatenate([g[:, :GL_GK], _pad_cols(g[:, GL_GK:GL_GK + GLA_GATE_RANK], LANE),
                          g[:, GL_GK + GLA_GATE_RANK:]], axis=1)
    return wr.astype(BF16), wg.astype(BF16), mu_p


def kernel(x, ln_in_w, ln_in_b, w_in, rwkv_mu, rwkv_w0, rwkv_w_up, rwkv_a0, rwkv_a_up, rwkv_g_up, rwkv_k_k, rwkv_k_a, rwkv_r_k, rwkv_gn_w, rwkv_gn_b, rwkv_v0, rwkv_v_down, rwkv_v_up, gla_gk_up, gla_gk_b, gla_norm_w, w_out, ln1_w, ln1_b, router_w, router_b, exp_w1, exp_b1, exp_w2, exp_b2, ln2_w, ln2_b):
    bsz, seq, d = x.shape
    t = bsz * seq
    xf = x.reshape(t, d)
    vfirst = None
    for l in range(DEPTH):
        wr, wg, mu_p = _pack_inproj(w_in[l], rwkv_mu[l])
        if l == 0:
            xf, p_r, p_g = _inproj(xf, ln_in_w[None, :], ln_in_b[None, :], wr, wg, apply_ln=True)
        else:
            p_r, p_g = _inproj(xf, ln_in_w[None, :], ln_in_b[None, :], wr, wg, apply_ln=False)
        rprm = {
            "mu": mu_p, "w0": rwkv_w0[l][None, :], "w_up": _pad_rows(rwkv_w_up[l], LANE).astype(BF16),
            "a0": rwkv_a0[l][None, :], "a_up": _pad_rows(rwkv_a_up[l], LANE).astype(BF16),
            "g_up": _pad_rows(rwkv_g_up[l], 2 * LANE).astype(BF16),
            "k_k": rwkv_k_k[l][None, :], "k_a": rwkv_k_a[l][None, :], "r_k": rwkv_r_k[l][None, :],
            "gn_w": rwkv_gn_w[l][None, :], "gn_b": rwkv_gn_b[l][None, :],
        }
        if l > 0:
            rprm["v0"] = rwkv_v0[l - 1][None, :]
            rprm["v_down"] = _pad_cols(rwkv_v_down[l - 1], LANE).astype(BF16)
            rprm["v_up"] = _pad_rows(rwkv_v_up[l - 1], LANE).astype(BF16)
        y_r, vfirst = _rwkv_mixer(p_r.reshape(bsz, seq, RW_WIDTH), vfirst, rprm, has_vres=l > 0)
        gprm = {"gk_up": _pad_rows(gla_gk_up[l], LANE).astype(BF16), "gk_b": gla_gk_b[l][None, :],
                "norm_w": jnp.tile(gla_norm_w[l], GLA_HEADS)[None, :]}
        y_g = _gla_mixer(p_g.reshape(bsz, seq, GL_WIDTH), gprm)
        x1, idx, gate, rank, cnt = _outproj_router(
            y_r.reshape(t, D_RWKV), y_g.reshape(t, D_GLA), xf, w_out[l].astype(BF16),
            ln1_w[l][None, :], ln1_b[l][None, :], router_w[l].T, router_b[l][:, None])
        xf = _moe(x1, idx, gate, rank, cnt[:, 0], exp_w1[l].astype(BF16), exp_b1[l][:, None, :],
                  exp_w2[l].astype(BF16), exp_b2[l][:, None, :], ln2_w[l][None, :], ln2_b[l][None, :])
    return xf.reshape(bsz, seq, d)
```

```python
import functools

import jax
import jax.numpy as jnp
from jax import lax
from jax.experimental import pallas as pl
from jax.experimental.pallas import tpu as pltpu

F32 = jnp.float32
BF16 = jnp.bfloat16
HIGHEST = lax.Precision.HIGHEST

D_MODEL = 1024
DEPTH = 2
CHUNK = 64
RWKV_HEAD = 64
D_RWKV = 512
RWKV_HEADS = 8
W_LORA = 64
A_LORA = 64
V_LORA = 32
G_LORA = 160
RWKV_GN_EPS = 64e-5
D_GLA = 512
GLA_HEADS = 4
GLA_DV = 128
GLA_DK = 64
GLA_GATE_RANK = 16
GLA_GATE_NORMALIZER = 16.0
RMS_EPS = 1e-6
N_EXPERTS = 32
TOP_K = 4
D_FF = 1024
SWIGLU_LIMIT = 7.0
SWIGLU_ALPHA = 1.702
MOE_BLOCK = 256
DEEPNORM_ALPHA = (2 * DEPTH) ** 0.25
LN_EPS = 1e-5

LANE = 128
RW_WD = 3 * D_RWKV
RW_AD = RW_WD + LANE
RW_GD = RW_AD + LANE
RW_WIDTH = RW_GD + 2 * LANE
GL_K = GLA_HEADS * GLA_DK
GL_V = 2 * GL_K
GL_GK = GL_V + D_GLA
GL_G = GL_GK + LANE
GL_WIDTH = GL_G + D_GLA

VMEM_LIMIT = 56 * 1024 * 1024


def _cparams(n_axes):
    return pltpu.CompilerParams(dimension_semantics=("arbitrary",) * n_axes,
                                vmem_limit_bytes=VMEM_LIMIT)


def _dot(a, b):
    return jnp.dot(a, b, preferred_element_type=F32)


def _dot_nt(a, b):
    return lax.dot_general(a, b, (((1,), (1,)), ((), ())), preferred_element_type=F32)


def _dot_tn(a, b):
    return lax.dot_general(a, b, (((0,), (0,)), ((), ())), preferred_element_type=F32)


def _split_bf16(x):
    hi = x.astype(BF16)
    lo = (x - hi.astype(F32)).astype(BF16)
    return hi, lo


def _layer_norm(z, w, b):
    mu = jnp.mean(z, axis=-1, keepdims=True)
    zc = z - mu
    var = jnp.mean(zc * zc, axis=-1, keepdims=True)
    return zc * lax.rsqrt(var + LN_EPS) * w + b


def _sigmoid(x):
    return 1.0 / (1.0 + jnp.exp(-x))


def _softplus(x):
    return jnp.maximum(x, 0.0) + jnp.log(1.0 + jnp.exp(-jnp.abs(x)))


def _tril_mask(n, strict):
    r = lax.broadcasted_iota(jnp.int32, (n, n), 0)
    c = lax.broadcasted_iota(jnp.int32, (n, n), 1)
    return (c < r) if strict else (c <= r)


def _inproj_kernel(apply_ln, x_ref, lnw_ref, lnb_ref, wr_ref, wg_ref, *out_refs):
    x = x_ref[...]
    if apply_ln:
        x0_ref, pr_ref, pg_ref = out_refs
        x = _layer_norm(x, lnw_ref[...], lnb_ref[...])
        x0_ref[...] = x
    else:
        pr_ref, pg_ref = out_refs
    xb = x.astype(BF16)
    pr_ref[...] = _dot(xb, wr_ref[...]).astype(BF16)
    pg_ref[...] = _dot(xb, wg_ref[...]).astype(BF16)


def _inproj(x, lnw, lnb, wr, wg, apply_ln):
    t = x.shape[0]
    tm = min(512, t)
    row = lambda i: (i, 0)
    const = lambda i: (0, 0)
    out_shape = [jax.ShapeDtypeStruct((t, RW_WIDTH), BF16), jax.ShapeDtypeStruct((t, GL_WIDTH), BF16)]
    out_specs = [pl.BlockSpec((tm, RW_WIDTH), row), pl.BlockSpec((tm, GL_WIDTH), row)]
    if apply_ln:
        out_shape = [jax.ShapeDtypeStruct((t, D_MODEL), F32)] + out_shape
        out_specs = [pl.BlockSpec((tm, D_MODEL), row)] + out_specs
    return pl.pallas_call(
        functools.partial(_inproj_kernel, apply_ln),
        grid=(t // tm,),
        in_specs=[pl.BlockSpec((tm, D_MODEL), row),
                  pl.BlockSpec((1, D_MODEL), const), pl.BlockSpec((1, D_MODEL), const),
                  pl.BlockSpec((D_MODEL, RW_WIDTH), const), pl.BlockSpec((D_MODEL, GL_WIDTH), const)],
        out_specs=out_specs,
        out_shape=out_shape,
        compiler_params=_cparams(1),
        name="inproj_ln" if apply_ln else "inproj",
    )(x, lnw, lnb, wr, wg)


def _head_sum(x, ones_blk):
    hi, lo = _split_bf16(x)
    return _dot(hi, ones_blk) + _dot(lo, ones_blk)


def _rwkv_kernel(has_vres, *refs):
    if has_vres:
        (p_ref, vfirst_ref, mu_ref, w0_ref, wup_ref, a0_ref, aup_ref, gup_ref, kk_ref, ka_ref, rk_ref,
         gnw_ref, gnb_ref, v0_ref, vdown_ref, vup_ref, y_ref, state_ref, prev_ref) = refs
    else:
        (p_ref, mu_ref, w0_ref, wup_ref, a0_ref, aup_ref, gup_ref, kk_ref, ka_ref, rk_ref,
         gnw_ref, gnb_ref, y_ref, vfirst_out_ref, state_ref, prev_ref) = refs

    @pl.when(pl.program_id(1) == 0)
    def _():
        state_ref[...] = jnp.zeros_like(state_ref)
        prev_ref[...] = jnp.zeros_like(prev_ref)

    p = p_ref[0].astype(F32)
    n_tok = p.shape[0]
    row = lax.broadcasted_iota(jnp.int32, (n_tok, 1), 0)
    prev = jnp.where(row == 0, prev_ref[...], pltpu.roll(p, 1, axis=0))
    prev_ref[...] = p[n_tok - 1:n_tok, :]
    xs = p + (prev - p) * mu_ref[...]
    r = xs[:, 0:D_RWKV]
    k = xs[:, D_RWKV:2 * D_RWKV]
    v = xs[:, 2 * D_RWKV:3 * D_RWKV]
    wd = xs[:, RW_WD:RW_AD]
    ad = xs[:, RW_AD:RW_GD]
    gd = xs[:, RW_GD:RW_WIDTH]

    w = w0_ref[...] + _dot(jnp.tanh(wd).astype(BF16), wup_ref[...])
    w = -_softplus(-w) - 0.5
    logw = -jnp.exp(w)
    a = _sigmoid(a0_ref[...] + _dot(ad.astype(BF16), aup_ref[...]))
    g = _dot(_sigmoid(gd).astype(BF16), gup_ref[...])
    if has_vres:
        vmix = _dot(_dot(v.astype(BF16), vdown_ref[...]).astype(BF16), vup_ref[...])
        v = v + (vfirst_ref[0] - v) * _sigmoid(v0_ref[...] + vmix)
    else:
        vfirst_out_ref[0] = v

    hr = lax.broadcasted_iota(jnp.int32, (D_RWKV, D_RWKV), 0) // RWKV_HEAD
    hc = lax.broadcasted_iota(jnp.int32, (D_RWKV, D_RWKV), 1) // RWKV_HEAD
    ones_blk = (hr == hc).astype(BF16)

    kk = k * kk_ref[...]
    kk = kk * lax.rsqrt(jnp.maximum(_head_sum(kk * kk, ones_blk), 1e-24))
    k = k * (1.0 + (a - 1.0) * ka_ref[...])
    bonus = _head_sum(r * k * rk_ref[...], ones_blk)

    n_chunks = n_tok // CHUNK
    c2 = 2 * CHUNK
    r_i = lax.broadcasted_iota(jnp.int32, (CHUNK, c2), 0)
    c_i = lax.broadcasted_iota(jnp.int32, (CHUNK, c2), 1)
    c_half = jnp.where(c_i >= CHUNK, c_i - CHUNK, c_i)
    m_strict_l = ((c_i < CHUNK) & (c_i < r_i)).astype(F32)
    m_strict_r = ((c_i >= CHUNK) & (c_half < r_i)).astype(F32)
    m_lower2 = (c_half <= r_i).astype(F32)
    rr = lax.broadcasted_iota(jnp.int32, (CHUNK, CHUNK), 0)
    cc = lax.broadcasted_iota(jnp.int32, (CHUNK, CHUNK), 1)
    tril_f = (cc <= rr).astype(F32)
    eye = (rr == cc).astype(F32)
    m_diag2 = ((rr // 2) == (cc // 2)).astype(F32)
    lvl_masks = []
    size = 2
    while size < CHUNK:
        lvl_masks.append((((rr // (2 * size)) == (cc // (2 * size))) & ((rr // size) != (cc // size))).astype(F32))
        size *= 2

    items = []
    g_ends = []
    for c in range(n_chunks):
        sl = slice(c * CHUNK, (c + 1) * CHUNK)
        lw = logw[sl]
        b = jnp.dot(tril_f, lw, precision=HIGHEST, preferred_element_type=F32)
        eb = jnp.exp(b)
        enb = jnp.exp(-b)
        ebx = jnp.exp(b - lw)
        rh = (r[sl] * eb).astype(BF16)
        kh = (k[sl] * enb).astype(BF16)
        ah = (-kk[sl] * ebx).astype(BF16)
        bh = (kk[sl] * a[sl] * enb).astype(BF16)
        vb = v[sl].astype(BF16)
        g_ends.append(eb[CHUNK - 1:CHUNK, :])
        for h in range(RWKV_HEADS):
            hs = slice(h * RWKV_HEAD, (h + 1) * RWKV_HEAD)
            items.append(dict(ah=ah[:, hs], rh=rh[:, hs], v=vb[:, hs],
                              bk=jnp.concatenate([bh[:, hs], kh[:, hs]], axis=0)))

    for it in items:
        it["s2"] = _dot_nt(jnp.concatenate([it["ah"], it["rh"]], axis=0), it["bk"])
    for it in items:
        top = it["s2"][:CHUNK]
        it["a_ab"] = (top * m_strict_l)[:, :CHUNK]
        it["ak"] = (top * m_strict_r).astype(BF16)
        it["rbk"] = (it["s2"][CHUNK:] * m_lower2).astype(BF16)
        it["d"] = eye + it["a_ab"] * m_diag2
        del it["s2"]
    for mk in lvl_masks:
        for it in items:
            it["db"] = it["d"].astype(BF16)
            it["m"] = _dot((it["a_ab"] * mk).astype(BF16), it["db"])
        for it in items:
            it["d"] = it["d"] + _dot(it["db"], it["m"].astype(BF16))
    for it in items:
        it["akv"] = _dot(it["ak"], jnp.concatenate([it["v"], it["v"]], axis=0))
        it["tinv"] = it["d"].astype(BF16)
    for it in items:
        it["w"] = _dot(it["tinv"], it["ah"])
        it["ut"] = _dot(it["tinv"], it["akv"].astype(BF16))

    states = [state_ref[h] for h in range(RWKV_HEADS)]
    y_chunks = []
    for c in range(n_chunks):
        its = items[c * RWKV_HEADS:(c + 1) * RWKV_HEADS]
        for h, it in enumerate(its):
            it["ws"] = _dot_nt(jnp.concatenate([it["w"].astype(BF16), it["rh"]], axis=0), states[h].astype(BF16))
        for h, it in enumerate(its):
            it["uv"] = jnp.concatenate([(it["ws"][:CHUNK] + it["ut"]).astype(BF16), it["v"]], axis=0)
            hs = slice(h * RWKV_HEAD, (h + 1) * RWKV_HEAD)
            states[h] = (states[h] + _dot_tn(it["uv"], it["bk"])) * g_ends[c][:, hs]
        y_heads = [it["ws"][CHUNK:] + _dot(it["rbk"], it["uv"]) for it in its]
        y_chunks.append(jnp.concatenate(y_heads, axis=1))
    for h in range(RWKV_HEADS):
        state_ref[h] = states[h]
    y = jnp.concatenate(y_chunks, axis=0) if len(y_chunks) > 1 else y_chunks[0]

    inv_n = 1.0 / RWKV_HEAD
    m = _head_sum(y, ones_blk) * inv_n
    yc = y - m
    var = _head_sum(yc * yc, ones_blk) * inv_n
    y = yc * lax.rsqrt(var + RWKV_GN_EPS) * gnw_ref[...] + gnb_ref[...]
    y = y + bonus * v
    y_ref[0] = (y * g).astype(BF16)


def _rwkv_mixer(p_r, vfirst, prm, has_vres):
    bsz, seq, _ = p_r.shape
    n_tok = min(128, seq)
    tok = lambda b, i: (b, i, 0)
    const = lambda b, i: (0, 0)
    vec = pl.BlockSpec((1, D_RWKV), const)
    in_specs = [pl.BlockSpec((1, n_tok, RW_WIDTH), tok)]
    args = [p_r]
    if has_vres:
        in_specs.append(pl.BlockSpec((1, n_tok, D_RWKV), tok))
        args.append(vfirst)
    in_specs += [pl.BlockSpec((1, RW_WIDTH), const), vec, pl.BlockSpec((LANE, D_RWKV), const), vec,
                 pl.BlockSpec((LANE, D_RWKV), const), pl.BlockSpec((2 * LANE, D_RWKV), const),
                 vec, vec, vec, vec, vec]
    args += [prm["mu"], prm["w0"], prm["w_up"], prm["a0"], prm["a_up"], prm["g_up"],
             prm["k_k"], prm["k_a"], prm["r_k"], prm["gn_w"], prm["gn_b"]]
    out_shape = [jax.ShapeDtypeStruct((bsz, seq, D_RWKV), BF16)]
    out_specs = [pl.BlockSpec((1, n_tok, D_RWKV), tok)]
    if has_vres:
        in_specs += [vec, pl.BlockSpec((D_RWKV, LANE), const), pl.BlockSpec((LANE, D_RWKV), const)]
        args += [prm["v0"], prm["v_down"], prm["v_up"]]
    else:
        out_shape.append(jax.ShapeDtypeStruct((bsz, seq, D_RWKV), F32))
        out_specs.append(pl.BlockSpec((1, n_tok, D_RWKV), tok))
    res = pl.pallas_call(
        functools.partial(_rwkv_kernel, has_vres),
        grid=(bsz, seq // n_tok),
        in_specs=in_specs,
        out_specs=out_specs,
        out_shape=out_shape,
        scratch_shapes=[pltpu.VMEM((RWKV_HEADS, RWKV_HEAD, RWKV_HEAD), F32),
                        pltpu.VMEM((1, RW_WIDTH), F32)],
        compiler_params=_cparams(2),
        name="rwkv7_vres" if has_vres else "rwkv7",
    )(*args)
    if has_vres:
        return res[0], vfirst
    return res[0], res[1]


def _gla_kernel(p_ref, gkup_ref, gkb_ref, nw_ref, y_ref, state_ref):
    @pl.when(pl.program_id(1) == 0)
    def _():
        state_ref[...] = jnp.zeros_like(state_ref)

    p = p_ref[0].astype(F32)
    n_tok = p.shape[0]
    q = p[:, 0:GL_K]
    k = p[:, GL_K:GL_V]
    v = p[:, GL_V:GL_GK]
    gkd = p[:, GL_GK:GL_G]
    g = p[:, GL_G:GL_WIDTH]
    z = _dot(gkd.astype(BF16), gkup_ref[...]) + gkb_ref[...]
    gk = -_softplus(-z) * (1.0 / GLA_GATE_NORMALIZER)

    lower = _tril_mask(CHUNK, strict=False)
    tril_f = lower.astype(F32)
    o_chunks = []
    for c in range(n_tok // CHUNK):
        sl = slice(c * CHUNK, (c + 1) * CHUNK)
        b = jnp.dot(tril_f, gk[sl], precision=HIGHEST, preferred_element_type=F32)
        b_last = b[CHUNK - 1:CHUNK, :]
        q_e = (q[sl] * jnp.exp(b) * (GLA_DK ** -0.5)).astype(BF16)
        k_e = (k[sl] * jnp.exp(-b)).astype(BF16)
        k_end = (k[sl] * jnp.exp(b_last - b)).astype(BF16)
        dec = jnp.exp(b_last)
        vb = v[sl].astype(BF16)
        o_heads = []
        for h in range(GLA_HEADS):
            ks = slice(h * GLA_DK, (h + 1) * GLA_DK)
            vs = slice(h * GLA_DV, (h + 1) * GLA_DV)
            scores = jnp.where(lower, _dot_nt(q_e[:, ks], k_e[:, ks]), 0.0)
            s_old = state_ref[h]
            o_h = _dot(scores.astype(BF16), vb[:, vs]) + _dot_nt(q_e[:, ks], s_old.astype(BF16))
            state_ref[h] = s_old * dec[:, ks] + _dot_tn(vb[:, vs], k_end[:, ks])
            o_h = o_h * lax.rsqrt(jnp.mean(o_h * o_h, axis=-1, keepdims=True) + RMS_EPS)
            o_heads.append(o_h)
        o_chunks.append(jnp.concatenate(o_heads, axis=1))
    o = jnp.concatenate(o_chunks, axis=0) if len(o_chunks) > 1 else o_chunks[0]
    silu_g = g * _sigmoid(g)
    y_ref[0] = (o * nw_ref[...] * silu_g).astype(BF16)


def _gla_mixer(p_g, prm):
    bsz, seq, _ = p_g.shape
    n_tok = min(128, seq)
    tok = lambda b, i: (b, i, 0)
    const = lambda b, i: (0, 0)
    return pl.pallas_call(
        _gla_kernel,
        grid=(bsz, seq // n_tok),
        in_specs=[pl.BlockSpec((1, n_tok, GL_WIDTH), tok),
                  pl.BlockSpec((LANE, GL_K), const), pl.BlockSpec((1, GL_K), const),
                  pl.BlockSpec((1, D_GLA), const)],
        out_specs=pl.BlockSpec((1, n_tok, D_GLA), tok),
        out_shape=jax.ShapeDtypeStruct((bsz, seq, D_GLA), BF16),
        scratch_shapes=[pltpu.VMEM((GLA_HEADS, GLA_DV, GLA_DK), F32)],
        compiler_params=_cparams(2),
        name="gla",
    )(p_g, prm["gk_up"], prm["gk_b"], prm["norm_w"])


def _outproj_router_kernel(yr_ref, yg_ref, x_ref, wo_ref, lnw_ref, lnb_ref, rw_ref, rb_ref,
                           x1_ref, idx_ref, gate_ref, rank_ref, cnt_ref, carry_ref):
    @pl.when(pl.program_id(0) == 0)
    def _():
        carry_ref[...] = jnp.zeros_like(carry_ref)

    mix = _dot(yr_ref[...], wo_ref[0:D_RWKV, :]) + _dot(yg_ref[...], wo_ref[D_RWKV:, :])
    x1 = _layer_norm(DEEPNORM_ALPHA * x_ref[...] + mix, lnw_ref[...], lnb_ref[...])
    x1_ref[...] = x1
    tm = x1.shape[0]

    xh, xl = _split_bf16(x1)
    rw = rw_ref[...]
    wh, wl = _split_bf16(rw)
    logits = _dot_nt(wh, xh) + _dot_nt(wh, xl) + _dot_nt(wl, xh) + rb_ref[...]

    e_iota = lax.broadcasted_iota(jnp.int32, (N_EXPERTS, tm), 0)
    work = logits
    vals, idxs = [], []
    member = jnp.zeros((N_EXPERTS, tm), F32)
    for _ in range(TOP_K):
        mx = jnp.max(work, axis=0, keepdims=True)
        ix = jnp.min(jnp.where(work == mx, e_iota, N_EXPERTS), axis=0, keepdims=True)
        sel = e_iota == ix
        work = jnp.where(sel, -jnp.inf, work)
        member = jnp.where(sel, 1.0, member)
        vals.append(mx)
        idxs.append(ix)
    exps = [jnp.exp(vv - vals[0]) for vv in vals]
    inv_den = 1.0 / (exps[0] + exps[1] + exps[2] + exps[3])

    tr = lax.broadcasted_iota(jnp.int32, (tm, tm), 0)
    tc = lax.broadcasted_iota(jnp.int32, (tm, tm), 1)
    before = (tr < tc).astype(BF16)
    cex = _dot(member.astype(BF16), before) + carry_ref[...][:, 0:1]
    for kq in range(TOP_K):
        sel = e_iota == idxs[kq]
        idx_ref[kq:kq + 1, :] = idxs[kq]
        gate_ref[kq:kq + 1, :] = exps[kq] * inv_den
        rank_ref[kq:kq + 1, :] = jnp.sum(jnp.where(sel, cex, 0.0), axis=0, keepdims=True).astype(jnp.int32)
    carry_ref[...] = carry_ref[...] + jnp.sum(member, axis=1, keepdims=True)
    cnt_ref[...] = carry_ref[...].astype(jnp.int32)


def _outproj_router(yr, yg, x, wo, lnw, lnb, rw_t, rb):
    t = x.shape[0]
    tm = min(512, t)
    row = lambda i: (i, 0)
    col = lambda i: (0, i)
    const = lambda i: (0, 0)
    return pl.pallas_call(
        _outproj_router_kernel,
        grid=(t // tm,),
        in_specs=[pl.BlockSpec((tm, D_RWKV), row), pl.BlockSpec((tm, D_GLA), row),
                  pl.BlockSpec((tm, D_MODEL), row), pl.BlockSpec((D_MODEL, D_MODEL), const),
                  pl.BlockSpec((1, D_MODEL), const), pl.BlockSpec((1, D_MODEL), const),
                  pl.BlockSpec((N_EXPERTS, D_MODEL), const), pl.BlockSpec((N_EXPERTS, 1), const)],
        out_specs=[pl.BlockSpec((tm, D_MODEL), row),
                   pl.BlockSpec((TOP_K, tm), col), pl.BlockSpec((TOP_K, tm), col),
                   pl.BlockSpec((TOP_K, tm), col), pl.BlockSpec((N_EXPERTS, LANE), const)],
        out_shape=[jax.ShapeDtypeStruct((t, D_MODEL), F32),
                   jax.ShapeDtypeStruct((TOP_K, t), jnp.int32), jax.ShapeDtypeStruct((TOP_K, t), F32),
                   jax.ShapeDtypeStruct((TOP_K, t), jnp.int32),
                   jax.ShapeDtypeStruct((N_EXPERTS, LANE), jnp.int32)],
        scratch_shapes=[pltpu.VMEM((N_EXPERTS, LANE), F32)],
        compiler_params=_cparams(1),
        name="outproj_router",
    )(yr, yg, x, wo, lnw, lnb, rw_t, rb)


def _dispatch_kernel(dest_ref, gaps_ref, x_ref, xs_ref, zero_ref, sem, zsem):
    te = x_ref.shape[0]
    base = pl.program_id(0) * (TOP_K * te)

    @pl.when(pl.program_id(0) == 0)
    def _():
        zero_ref[...] = jnp.zeros_like(zero_ref)

        def zero_copy(d):
            return pltpu.make_async_copy(zero_ref, xs_ref.at[pl.ds(d, 1), :], zsem)

        for g in range(N_EXPERTS + 1):
            lo = gaps_ref[2 * g]
            hi = gaps_ref[2 * g + 1]
            lax.fori_loop(lo, hi, lambda d, c: (zero_copy(d).start(), c)[1], 0)
        for g in range(N_EXPERTS + 1):
            lo = gaps_ref[2 * g]
            hi = gaps_ref[2 * g + 1]
            lax.fori_loop(lo, hi, lambda d, c: (zero_copy(d).wait(), c)[1], 0)

    def row_copy(r, kq):
        d = dest_ref[base + kq * te + r]
        return pltpu.make_async_copy(x_ref.at[pl.ds(r, 1), :], xs_ref.at[pl.ds(d, 1), :], sem)

    def issue(r, carry):
        for kq in range(TOP_K):
            row_copy(r, kq).start()
        return carry

    def drain(r, carry):
        for kq in range(TOP_K):
            row_copy(r, kq).wait()
        return carry

    lax.fori_loop(0, te, issue, 0)
    lax.fori_loop(0, te, drain, 0)


def _dispatch(x1, dest_tiles, gaps, m_pad, te):
    t = x1.shape[0]
    return pl.pallas_call(
        _dispatch_kernel,
        grid_spec=pltpu.PrefetchScalarGridSpec(
            num_scalar_prefetch=2,
            grid=(t // te,),
            in_specs=[pl.BlockSpec((te, D_MODEL), lambda i, d, g: (i, 0))],
            out_specs=pl.BlockSpec(memory_space=pl.ANY),
            scratch_shapes=[pltpu.VMEM((1, D_MODEL), F32), pltpu.SemaphoreType.DMA(()),
                            pltpu.SemaphoreType.DMA(())],
        ),
        out_shape=jax.ShapeDtypeStruct((m_pad, D_MODEL), F32),
        compiler_params=_cparams(1),
        name="moe_dispatch",
    )(dest_tiles, gaps, x1)


def _expert_kernel(be_ref, nv_ref, xs_ref, w1_ref, b1_ref, w2_ref, b2_ref, ys_ref, act_ref):
    j = pl.program_id(0)
    n_valid = nv_ref[j]

    @pl.when(n_valid == 0)
    def _():
        ys_ref[...] = jnp.zeros_like(ys_ref)

    @pl.when(n_valid > 0)
    def _():
        xb = xs_ref[...].astype(BF16)
        n_chunk = 256
        for c in range(D_FF // n_chunk):
            cs = slice(c * n_chunk, (c + 1) * n_chunk)
            us = slice(D_FF + c * n_chunk, D_FF + (c + 1) * n_chunk)
            gt = _dot(xb, w1_ref[0, :, cs]) + b1_ref[0, :, cs]
            up = _dot(xb, w1_ref[0, :, us]) + b1_ref[0, :, us]
            gt = jnp.minimum(gt, SWIGLU_LIMIT)
            up = jnp.clip(up, -SWIGLU_LIMIT, SWIGLU_LIMIT)
            act_ref[:, cs] = ((up + 1.0) * gt * _sigmoid(SWIGLU_ALPHA * gt)).astype(BF16)
        ys_ref[...] = _dot(act_ref[...], w2_ref[0]) + b2_ref[0]


def _experts(xs, block_e, n_valid, w1, b1, w2, b2):
    m_pad = xs.shape[0]
    n_blocks = m_pad // MOE_BLOCK
    return pl.pallas_call(
        _expert_kernel,
        grid_spec=pltpu.PrefetchScalarGridSpec(
            num_scalar_prefetch=2,
            grid=(n_blocks,),
            in_specs=[pl.BlockSpec((MOE_BLOCK, D_MODEL), lambda j, be, nv: (j, 0)),
                      pl.BlockSpec((1, D_MODEL, 2 * D_FF), lambda j, be, nv: (be[j], 0, 0)),
                      pl.BlockSpec((1, 1, 2 * D_FF), lambda j, be, nv: (be[j], 0, 0)),
                      pl.BlockSpec((1, D_FF, D_MODEL), lambda j, be, nv: (be[j], 0, 0)),
                      pl.BlockSpec((1, 1, D_MODEL), lambda j, be, nv: (be[j], 0, 0))],
            out_specs=pl.BlockSpec((MOE_BLOCK, D_MODEL), lambda j, be, nv: (j, 0)),
            scratch_shapes=[pltpu.VMEM((MOE_BLOCK, D_FF), BF16)],
        ),
        out_shape=jax.ShapeDtypeStruct((m_pad, D_MODEL), F32),
        compiler_params=_cparams(1),
        name="moe_experts",
    )(block_e, n_valid, xs, w1, b1, w2, b2)


def _combine_kernel(dest_ref, ys_ref, x1_ref, gate_ref, lnw_ref, lnb_ref, out_ref, buf_ref, sem):
    te = x1_ref.shape[0]
    base = pl.program_id(0) * (TOP_K * te)

    def row_copy(r, kq):
        d = dest_ref[base + kq * te + r]
        return pltpu.make_async_copy(ys_ref.at[pl.ds(d, 1), :], buf_ref.at[kq, pl.ds(r, 1), :], sem)

    def issue(r, carry):
        for kq in range(TOP_K):
            row_copy(r, kq).start()
        return carry

    def drain(r, carry):
        for kq in range(TOP_K):
            row_copy(r, kq).wait()
        return carry

    lax.fori_loop(0, te, issue, 0)
    lax.fori_loop(0, te, drain, 0)

    gates = gate_ref[...]
    z = DEEPNORM_ALPHA * x1_ref[...]
    for kq in range(TOP_K):
        z = z + buf_ref[kq] * gates[:, kq:kq + 1]
    out_ref[...] = _layer_norm(z, lnw_ref[...], lnb_ref[...])


def _combine(ys, dest_tiles, x1, gates_t, lnw, lnb, te):
    t = x1.shape[0]
    return pl.pallas_call(
        _combine_kernel,
        grid_spec=pltpu.PrefetchScalarGridSpec(
            num_scalar_prefetch=1,
            grid=(t // te,),
            in_specs=[pl.BlockSpec(memory_space=pl.ANY),
                      pl.BlockSpec((te, D_MODEL), lambda i, d: (i, 0)),
                      pl.BlockSpec((te, TOP_K), lambda i, d: (i, 0)),
                      pl.BlockSpec((1, D_MODEL), lambda i, d: (0, 0)),
                      pl.BlockSpec((1, D_MODEL), lambda i, d: (0, 0))],
            out_specs=pl.BlockSpec((te, D_MODEL), lambda i, d: (i, 0)),
            scratch_shapes=[pltpu.VMEM((TOP_K, te, D_MODEL), F32), pltpu.SemaphoreType.DMA(())],
        ),
        out_shape=jax.ShapeDtypeStruct((t, D_MODEL), F32),
        compiler_params=_cparams(1),
        name="moe_combine",
    )(dest_tiles, ys, x1, gates_t, lnw, lnb)


def _moe(x1, idx, gate, rank, counts, w1, b1, w2, b2, lnw, lnb):
    t = x1.shape[0]
    te = min(256, t)
    n_blocks = -(-(t * TOP_K) // MOE_BLOCK) + N_EXPERTS
    m_pad = n_blocks * MOE_BLOCK
    padded = ((counts + MOE_BLOCK - 1) // MOE_BLOCK) * MOE_BLOCK
    pend = jnp.cumsum(padded)
    pstart = pend - padded
    dest = rank
    for e in range(N_EXPERTS):
        dest = dest + jnp.where(idx == e, pstart[e], 0)
    dest_tiles = dest.reshape(TOP_K, t // te, te).transpose(1, 0, 2).reshape(-1)
    blk_start = jnp.arange(n_blocks, dtype=jnp.int32) * MOE_BLOCK
    block_e = jnp.minimum(jnp.sum(pend[None, :] <= blk_start[:, None], axis=1), N_EXPERTS - 1).astype(jnp.int32)
    is_e = block_e[:, None] == jnp.arange(N_EXPERTS, dtype=jnp.int32)[None, :]
    blk_end = jnp.sum(jnp.where(is_e, (pstart + counts)[None, :], 0), axis=1)
    n_valid = jnp.clip(blk_end - blk_start, 0, MOE_BLOCK).astype(jnp.int32)

    gap_lo = jnp.concatenate([pstart + counts, pend[-1:]])
    gap_hi = jnp.concatenate([pend, jnp.full((1,), m_pad, pend.dtype)])
    gaps = jnp.stack([gap_lo, gap_hi], axis=1).reshape(-1).astype(jnp.int32)

    xs = _dispatch(x1, dest_tiles, gaps, m_pad, te)
    ys = _experts(xs, block_e, n_valid, w1, b1, w2, b2)
    return _combine(ys, dest_tiles, x1, gate.T, lnw, lnb, te)


def _pad_cols(w, width):
    return jnp.pad(w, ((0, 0), (0, width - w.shape[1])))


def _pad_rows(w, height):
    return jnp.pad(w, ((0, height - w.shape[0]), (0, 0)))


def _pack_inproj(w_in_l, mu_l):
    d3 = 3 * D_RWKV
    o_ad = d3 + W_LORA
    o_gd = o_ad + A_LORA
    o_gla = o_gd + G_LORA
    wr = jnp.concatenate([w_in_l[:, :d3],
                          _pad_cols(w_in_l[:, d3:o_ad], LANE),
                          _pad_cols(w_in_l[:, o_ad:o_gd], LANE),
                          _pad_cols(w_in_l[:, o_gd:o_gla], 2 * LANE)], axis=1)
    mu = mu_l[None, :]
    mu_p = jnp.concatenate([mu[:, :d3], _pad_cols(mu[:, d3:o_ad], LANE), _pad_cols(mu[:, o_ad:o_gd], LANE),
                            _pad_cols(mu[:, o_gd:o_gla], 2 * LANE)], axis=1)
    g = w_in_l[:, o_gla:]
    wg = jnp.concatenate([g[:, :GL_GK], _pad_cols(g[:, GL_GK:GL_GK + GLA_GATE_RANK], LANE),
                          g[:, GL_GK + GLA_GATE_RANK:]], axis=1)
    return wr.astype(BF16), wg.astype(BF16), mu_p


def kernel(x, ln_in_w, ln_in_b, w_in, rwkv_mu, rwkv_w0, rwkv_w_up, rwkv_a0, rwkv_a_up, rwkv_g_up, rwkv_k_k, rwkv_k_a, rwkv_r_k, rwkv_gn_w, rwkv_gn_b, rwkv_v0, rwkv_v_down, rwkv_v_up, gla_gk_up, gla_gk_b, gla_norm_w, w_out, ln1_w, ln1_b, router_w, router_b, exp_w1, exp_b1, exp_w2, exp_b2, ln2_w, ln2_b):
    bsz, seq, d = x.shape
    t = bsz * seq
    xf = x.reshape(t, d)
    vfirst = None
    for l in range(DEPTH):
        wr, wg, mu_p = _pack_inproj(w_in[l], rwkv_mu[l])
        if l == 0:
            xf, p_r, p_g = _inproj(xf, ln_in_w[None, :], ln_in_b[None, :], wr, wg, apply_ln=True)
        else:
            p_r, p_g = _inproj(xf, ln_in_w[None, :], ln_in_b[None, :], wr, wg, apply_ln=False)
        rprm = {
            "mu": mu_p, "w0": rwkv_w0[l][None, :], "w_up": _pad_rows(rwkv_w_up[l], LANE).astype(BF16),
            "a0": rwkv_a0[l][None, :], "a_up": _pad_rows(rwkv_a_up[l], LANE).astype(BF16),
            "g_up": _pad_rows(rwkv_g_up[l], 2 * LANE).astype(BF16),
            "k_k": rwkv_k_k[l][None, :], "k_a": rwkv_k_a[l][None, :], "r_k": rwkv_r_k[l][None, :],
            "gn_w": rwkv_gn_w[l][None, :], "gn_b": rwkv_gn_b[l][None, :],
        }
        if l > 0:
            rprm["v0"] = rwkv_v0[l - 1][None, :]
            rprm["v_down"] = _pad_cols(rwkv_v_down[l - 1], LANE).astype(BF16)
            rprm["v_up"] = _pad_rows(rwkv_v_up[l - 1], LANE).astype(BF16)
        y_r, vfirst = _rwkv_mixer(p_r.reshape(bsz, seq, RW_WIDTH), vfirst, rprm, has_vres=l > 0)
        gprm = {"gk_up": _pad_rows(gla_gk_up[l], LANE).astype(BF16), "gk_b": gla_gk_b[l][None, :],
                "norm_w": jnp.tile(gla_norm_w[l], GLA_HEADS)[None, :]}
        y_g = _gla_mixer(p_g.reshape(bsz, seq, GL_WIDTH), gprm)
        x1, idx, gate, rank, cnt = _outproj_router(
            y_r.reshape(t, D_RWKV), y_g.reshape(t, D_GLA), xf, w_out[l].astype(BF16),
            ln1_w[l][None, :], ln1_b[l][None, :], router_w[l].T, router_b[l][:, None])
        xf = _moe(x1, idx, gate, rank, cnt[:, 0], exp_w1[l].astype(BF16), exp_b1[l][:, None, :],
                  exp_w2[l].astype(BF16), exp_b2[l][:, None, :], ln2_w[l][None, :], ln2_b[l][None, :])
    return xf.reshape(bsz, seq, d)
```

```python
import functools

import jax
import jax.numpy as jnp
from jax import lax
from jax.experimental import pallas as pl
from jax.experimental.pallas import tpu as pltpu

F32 = jnp.float32
BF16 = jnp.bfloat16
HIGHEST = lax.Precision.HIGHEST

D_MODEL = 1024
DEPTH = 2
CHUNK = 64
RWKV_HEAD = 64
D_RWKV = 512
RWKV_HEADS = 8
W_LORA = 64
A_LORA = 64
V_LORA = 32
G_LORA = 160
RWKV_GN_EPS = 64e-5
D_GLA = 512
GLA_HEADS = 4
GLA_DV = 128
GLA_DK = 64
GLA_GATE_RANK = 16
GLA_GATE_NORMALIZER = 16.0
RMS_EPS = 1e-6
N_EXPERTS = 32
TOP_K = 4
D_FF = 1024
SWIGLU_LIMIT = 7.0
SWIGLU_ALPHA = 1.702
MOE_BLOCK = 256
DEEPNORM_ALPHA = (2 * DEPTH) ** 0.25
LN_EPS = 1e-5

LANE = 128
RW_WD = 3 * D_RWKV
RW_AD = RW_WD + LANE
RW_GD = RW_AD + LANE
RW_WIDTH = RW_GD + 2 * LANE
GL_K = GLA_HEADS * GLA_DK
GL_V = 2 * GL_K
GL_GK = GL_V + D_GLA
GL_G = GL_GK + LANE
GL_WIDTH = GL_G + D_GLA

VMEM_LIMIT = 56 * 1024 * 1024


def _cparams(n_axes):
    return pltpu.CompilerParams(dimension_semantics=("arbitrary",) * n_axes,
                                vmem_limit_bytes=VMEM_LIMIT)


def _dot(a, b):
    return jnp.dot(a, b, preferred_element_type=F32)


def _dot_nt(a, b):
    return lax.dot_general(a, b, (((1,), (1,)), ((), ())), preferred_element_type=F32)


def _dot_tn(a, b):
    return lax.dot_general(a, b, (((0,), (0,)), ((), ())), preferred_element_type=F32)


def _split_bf16(x):
    hi = x.astype(BF16)
    lo = (x - hi.astype(F32)).astype(BF16)
    return hi, lo


def _layer_norm(z, w, b):
    mu = jnp.mean(z, axis=-1, keepdims=True)
    zc = z - mu
    var = jnp.mean(zc * zc, axis=-1, keepdims=True)
    return zc * lax.rsqrt(var + LN_EPS) * w + b


def _sigmoid(x):
    return 1.0 / (1.0 + jnp.exp(-x))


def _softplus(x):
    return jnp.maximum(x, 0.0) + jnp.log(1.0 + jnp.exp(-jnp.abs(x)))


def _tril_mask(n, strict):
    r = lax.broadcasted_iota(jnp.int32, (n, n), 0)
    c = lax.broadcasted_iota(jnp.int32, (n, n), 1)
    return (c < r) if strict else (c <= r)


def _inproj_kernel(apply_ln, x_ref, lnw_ref, lnb_ref, wr_ref, wg_ref, *out_refs):
    x = x_ref[...]
    if apply_ln:
        x0_ref, pr_ref, pg_ref = out_refs
        x = _layer_norm(x, lnw_ref[...], lnb_ref[...])
        x0_ref[...] = x
    else:
        pr_ref, pg_ref = out_refs
    xb = x.astype(BF16)
    pr_ref[...] = _dot(xb, wr_ref[...]).astype(BF16)
    pg_ref[...] = _dot(xb, wg_ref[...]).astype(BF16)


def _inproj(x, lnw, lnb, wr, wg, apply_ln):
    t = x.shape[0]
    tm = min(512, t)
    row = lambda i: (i, 0)
    const = lambda i: (0, 0)
    out_shape = [jax.ShapeDtypeStruct((t, RW_WIDTH), BF16), jax.ShapeDtypeStruct((t, GL_WIDTH), BF16)]
    out_specs = [pl.BlockSpec((tm, RW_WIDTH), row), pl.BlockSpec((tm, GL_WIDTH), row)]
    if apply_ln:
        out_shape = [jax.ShapeDtypeStruct((t, D_MODEL), F32)] + out_shape
        out_specs = [pl.BlockSpec((tm, D_MODEL), row)] + out_specs
    return pl.pallas_call(
        functools.partial(_inproj_kernel, apply_ln),
        grid=(t // tm,),
        in_specs=[pl.BlockSpec((tm, D_MODEL), row),
                  pl.BlockSpec((1, D_MODEL), const), pl.BlockSpec((1, D_MODEL), const),
                  pl.BlockSpec((D_MODEL, RW_WIDTH), const), pl.BlockSpec((D_MODEL, GL_WIDTH), const)],
        out_specs=out_specs,
        out_shape=out_shape,
        compiler_params=_cparams(1),
        name="inproj_ln" if apply_ln else "inproj",
    )(x, lnw, lnb, wr, wg)


def _head_sum(x, ones_blk):
    hi, lo = _split_bf16(x)
    return _dot(hi, ones_blk) + _dot(lo, ones_blk)


def _rwkv_kernel(has_vres, *refs):
    if has_vres:
        (p_ref, vfirst_ref, mu_ref, w0_ref, wup_ref, a0_ref, aup_ref, gup_ref, kk_ref, ka_ref, rk_ref,
         gnw_ref, gnb_ref, v0_ref, vdown_ref, vup_ref, y_ref, state_ref, prev_ref) = refs
    else:
        (p_ref, mu_ref, w0_ref, wup_ref, a0_ref, aup_ref, gup_ref, kk_ref, ka_ref, rk_ref,
         gnw_ref, gnb_ref, y_ref, vfirst_out_ref, state_ref, prev_ref) = refs

    @pl.when(pl.program_id(1) == 0)
    def _():
        state_ref[...] = jnp.zeros_like(state_ref)
        prev_ref[...] = jnp.zeros_like(prev_ref)

    p = p_ref[0].astype(F32)
    n_tok = p.shape[0]
    row = lax.broadcasted_iota(jnp.int32, (n_tok, 1), 0)
    prev = jnp.where(row == 0, prev_ref[...], pltpu.roll(p, 1, axis=0))
    prev_ref[...] = p[n_tok - 1:n_tok, :]
    xs = p + (prev - p) * mu_ref[...]
    r = xs[:, 0:D_RWKV]
    k = xs[:, D_RWKV:2 * D_RWKV]
    v = xs[:, 2 * D_RWKV:3 * D_RWKV]
    wd = xs[:, RW_WD:RW_AD]
    ad = xs[:, RW_AD:RW_GD]
    gd = xs[:, RW_GD:RW_WIDTH]

    w = w0_ref[...] + _dot(jnp.tanh(wd).astype(BF16), wup_ref[...])
    w = -_softplus(-w) - 0.5
    logw = -jnp.exp(w)
    a = _sigmoid(a0_ref[...] + _dot(ad.astype(BF16), aup_ref[...]))
    g = _dot(_sigmoid(gd).astype(BF16), gup_ref[...])
    if has_vres:
        vmix = _dot(_dot(v.astype(BF16), vdown_ref[...]).astype(BF16), vup_ref[...])
        v = v + (vfirst_ref[0] - v) * _sigmoid(v0_ref[...] + vmix)
    else:
        vfirst_out_ref[0] = v

    hr = lax.broadcasted_iota(jnp.int32, (D_RWKV, D_RWKV), 0) // RWKV_HEAD
    hc = lax.broadcasted_iota(jnp.int32, (D_RWKV, D_RWKV), 1) // RWKV_HEAD
    ones_blk = (hr == hc).astype(BF16)

    kk = k * kk_ref[...]
    kk = kk * lax.rsqrt(jnp.maximum(_head_sum(kk * kk, ones_blk), 1e-24))
    k = k * (1.0 + (a - 1.0) * ka_ref[...])
    bonus = _head_sum(r * k * rk_ref[...], ones_blk)

    n_chunks = n_tok // CHUNK
    c2 = 2 * CHUNK
    r_i = lax.broadcasted_iota(jnp.int32, (CHUNK, c2), 0)
    c_i = lax.broadcasted_iota(jnp.int32, (CHUNK, c2), 1)
    c_half = jnp.where(c_i >= CHUNK, c_i - CHUNK, c_i)
    m_strict_l = ((c_i < CHUNK) & (c_i < r_i)).astype(F32)
    m_strict_r = ((c_i >= CHUNK) & (c_half < r_i)).astype(F32)
    m_lower2 = (c_half <= r_i).astype(F32)
    rr = lax.broadcasted_iota(jnp.int32, (CHUNK, CHUNK), 0)
    cc = lax.broadcasted_iota(jnp.int32, (CHUNK, CHUNK), 1)
    tril_f = (cc <= rr).astype(F32)
    eye = (rr == cc).astype(F32)
    m_diag2 = ((rr // 2) == (cc // 2)).astype(F32)
    lvl_masks = []
    size = 2
    while size < CHUNK:
        lvl_masks.append((((rr // (2 * size)) == (cc // (2 * size))) & ((rr // size) != (cc // size))).astype(F32))
        size *= 2

    items = []
    g_ends = []
    for c in range(n_chunks):
        sl = slice(c * CHUNK, (c + 1) * CHUNK)
        lw = logw[sl]
        b = jnp.dot(tril_f, lw, precision=HIGHEST, preferred_element_type=F32)
        eb = jnp.exp(b)
        enb = jnp.exp(-b)
        ebx = jnp.exp(b - lw)
        rh = (r[sl] * eb).astype(BF16)
        kh = (k[sl] * enb).astype(BF16)
        ah = (-kk[sl] * ebx).astype(BF16)
        bh = (kk[sl] * a[sl] * enb).astype(BF16)
        vb = v[sl].astype(BF16)
        g_ends.append(eb[CHUNK - 1:CHUNK, :])
        for h in range(RWKV_HEADS):
            hs = slice(h * RWKV_HEAD, (h + 1) * RWKV_HEAD)
            items.append(dict(ah=ah[:, hs], rh=rh[:, hs], v=vb[:, hs],
                              bk=jnp.concatenate([bh[:, hs], kh[:, hs]], axis=0)))

    for it in items:
        it["s2"] = _dot_nt(jnp.concatenate([it["ah"], it["rh"]], axis=0), it["bk"])
    for it in items:
        top = it["s2"][:CHUNK]
        it["a_ab"] = (top * m_strict_l)[:, :CHUNK]
        it["ak"] = (top * m_strict_r).astype(BF16)
        it["rbk"] = (it["s2"][CHUNK:] * m_lower2).astype(BF16)
        it["d"] = eye + it["a_ab"] * m_diag2
        del it["s2"]
    for mk in lvl_masks:
        for it in items:
            it["db"] = it["d"].astype(BF16)
            it["m"] = _dot((it["a_ab"] * mk).astype(BF16), it["db"])
        for it in items:
            it["d"] = it["d"] + _dot(it["db"], it["m"].astype(BF16))
    for it in items:
        it["akv"] = _dot(it["ak"], jnp.concatenate([it["v"], it["v"]], axis=0))
        it["tinv"] = it["d"].astype(BF16)
    for it in items:
        it["w"] = _dot(it["tinv"], it["ah"])
        it["ut"] = _dot(it["tinv"], it["akv"].astype(BF16))

    states = [state_ref[h] for h in range(RWKV_HEADS)]
    y_chunks = []
    for c in range(n_chunks):
        its = items[c * RWKV_HEADS:(c + 1) * RWKV_HEADS]
        for h, it in enumerate(its):
            it["ws"] = _dot_nt(jnp.concatenate([it["w"].astype(BF16), it["rh"]], axis=0), states[h].astype(BF16))
        for h, it in enumerate(its):
            it["uv"] = jnp.concatenate([(it["ws"][:CHUNK] + it["ut"]).astype(BF16), it["v"]], axis=0)
            hs = slice(h * RWKV_HEAD, (h + 1) * RWKV_HEAD)
            states[h] = (states[h] + _dot_tn(it["uv"], it["bk"])) * g_ends[c][:, hs]
        y_heads = [it["ws"][CHUNK:] + _dot(it["rbk"], it["uv"]) for it in its]
        y_chunks.append(jnp.concatenate(y_heads, axis=1))
    for h in range(RWKV_HEADS):
        state_ref[h] = states[h]
    y = jnp.concatenate(y_chunks, axis=0) if len(y_chunks) > 1 else y_chunks[0]

    inv_n = 1.0 / RWKV_HEAD
    m = _head_sum(y, ones_blk) * inv_n
    yc = y - m
    var = _head_sum(yc * yc, ones_blk) * inv_n
    y = yc * lax.rsqrt(var + RWKV_GN_EPS) * gnw_ref[...] + gnb_ref[...]
    y = y + bonus * v
    y_ref[0] = (y * g).astype(BF16)


def _rwkv_mixer(p_r, vfirst, prm, has_vres):
    bsz, seq, _ = p_r.shape
    n_tok = min(128, seq)
    tok = lambda b, i: (b, i, 0)
    const = lambda b, i: (0, 0)
    vec = pl.BlockSpec((1, D_RWKV), const)
    in_specs = [pl.BlockSpec((1, n_tok, RW_WIDTH), tok)]
    args = [p_r]
    if has_vres:
        in_specs.append(pl.BlockSpec((1, n_tok, D_RWKV), tok))
        args.append(vfirst)
    in_specs += [pl.BlockSpec((1, RW_WIDTH), const), vec, pl.BlockSpec((LANE, D_RWKV), const), vec,
                 pl.BlockSpec((LANE, D_RWKV), const), pl.BlockSpec((2 * LANE, D_RWKV), const),
                 vec, vec, vec, vec, vec]
    args += [prm["mu"], prm["w0"], prm["w_up"], prm["a0"], prm["a_up"], prm["g_up"],
             prm["k_k"], prm["k_a"], prm["r_k"], prm["gn_w"], prm["gn_b"]]
    out_shape = [jax.ShapeDtypeStruct((bsz, seq, D_RWKV), BF16)]
    out_specs = [pl.BlockSpec((1, n_tok, D_RWKV), tok)]
    if has_vres:
        in_specs += [vec, pl.BlockSpec((D_RWKV, LANE), const), pl.BlockSpec((LANE, D_RWKV), const)]
        args += [prm["v0"], prm["v_down"], prm["v_up"]]
    else:
        out_shape.append(jax.ShapeDtypeStruct((bsz, seq, D_RWKV), F32))
        out_specs.append(pl.BlockSpec((1, n_tok, D_RWKV), tok))
    res = pl.pallas_call(
        functools.partial(_rwkv_kernel, has_vres),
        grid=(bsz, seq // n_tok),
        in_specs=in_specs,
        out_specs=out_specs,
        out_shape=out_shape,
        scratch_shapes=[pltpu.VMEM((RWKV_HEADS, RWKV_HEAD, RWKV_HEAD), F32),
                        pltpu.VMEM((1, RW_WIDTH), F32)],
        compiler_params=_cparams(2),
        name="rwkv7_vres" if has_vres else "rwkv7",
    )(*args)
    if has_vres:
        return res[0], vfirst
    return res[0], res[1]


def _gla_kernel(p_ref, gkup_ref, gkb_ref, nw_ref, y_ref, state_ref):
    @pl.when(pl.program_id(1) == 0)
    def _():
        state_ref[...] = jnp.zeros_like(state_ref)

    p = p_ref[0].astype(F32)
    n_tok = p.shape[0]
    q = p[:, 0:GL_K]
    k = p[:, GL_K:GL_V]
    v = p[:, GL_V:GL_GK]
    gkd = p[:, GL_GK:GL_G]
    g = p[:, GL_G:GL_WIDTH]
    z = _dot(gkd.astype(BF16), gkup_ref[...]) + gkb_ref[...]
    gk = -_softplus(-z) * (1.0 / GLA_GATE_NORMALIZER)

    lower = _tril_mask(CHUNK, strict=False)
    tril_f = lower.astype(F32)
    o_chunks = []
    for c in range(n_tok // CHUNK):
        sl = slice(c * CHUNK, (c + 1) * CHUNK)
        b = jnp.dot(tril_f, gk[sl], precision=HIGHEST, preferred_element_type=F32)
        b_last = b[CHUNK - 1:CHUNK, :]
        q_e = (q[sl] * jnp.exp(b) * (GLA_DK ** -0.5)).astype(BF16)
        k_e = (k[sl] * jnp.exp(-b)).astype(BF16)
        k_end = (k[sl] * jnp.exp(b_last - b)).astype(BF16)
        dec = jnp.exp(b_last)
        vb = v[sl].astype(BF16)
        o_heads = []
        for h in range(GLA_HEADS):
            ks = slice(h * GLA_DK, (h + 1) * GLA_DK)
            vs = slice(h * GLA_DV, (h + 1) * GLA_DV)
            scores = jnp.where(lower, _dot_nt(q_e[:, ks], k_e[:, ks]), 0.0)
            s_old = state_ref[h]
            o_h = _dot(scores.astype(BF16), vb[:, vs]) + _dot_nt(q_e[:, ks], s_old.astype(BF16))
            state_ref[h] = s_old * dec[:, ks] + _dot_tn(vb[:, vs], k_end[:, ks])
            o_h = o_h * lax.rsqrt(jnp.mean(o_h * o_h, axis=-1, keepdims=True) + RMS_EPS)
            o_heads.append(o_h)
        o_chunks.append(jnp.concatenate(o_heads, axis=1))
    o = jnp.concatenate(o_chunks, axis=0) if len(o_chunks) > 1 else o_chunks[0]
    silu_g = g * _sigmoid(g)
    y_ref[0] = (o * nw_ref[...] * silu_g).astype(BF16)


def _gla_mixer(p_g, prm):
    bsz, seq, _ = p_g.shape
    n_tok = min(128, seq)
    tok = lambda b, i: (b, i, 0)
    const = lambda b, i: (0, 0)
    return pl.pallas_call(
        _gla_kernel,
        grid=(bsz, seq // n_tok),
        in_specs=[pl.BlockSpec((1, n_tok, GL_WIDTH), tok),
                  pl.BlockSpec((LANE, GL_K), const), pl.BlockSpec((1, GL_K), const),
                  pl.BlockSpec((1, D_GLA), const)],
        out_specs=pl.BlockSpec((1, n_tok, D_GLA), tok),
        out_shape=jax.ShapeDtypeStruct((bsz, seq, D_GLA), BF16),
        scratch_shapes=[pltpu.VMEM((GLA_HEADS, GLA_DV, GLA_DK), F32)],
        compiler_params=_cparams(2),
        name="gla",
    )(p_g, prm["gk_up"], prm["gk_b"], prm["norm_w"])


def _outproj_router_kernel(yr_ref, yg_ref, x_ref, wo_ref, lnw_ref, lnb_ref, rw_ref, rb_ref,
                           x1_ref, idx_ref, gate_ref, rank_ref, cnt_ref, carry_ref):
    @pl.when(pl.program_id(0) == 0)
    def _():
        carry_ref[...] = jnp.zeros_like(carry_ref)

    mix = _dot(yr_ref[...], wo_ref[0:D_RWKV, :]) + _dot(yg_ref[...], wo_ref[D_RWKV:, :])
    x1 = _layer_norm(DEEPNORM_ALPHA * x_ref[...] + mix, lnw_ref[...], lnb_ref[...])
    x1_ref[...] = x1
    tm = x1.shape[0]

    xh, xl = _split_bf16(x1)
    rw = rw_ref[...]
    wh, wl = _split_bf16(rw)
    logits = _dot_nt(wh, xh) + _dot_nt(wh, xl) + _dot_nt(wl, xh) + rb_ref[...]

    e_iota = lax.broadcasted_iota(jnp.int32, (N_EXPERTS, tm), 0)
    work = logits
    vals, idxs = [], []
    member = jnp.zeros((N_EXPERTS, tm), F32)
    for _ in range(TOP_K):
        mx = jnp.max(work, axis=0, keepdims=True)
        ix = jnp.min(jnp.where(work == mx, e_iota, N_EXPERTS), axis=0, keepdims=True)
        sel = e_iota == ix
        work = jnp.where(sel, -jnp.inf, work)
        member = jnp.where(sel, 1.0, member)
        vals.append(mx)
        idxs.append(ix)
    exps = [jnp.exp(vv - vals[0]) for vv in vals]
    inv_den = 1.0 / (exps[0] + exps[1] + exps[2] + exps[3])

    tr = lax.broadcasted_iota(jnp.int32, (tm, tm), 0)
    tc = lax.broadcasted_iota(jnp.int32, (tm, tm), 1)
    before = (tr < tc).astype(BF16)
    cex = _dot(member.astype(BF16), before) + carry_ref[...][:, 0:1]
    for kq in range(TOP_K):
        sel = e_iota == idxs[kq]
        idx_ref[kq:kq + 1, :] = idxs[kq]
        gate_ref[kq:kq + 1, :] = exps[kq] * inv_den
        rank_ref[kq:kq + 1, :] = jnp.sum(jnp.where(sel, cex, 0.0), axis=0, keepdims=True).astype(jnp.int32)
    carry_ref[...] = carry_ref[...] + jnp.sum(member, axis=1, keepdims=True)
    cnt_ref[...] = carry_ref[...].astype(jnp.int32)


def _outproj_router(yr, yg, x, wo, lnw, lnb, rw_t, rb):
    t = x.shape[0]
    tm = min(512, t)
    row = lambda i: (i, 0)
    col = lambda i: (0, i)
    const = lambda i: (0, 0)
    return pl.pallas_call(
        _outproj_router_kernel,
        grid=(t // tm,),
        in_specs=[pl.BlockSpec((tm, D_RWKV), row), pl.BlockSpec((tm, D_GLA), row),
                  pl.BlockSpec((tm, D_MODEL), row), pl.BlockSpec((D_MODEL, D_MODEL), const),
                  pl.BlockSpec((1, D_MODEL), const), pl.BlockSpec((1, D_MODEL), const),
                  pl.BlockSpec((N_EXPERTS, D_MODEL), const), pl.BlockSpec((N_EXPERTS, 1), const)],
        out_specs=[pl.BlockSpec((tm, D_MODEL), row),
                   pl.BlockSpec((TOP_K, tm), col), pl.BlockSpec((TOP_K, tm), col),
                   pl.BlockSpec((TOP_K, tm), col), pl.BlockSpec((N_EXPERTS, LANE), const)],
        out_shape=[jax.ShapeDtypeStruct((t, D_MODEL), F32),
                   jax.ShapeDtypeStruct((TOP_K, t), jnp.int32), jax.ShapeDtypeStruct((TOP_K, t), F32),
                   jax.ShapeDtypeStruct((TOP_K, t), jnp.int32),
                   jax.ShapeDtypeStruct((N_EXPERTS, LANE), jnp.int32)],
        scratch_shapes=[pltpu.VMEM((N_EXPERTS, LANE), F32)],
        compiler_params=_cparams(1),
        name="outproj_router",
    )(yr, yg, x, wo, lnw, lnb, rw_t, rb)


def _dispatch_kernel(dest_ref, gaps_ref, x_ref, xs_ref, zero_ref, sem, zsem):
    te = x_ref.shape[0]
    base = pl.program_id(0) * (TOP_K * te)

    @pl.when(pl.program_id(0) == 0)
    def _():
        zero_ref[...] = jnp.zeros_like(zero_ref)

        def zero_copy(d):
            return pltpu.make_async_copy(zero_ref, xs_ref.at[pl.ds(d, 1), :], zsem)

        for g in range(N_EXPERTS + 1):
            lo = gaps_ref[2 * g]
            hi = gaps_ref[2 * g + 1]
            lax.fori_loop(lo, hi, lambda d, c: (zero_copy(d).start(), c)[1], 0)
        for g in range(N_EXPERTS + 1):
            lo = gaps_ref[2 * g]
            hi = gaps_ref[2 * g + 1]
            lax.fori_loop(lo, hi, lambda d, c: (zero_copy(d).wait(), c)[1], 0)

    def issue(r, carry):
        for kq in range(TOP_K):
            d = dest_ref[base + kq * te + r]
            pltpu.make_async_copy(x_ref.at[pl.ds(r, 1), :], xs_ref.at[pl.ds(d, 1), :], sem).start(priority=kq % 2)
        return carry

    lax.fori_loop(0, te, issue, 0, unroll=4)
    for kq in range(TOP_K):
        pltpu.make_async_copy(x_ref, xs_ref.at[pl.ds(0, te), :], sem).wait()


def _dispatch(x1, dest_tiles, gaps, m_pad, te):
    t = x1.shape[0]
    return pl.pallas_call(
        _dispatch_kernel,
        grid_spec=pltpu.PrefetchScalarGridSpec(
            num_scalar_prefetch=2,
            grid=(t // te,),
            in_specs=[pl.BlockSpec((te, D_MODEL), lambda i, d, g: (i, 0))],
            out_specs=pl.BlockSpec(memory_space=pl.ANY),
            scratch_shapes=[pltpu.VMEM((1, D_MODEL), F32), pltpu.SemaphoreType.DMA(()),
                            pltpu.SemaphoreType.DMA(())],
        ),
        out_shape=jax.ShapeDtypeStruct((m_pad, D_MODEL), F32),
        compiler_params=_cparams(1),
        name="moe_dispatch",
    )(dest_tiles, gaps, x1)


def _expert_kernel(be_ref, nv_ref, xs_ref, w1_ref, b1_ref, w2_ref, b2_ref, ys_ref, act_ref, w1b_ref, w2b_ref):
    j = pl.program_id(0)
    n_valid = nv_ref[j]

    @pl.when((j == 0) | (be_ref[j] != be_ref[jnp.maximum(j - 1, 0)]))
    def _():
        rows = 128
        for c in range(D_MODEL // rows):
            rs = slice(c * rows, (c + 1) * rows)
            w1b_ref[rs, :] = w1_ref[0, 0, rs, :].astype(BF16)
            w2b_ref[rs, :] = w2_ref[0, 0, rs, :].astype(BF16)

    @pl.when(n_valid == 0)
    def _():
        ys_ref[...] = jnp.zeros_like(ys_ref)

    @pl.when(n_valid > 0)
    def _():
        xb = xs_ref[...].astype(BF16)
        n_chunk = 256
        for c in range(D_FF // n_chunk):
            cs = slice(c * n_chunk, (c + 1) * n_chunk)
            us = slice(D_FF + c * n_chunk, D_FF + (c + 1) * n_chunk)
            gt = _dot(xb, w1b_ref[:, cs]) + b1_ref[0, 0, :, cs]
            up = _dot(xb, w1b_ref[:, us]) + b1_ref[0, 0, :, us]
            gt = jnp.minimum(gt, SWIGLU_LIMIT)
            up = jnp.clip(up, -SWIGLU_LIMIT, SWIGLU_LIMIT)
            act_ref[:, cs] = ((up + 1.0) * gt * _sigmoid(SWIGLU_ALPHA * gt)).astype(BF16)
        ys_ref[...] = _dot(act_ref[...], w2b_ref[...]) + b2_ref[0, 0]


def _experts(xs, block_e, n_valid, layer, w1, b1, w2, b2):
    m_pad = xs.shape[0]
    n_blocks = m_pad // MOE_BLOCK
    wmap = lambda j, be, nv: (layer, be[j], 0, 0)
    return pl.pallas_call(
        _expert_kernel,
        grid_spec=pltpu.PrefetchScalarGridSpec(
            num_scalar_prefetch=2,
            grid=(n_blocks,),
            in_specs=[pl.BlockSpec((MOE_BLOCK, D_MODEL), lambda j, be, nv: (j, 0)),
                      pl.BlockSpec((1, 1, D_MODEL, 2 * D_FF), wmap),
                      pl.BlockSpec((1, 1, 1, 2 * D_FF), wmap),
                      pl.BlockSpec((1, 1, D_FF, D_MODEL), wmap),
                      pl.BlockSpec((1, 1, 1, D_MODEL), wmap)],
            out_specs=pl.BlockSpec((MOE_BLOCK, D_MODEL), lambda j, be, nv: (j, 0)),
            scratch_shapes=[pltpu.VMEM((MOE_BLOCK, D_FF), BF16),
                            pltpu.VMEM((D_MODEL, 2 * D_FF), BF16), pltpu.VMEM((D_FF, D_MODEL), BF16)],
        ),
        out_shape=jax.ShapeDtypeStruct((m_pad, D_MODEL), F32),
        compiler_params=_cparams(1),
        name="moe_experts",
    )(block_e, n_valid, xs, w1, b1, w2, b2)


def _combine_kernel(dest_ref, ys_ref, x1_ref, gate_ref, lnw_ref, lnb_ref, out_ref, buf_ref, sem):
    te = x1_ref.shape[0]
    i = pl.program_id(0)
    n_tiles = pl.num_programs(0)

    def gather(tile, slot):
        base = tile * (TOP_K * te)

        def issue(r, carry):
            for kq in range(TOP_K):
                d = dest_ref[base + kq * te + r]
                pltpu.make_async_copy(ys_ref.at[pl.ds(d, 1), :], buf_ref.at[slot, kq, pl.ds(r, 1), :],
                                      sem.at[slot]).start(priority=kq % 2)
            return carry

        lax.fori_loop(0, te, issue, 0, unroll=4)

    @pl.when(i == 0)
    def _():
        gather(0, 0)

    for slot in range(2):
        @pl.when((i + 1 < n_tiles) & ((i + 1) % 2 == slot))
        def _():
            gather(i + 1, slot)

    for slot in range(2):
        @pl.when(i % 2 == slot)
        def _():
            for kq in range(TOP_K):
                pltpu.make_async_copy(ys_ref.at[pl.ds(0, te), :], buf_ref.at[slot, kq], sem.at[slot]).wait()
            gates = gate_ref[...]
            z = DEEPNORM_ALPHA * x1_ref[...]
            for kq in range(TOP_K):
                z = z + buf_ref[slot, kq] * gates[:, kq:kq + 1]
            out_ref[...] = _layer_norm(z, lnw_ref[...], lnb_ref[...])


def _combine(ys, dest_tiles, x1, gates_t, lnw, lnb, te):
    t = x1.shape[0]
    return pl.pallas_call(
        _combine_kernel,
        grid_spec=pltpu.PrefetchScalarGridSpec(
            num_scalar_prefetch=1,
            grid=(t // te,),
            in_specs=[pl.BlockSpec(memory_space=pl.ANY),
                      pl.BlockSpec((te, D_MODEL), lambda i, d: (i, 0)),
                      pl.BlockSpec((te, TOP_K), lambda i, d: (i, 0)),
                      pl.BlockSpec((1, D_MODEL), lambda i, d: (0, 0)),
                      pl.BlockSpec((1, D_MODEL), lambda i, d: (0, 0))],
            out_specs=pl.BlockSpec((te, D_MODEL), lambda i, d: (i, 0)),
            scratch_shapes=[pltpu.VMEM((2, TOP_K, te, D_MODEL), F32), pltpu.SemaphoreType.DMA((2,))],
        ),
        out_shape=jax.ShapeDtypeStruct((t, D_MODEL), F32),
        compiler_params=_cparams(1),
        name="moe_combine",
    )(dest_tiles, ys, x1, gates_t, lnw, lnb)


def _moe(x1, idx, gate, rank, counts, layer, w1, b1, w2, b2, lnw, lnb):
    t = x1.shape[0]
    te = min(256, t)
    n_blocks = -(-(t * TOP_K) // MOE_BLOCK) + N_EXPERTS
    m_pad = n_blocks * MOE_BLOCK
    padded = ((counts + MOE_BLOCK - 1) // MOE_BLOCK) * MOE_BLOCK
    pend = jnp.cumsum(padded)
    pstart = pend - padded
    dest = rank
    for e in range(N_EXPERTS):
        dest = dest + jnp.where(idx == e, pstart[e], 0)
    dest_tiles = dest.reshape(TOP_K, t // te, te).transpose(1, 0, 2).reshape(-1)
    blk_start = jnp.arange(n_blocks, dtype=jnp.int32) * MOE_BLOCK
    block_e = jnp.minimum(jnp.sum(pend[None, :] <= blk_start[:, None], axis=1), N_EXPERTS - 1).astype(jnp.int32)
    is_e = block_e[:, None] == jnp.arange(N_EXPERTS, dtype=jnp.int32)[None, :]
    blk_end = jnp.sum(jnp.where(is_e, (pstart + counts)[None, :], 0), axis=1)
    n_valid = jnp.clip(blk_end - blk_start, 0, MOE_BLOCK).astype(jnp.int32)

    gap_lo = jnp.concatenate([pstart + counts, pend[-1:]])
    gap_hi = jnp.concatenate([pend, jnp.full((1,), m_pad, pend.dtype)])
    gaps = jnp.stack([gap_lo, gap_hi], axis=1).reshape(-1).astype(jnp.int32)

    xs = _dispatch(x1, dest_tiles, gaps, m_pad, te)
    ys = _experts(xs, block_e, n_valid, layer, w1, b1, w2, b2)
    return _combine(ys, dest_tiles, x1, gate.T, lnw, lnb, te)


def _pad_cols(w, width):
    return jnp.pad(w, ((0, 0), (0, width - w.shape[1])))


def _pad_rows(w, height):
    return jnp.pad(w, ((0, height - w.shape[0]), (0, 0)))


def _pack_inproj(w_in_l, mu_l):
    d3 = 3 * D_RWKV
    o_ad = d3 + W_LORA
    o_gd = o_ad + A_LORA
    o_gla = o_gd + G_LORA
    wr = jnp.concatenate([w_in_l[:, :d3],
                          _pad_cols(w_in_l[:, d3:o_ad], LANE),
                          _pad_cols(w_in_l[:, o_ad:o_gd], LANE),
                          _pad_cols(w_in_l[:, o_gd:o_gla], 2 * LANE)], axis=1)
    mu = mu_l[None, :]
    mu_p = jnp.concatenate([mu[:, :d3], _pad_cols(mu[:, d3:o_ad], LANE), _pad_cols(mu[:, o_ad:o_gd], LANE),
                            _pad_cols(mu[:, o_gd:o_gla], 2 * LANE)], axis=1)
    g = w_in_l[:, o_gla:]
    wg = jnp.concatenate([g[:, :GL_GK], _pad_cols(g[:, GL_GK:GL_GK + GLA_GATE_RANK], LANE),
                          g[:, GL_GK + GLA_GATE_RANK:]], axis=1)
    return wr.astype(BF16), wg.astype(BF16), mu_p


def kernel(x, ln_in_w, ln_in_b, w_in, rwkv_mu, rwkv_w0, rwkv_w_up, rwkv_a0, rwkv_a_up, rwkv_g_up, rwkv_k_k, rwkv_k_a, rwkv_r_k, rwkv_gn_w, rwkv_gn_b, rwkv_v0, rwkv_v_down, rwkv_v_up, gla_gk_up, gla_gk_b, gla_norm_w, w_out, ln1_w, ln1_b, router_w, router_b, exp_w1, exp_b1, exp_w2, exp_b2, ln2_w, ln2_b):
    bsz, seq, d = x.shape
    t = bsz * seq
    xf = x.reshape(t, d)
    vfirst = None
    for l in range(DEPTH):
        wr, wg, mu_p = _pack_inproj(w_in[l], rwkv_mu[l])
        if l == 0:
            xf, p_r, p_g = _inproj(xf, ln_in_w[None, :], ln_in_b[None, :], wr, wg, apply_ln=True)
        else:
            p_r, p_g = _inproj(xf, ln_in_w[None, :], ln_in_b[None, :], wr, wg, apply_ln=False)
        rprm = {
            "mu": mu_p, "w0": rwkv_w0[l][None, :], "w_up": _pad_rows(rwkv_w_up[l], LANE).astype(BF16),
            "a0": rwkv_a0[l][None, :], "a_up": _pad_rows(rwkv_a_up[l], LANE).astype(BF16),
            "g_up": _pad_rows(rwkv_g_up[l], 2 * LANE).astype(BF16),
            "k_k": rwkv_k_k[l][None, :], "k_a": rwkv_k_a[l][None, :], "r_k": rwkv_r_k[l][None, :],
            "gn_w": rwkv_gn_w[l][None, :], "gn_b": rwkv_gn_b[l][None, :],
        }
        if l > 0:
            rprm["v0"] = rwkv_v0[l - 1][None, :]
            rprm["v_down"] = _pad_cols(rwkv_v_down[l - 1], LANE).astype(BF16)
            rprm["v_up"] = _pad_rows(rwkv_v_up[l - 1], LANE).astype(BF16)
        y_r, vfirst = _rwkv_mixer(p_r.reshape(bsz, seq, RW_WIDTH), vfirst, rprm, has_vres=l > 0)
        gprm = {"gk_up": _pad_rows(gla_gk_up[l], LANE).astype(BF16), "gk_b": gla_gk_b[l][None, :],
                "norm_w": jnp.tile(gla_norm_w[l], GLA_HEADS)[None, :]}
        y_g = _gla_mixer(p_g.reshape(bsz, seq, GL_WIDTH), gprm)
        x1, idx, gate, rank, cnt = _outproj_router(
            y_r.reshape(t, D_RWKV), y_g.reshape(t, D_GLA), xf, w_out[l].astype(BF16),
            ln1_w[l][None, :], ln1_b[l][None, :], router_w[l].T, router_b[l][:, None])
        xf = _moe(x1, idx, gate, rank, cnt[:, 0], l, exp_w1, exp_b1[:, :, None, :],
                  exp_w2, exp_b2[:, :, None, :], ln2_w[l][None, :], ln2_b[l][None, :])
    return xf.reshape(bsz, seq, d)
```

```python
import functools

import jax
import jax.numpy as jnp
from jax import lax
from jax.experimental import pallas as pl
from jax.experimental.pallas import tpu as pltpu

F32 = jnp.float32
BF16 = jnp.bfloat16
HIGHEST = lax.Precision.HIGHEST

D_MODEL = 1024
DEPTH = 2
CHUNK = 64
RWKV_HEAD = 64
D_RWKV = 512
RWKV_HEADS = 8
W_LORA = 64
A_LORA = 64
V_LORA = 32
G_LORA = 160
RWKV_GN_EPS = 64e-5
D_GLA = 512
GLA_HEADS = 4
GLA_DV = 128
GLA_DK = 64
GLA_GATE_RANK = 16
GLA_GATE_NORMALIZER = 16.0
RMS_EPS = 1e-6
N_EXPERTS = 32
TOP_K = 4
D_FF = 1024
SWIGLU_LIMIT = 7.0
SWIGLU_ALPHA = 1.702
MOE_BLOCK = 256
DEEPNORM_ALPHA = (2 * DEPTH) ** 0.25
LN_EPS = 1e-5

LANE = 128
SUBLANE = 8
ROW_R = D_MODEL // LANE
assert ROW_R == SUBLANE
RW_WD = 3 * D_RWKV
RW_AD = RW_WD + LANE
RW_GD = RW_AD + LANE
RW_WIDTH = RW_GD + 2 * LANE
GL_K = GLA_HEADS * GLA_DK
GL_V = 2 * GL_K
GL_GK = GL_V + D_GLA
GL_G = GL_GK + LANE
GL_WIDTH = GL_G + D_GLA

VMEM_LIMIT = 56 * 1024 * 1024


def _cparams(n_axes):
    return pltpu.CompilerParams(dimension_semantics=("arbitrary",) * n_axes,
                                vmem_limit_bytes=VMEM_LIMIT)


def _dot(a, b):
    return jnp.dot(a, b, preferred_element_type=F32)


def _dot_nt(a, b):
    return lax.dot_general(a, b, (((1,), (1,)), ((), ())), preferred_element_type=F32)


def _dot_tn(a, b):
    return lax.dot_general(a, b, (((0,), (0,)), ((), ())), preferred_element_type=F32)


def _split_bf16(x):
    hi = x.astype(BF16)
    lo = (x - hi.astype(F32)).astype(BF16)
    return hi, lo


def _layer_norm(z, w, b):
    mu = jnp.mean(z, axis=-1, keepdims=True)
    zc = z - mu
    var = jnp.mean(zc * zc, axis=-1, keepdims=True)
    return zc * lax.rsqrt(var + LN_EPS) * w + b


def _sigmoid(x):
    return 1.0 / (1.0 + jnp.exp(-x))


def _softplus(x):
    return jnp.maximum(x, 0.0) + jnp.log(1.0 + jnp.exp(-jnp.abs(x)))


def _rows_load(ref, n):
    return jnp.concatenate([ref[pl.ds(s, n, stride=ROW_R), :] for s in range(ROW_R)], axis=1)


def _rows_store(ref, val):
    for s in range(ROW_R):
        ref[pl.ds(s, val.shape[0], stride=ROW_R), :] = val[:, s * LANE:(s + 1) * LANE]


def _row_tile(ref, row):
    return ref.at[pl.ds(pl.multiple_of(row * ROW_R, ROW_R), ROW_R), :]


def _tril_mask(n, strict):
    r = lax.broadcasted_iota(jnp.int32, (n, n), 0)
    c = lax.broadcasted_iota(jnp.int32, (n, n), 1)
    return (c < r) if strict else (c <= r)


def _inproj_kernel(apply_ln, x_ref, lnw_ref, lnb_ref, wr_ref, wg_ref, *out_refs):
    x = x_ref[...]
    if apply_ln:
        x0_ref, pr_ref, pg_ref = out_refs
        x = _layer_norm(x, lnw_ref[...], lnb_ref[...])
        x0_ref[...] = x
    else:
        pr_ref, pg_ref = out_refs
    xb = x.astype(BF16)
    pr_ref[...] = _dot(xb, wr_ref[...]).astype(BF16)
    pg_ref[...] = _dot(xb, wg_ref[...]).astype(BF16)


def _inproj(x, lnw, lnb, wr, wg, apply_ln):
    t = x.shape[0]
    tm = min(512, t)
    row = lambda i: (i, 0)
    const = lambda i: (0, 0)
    out_shape = [jax.ShapeDtypeStruct((t, RW_WIDTH), BF16), jax.ShapeDtypeStruct((t, GL_WIDTH), BF16)]
    out_specs = [pl.BlockSpec((tm, RW_WIDTH), row), pl.BlockSpec((tm, GL_WIDTH), row)]
    if apply_ln:
        out_shape = [jax.ShapeDtypeStruct((t, D_MODEL), F32)] + out_shape
        out_specs = [pl.BlockSpec((tm, D_MODEL), row)] + out_specs
    return pl.pallas_call(
        functools.partial(_inproj_kernel, apply_ln),
        grid=(t // tm,),
        in_specs=[pl.BlockSpec((tm, D_MODEL), row),
                  pl.BlockSpec((1, D_MODEL), const), pl.BlockSpec((1, D_MODEL), const),
                  pl.BlockSpec((D_MODEL, RW_WIDTH), const), pl.BlockSpec((D_MODEL, GL_WIDTH), const)],
        out_specs=out_specs,
        out_shape=out_shape,
        compiler_params=_cparams(1),
        name="inproj_ln" if apply_ln else "inproj",
    )(x, lnw, lnb, wr, wg)


def _head_sum(x, ones_blk):
    hi, lo = _split_bf16(x)
    return _dot(hi, ones_blk) + _dot(lo, ones_blk)


def _rwkv_kernel(has_vres, *refs):
    if has_vres:
        (p_ref, vfirst_ref, mu_ref, w0_ref, wup_ref, a0_ref, aup_ref, gup_ref, kk_ref, ka_ref, rk_ref,
         gnw_ref, gnb_ref, v0_ref, vdown_ref, vup_ref, y_ref, state_ref, prev_ref) = refs
    else:
        (p_ref, mu_ref, w0_ref, wup_ref, a0_ref, aup_ref, gup_ref, kk_ref, ka_ref, rk_ref,
         gnw_ref, gnb_ref, y_ref, vfirst_out_ref, state_ref, prev_ref) = refs

    @pl.when(pl.program_id(1) == 0)
    def _():
        state_ref[...] = jnp.zeros_like(state_ref)
        prev_ref[...] = jnp.zeros_like(prev_ref)

    p = p_ref[0].astype(F32)
    n_tok = p.shape[0]
    row = lax.broadcasted_iota(jnp.int32, (n_tok, 1), 0)
    prev = jnp.where(row == 0, prev_ref[...], pltpu.roll(p, 1, axis=0))
    prev_ref[...] = p[n_tok - 1:n_tok, :]
    xs = p + (prev - p) * mu_ref[...]
    r = xs[:, 0:D_RWKV]
    k = xs[:, D_RWKV:2 * D_RWKV]
    v = xs[:, 2 * D_RWKV:3 * D_RWKV]
    wd = xs[:, RW_WD:RW_AD]
    ad = xs[:, RW_AD:RW_GD]
    gd = xs[:, RW_GD:RW_WIDTH]

    w = w0_ref[...] + _dot(jnp.tanh(wd).astype(BF16), wup_ref[...])
    w = -_softplus(-w) - 0.5
    logw = -jnp.exp(w)
    a = _sigmoid(a0_ref[...] + _dot(ad.astype(BF16), aup_ref[...]))
    g = _dot(_sigmoid(gd).astype(BF16), gup_ref[...])
    if has_vres:
        vmix = _dot(_dot(v.astype(BF16), vdown_ref[...]).astype(BF16), vup_ref[...])
        v = v + (vfirst_ref[0] - v) * _sigmoid(v0_ref[...] + vmix)
    else:
        vfirst_out_ref[0] = v

    hr = lax.broadcasted_iota(jnp.int32, (D_RWKV, D_RWKV), 0) // RWKV_HEAD
    hc = lax.broadcasted_iota(jnp.int32, (D_RWKV, D_RWKV), 1) // RWKV_HEAD
    ones_blk = (hr == hc).astype(BF16)

    kk = k * kk_ref[...]
    kk = kk * lax.rsqrt(jnp.maximum(_head_sum(kk * kk, ones_blk), 1e-24))
    k = k * (1.0 + (a - 1.0) * ka_ref[...])
    bonus = _head_sum(r * k * rk_ref[...], ones_blk)

    n_chunks = n_tok // CHUNK
    c2 = 2 * CHUNK
    r_i = lax.broadcasted_iota(jnp.int32, (CHUNK, c2), 0)
    c_i = lax.broadcasted_iota(jnp.int32, (CHUNK, c2), 1)
    c_half = jnp.where(c_i >= CHUNK, c_i - CHUNK, c_i)
    m_strict_l = ((c_i < CHUNK) & (c_i < r_i)).astype(F32)
    m_strict_r = ((c_i >= CHUNK) & (c_half < r_i)).astype(F32)
    m_lower2 = (c_half <= r_i).astype(F32)
    rr = lax.broadcasted_iota(jnp.int32, (CHUNK, CHUNK), 0)
    cc = lax.broadcasted_iota(jnp.int32, (CHUNK, CHUNK), 1)
    tril_f = (cc <= rr).astype(F32)
    eye = (rr == cc).astype(F32)
    m_diag2 = ((rr // 2) == (cc // 2)).astype(F32)
    lvl_masks = []
    size = 2
    while size < CHUNK:
        lvl_masks.append((((rr // (2 * size)) == (cc // (2 * size))) & ((rr // size) != (cc // size))).astype(F32))
        size *= 2

    items = []
    g_ends = []
    for c in range(n_chunks):
        sl = slice(c * CHUNK, (c + 1) * CHUNK)
        lw = logw[sl]
        b = jnp.dot(tril_f, lw, precision=HIGHEST, preferred_element_type=F32)
        eb = jnp.exp(b)
        enb = jnp.exp(-b)
        ebx = jnp.exp(b - lw)
        rh = (r[sl] * eb).astype(BF16)
        kh = (k[sl] * enb).astype(BF16)
        ah = (-kk[sl] * ebx).astype(BF16)
        bh = (kk[sl] * a[sl] * enb).astype(BF16)
        vb = v[sl].astype(BF16)
        g_ends.append(eb[CHUNK - 1:CHUNK, :])
        for h in range(RWKV_HEADS):
            hs = slice(h * RWKV_HEAD, (h + 1) * RWKV_HEAD)
            items.append(dict(ah=ah[:, hs], rh=rh[:, hs], v=vb[:, hs],
                              bk=jnp.concatenate([bh[:, hs], kh[:, hs]], axis=0)))

    for it in items:
        it["s2"] = _dot_nt(jnp.concatenate([it["ah"], it["rh"]], axis=0), it["bk"])
    for it in items:
        top = it["s2"][:CHUNK]
        it["a_ab"] = (top * m_strict_l)[:, :CHUNK]
        it["ak"] = (top * m_strict_r).astype(BF16)
        it["rbk"] = (it["s2"][CHUNK:] * m_lower2).astype(BF16)
        it["d"] = eye + it["a_ab"] * m_diag2
        del it["s2"]
    for mk in lvl_masks:
        for it in items:
            it["db"] = it["d"].astype(BF16)
            it["m"] = _dot((it["a_ab"] * mk).astype(BF16), it["db"])
        for it in items:
            it["d"] = it["d"] + _dot(it["db"], it["m"].astype(BF16))
    for it in items:
        it["akv"] = _dot(it["ak"], jnp.concatenate([it["v"], it["v"]], axis=0))
        it["tinv"] = it["d"].astype(BF16)
    for it in items:
        it["w"] = _dot(it["tinv"], it["ah"])
        it["ut"] = _dot(it["tinv"], it["akv"].astype(BF16))

    states = [state_ref[h] for h in range(RWKV_HEADS)]
    y_chunks = []
    for c in range(n_chunks):
        its = items[c * RWKV_HEADS:(c + 1) * RWKV_HEADS]
        for h, it in enumerate(its):
            it["ws"] = _dot_nt(jnp.concatenate([it["w"].astype(BF16), it["rh"]], axis=0), states[h].astype(BF16))
        for h, it in enumerate(its):
            it["uv"] = jnp.concatenate([(it["ws"][:CHUNK] + it["ut"]).astype(BF16), it["v"]], axis=0)
            hs = slice(h * RWKV_HEAD, (h + 1) * RWKV_HEAD)
            states[h] = (states[h] + _dot_tn(it["uv"], it["bk"])) * g_ends[c][:, hs]
        y_heads = [it["ws"][CHUNK:] + _dot(it["rbk"], it["uv"]) for it in its]
        y_chunks.append(jnp.concatenate(y_heads, axis=1))
    for h in range(RWKV_HEADS):
        state_ref[h] = states[h]
    y = jnp.concatenate(y_chunks, axis=0) if len(y_chunks) > 1 else y_chunks[0]

    inv_n = 1.0 / RWKV_HEAD
    m = _head_sum(y, ones_blk) * inv_n
    yc = y - m
    var = _head_sum(yc * yc, ones_blk) * inv_n
    y = yc * lax.rsqrt(var + RWKV_GN_EPS) * gnw_ref[...] + gnb_ref[...]
    y = y + bonus * v
    y_ref[0] = (y * g).astype(BF16)


def _rwkv_mixer(p_r, vfirst, prm, has_vres):
    bsz, seq, _ = p_r.shape
    n_tok = min(128, seq)
    tok = lambda b, i: (b, i, 0)
    const = lambda b, i: (0, 0)
    vec = pl.BlockSpec((1, D_RWKV), const)
    in_specs = [pl.BlockSpec((1, n_tok, RW_WIDTH), tok)]
    args = [p_r]
    if has_vres:
        in_specs.append(pl.BlockSpec((1, n_tok, D_RWKV), tok))
        args.append(vfirst)
    in_specs += [pl.BlockSpec((1, RW_WIDTH), const), vec, pl.BlockSpec((LANE, D_RWKV), const), vec,
                 pl.BlockSpec((LANE, D_RWKV), const), pl.BlockSpec((2 * LANE, D_RWKV), const),
                 vec, vec, vec, vec, vec]
    args += [prm["mu"], prm["w0"], prm["w_up"], prm["a0"], prm["a_up"], prm["g_up"],
             prm["k_k"], prm["k_a"], prm["r_k"], prm["gn_w"], prm["gn_b"]]
    out_shape = [jax.ShapeDtypeStruct((bsz, seq, D_RWKV), BF16)]
    out_specs = [pl.BlockSpec((1, n_tok, D_RWKV), tok)]
    if has_vres:
        in_specs += [vec, pl.BlockSpec((D_RWKV, LANE), const), pl.BlockSpec((LANE, D_RWKV), const)]
        args += [prm["v0"], prm["v_down"], prm["v_up"]]
    else:
        out_shape.append(jax.ShapeDtypeStruct((bsz, seq, D_RWKV), F32))
        out_specs.append(pl.BlockSpec((1, n_tok, D_RWKV), tok))
    res = pl.pallas_call(
        functools.partial(_rwkv_kernel, has_vres),
        grid=(bsz, seq // n_tok),
        in_specs=in_specs,
        out_specs=out_specs,
        out_shape=out_shape,
        scratch_shapes=[pltpu.VMEM((RWKV_HEADS, RWKV_HEAD, RWKV_HEAD), F32),
                        pltpu.VMEM((1, RW_WIDTH), F32)],
        compiler_params=_cparams(2),
        name="rwkv7_vres" if has_vres else "rwkv7",
    )(*args)
    if has_vres:
        return res[0], vfirst
    return res[0], res[1]


def _gla_kernel(p_ref, gkup_ref, gkb_ref, nw_ref, y_ref, state_ref):
    @pl.when(pl.program_id(1) == 0)
    def _():
        state_ref[...] = jnp.zeros_like(state_ref)

    p = p_ref[0].astype(F32)
    n_tok = p.shape[0]
    q = p[:, 0:GL_K]
    k = p[:, GL_K:GL_V]
    v = p[:, GL_V:GL_GK]
    gkd = p[:, GL_GK:GL_G]
    g = p[:, GL_G:GL_WIDTH]
    z = _dot(gkd.astype(BF16), gkup_ref[...]) + gkb_ref[...]
    gk = -_softplus(-z) * (1.0 / GLA_GATE_NORMALIZER)

    lower = _tril_mask(CHUNK, strict=False)
    tril_f = lower.astype(F32)
    o_chunks = []
    for c in range(n_tok // CHUNK):
        sl = slice(c * CHUNK, (c + 1) * CHUNK)
        b = jnp.dot(tril_f, gk[sl], precision=HIGHEST, preferred_element_type=F32)
        b_last = b[CHUNK - 1:CHUNK, :]
        q_e = (q[sl] * jnp.exp(b) * (GLA_DK ** -0.5)).astype(BF16)
        k_e = (k[sl] * jnp.exp(-b)).astype(BF16)
        k_end = (k[sl] * jnp.exp(b_last - b)).astype(BF16)
        dec = jnp.exp(b_last)
        vb = v[sl].astype(BF16)
        o_heads = []
        for h in range(GLA_HEADS):
            ks = slice(h * GLA_DK, (h + 1) * GLA_DK)
            vs = slice(h * GLA_DV, (h + 1) * GLA_DV)
            scores = jnp.where(lower, _dot_nt(q_e[:, ks], k_e[:, ks]), 0.0)
            s_old = state_ref[h]
            o_h = _dot(scores.astype(BF16), vb[:, vs]) + _dot_nt(q_e[:, ks], s_old.astype(BF16))
            state_ref[h] = s_old * dec[:, ks] + _dot_tn(vb[:, vs], k_end[:, ks])
            o_h = o_h * lax.rsqrt(jnp.mean(o_h * o_h, axis=-1, keepdims=True) + RMS_EPS)
            o_heads.append(o_h)
        o_chunks.append(jnp.concatenate(o_heads, axis=1))
    o = jnp.concatenate(o_chunks, axis=0) if len(o_chunks) > 1 else o_chunks[0]
    silu_g = g * _sigmoid(g)
    y_ref[0] = (o * nw_ref[...] * silu_g).astype(BF16)


def _gla_mixer(p_g, prm):
    bsz, seq, _ = p_g.shape
    n_tok = min(128, seq)
    tok = lambda b, i: (b, i, 0)
    const = lambda b, i: (0, 0)
    return pl.pallas_call(
        _gla_kernel,
        grid=(bsz, seq // n_tok),
        in_specs=[pl.BlockSpec((1, n_tok, GL_WIDTH), tok),
                  pl.BlockSpec((LANE, GL_K), const), pl.BlockSpec((1, GL_K), const),
                  pl.BlockSpec((1, D_GLA), const)],
        out_specs=pl.BlockSpec((1, n_tok, D_GLA), tok),
        out_shape=jax.ShapeDtypeStruct((bsz, seq, D_GLA), BF16),
        scratch_shapes=[pltpu.VMEM((GLA_HEADS, GLA_DV, GLA_DK), F32)],
        compiler_params=_cparams(2),
        name="gla",
    )(p_g, prm["gk_up"], prm["gk_b"], prm["norm_w"])


def _outproj_router_kernel(yr_ref, yg_ref, x_ref, wo_ref, lnw_ref, lnb_ref, rw_ref, rb_ref,
                           x1_ref, idx_ref, gate_ref, rank_ref, cnt_ref, carry_ref):
    @pl.when(pl.program_id(0) == 0)
    def _():
        carry_ref[...] = jnp.zeros_like(carry_ref)

    mix = _dot(yr_ref[...], wo_ref[0:D_RWKV, :]) + _dot(yg_ref[...], wo_ref[D_RWKV:, :])
    x1 = _layer_norm(DEEPNORM_ALPHA * x_ref[...] + mix, lnw_ref[...], lnb_ref[...])
    _rows_store(x1_ref, x1)
    tm = x1.shape[0]

    xh, xl = _split_bf16(x1)
    rw = rw_ref[...]
    wh, wl = _split_bf16(rw)
    logits = _dot_nt(wh, xh) + _dot_nt(wh, xl) + _dot_nt(wl, xh) + rb_ref[...]

    e_iota = lax.broadcasted_iota(jnp.int32, (N_EXPERTS, tm), 0)
    work = logits
    vals, idxs = [], []
    member = jnp.zeros((N_EXPERTS, tm), F32)
    for _ in range(TOP_K):
        mx = jnp.max(work, axis=0, keepdims=True)
        ix = jnp.min(jnp.where(work == mx, e_iota, N_EXPERTS), axis=0, keepdims=True)
        sel = e_iota == ix
        work = jnp.where(sel, -jnp.inf, work)
        member = jnp.where(sel, 1.0, member)
        vals.append(mx)
        idxs.append(ix)
    exps = [jnp.exp(vv - vals[0]) for vv in vals]
    inv_den = 1.0 / (exps[0] + exps[1] + exps[2] + exps[3])

    tr = lax.broadcasted_iota(jnp.int32, (tm, tm), 0)
    tc = lax.broadcasted_iota(jnp.int32, (tm, tm), 1)
    before = (tr < tc).astype(BF16)
    cex = _dot(member.astype(BF16), before) + carry_ref[...][:, 0:1]
    for kq in range(TOP_K):
        sel = e_iota == idxs[kq]
        idx_ref[kq:kq + 1, :] = idxs[kq]
        gate_ref[kq:kq + 1, :] = exps[kq] * inv_den
        rank_ref[kq:kq + 1, :] = jnp.sum(jnp.where(sel, cex, 0.0), axis=0, keepdims=True).astype(jnp.int32)
    carry_ref[...] = carry_ref[...] + jnp.sum(member, axis=1, keepdims=True)
    cnt_ref[...] = carry_ref[...].astype(jnp.int32)


def _outproj_router(yr, yg, x, wo, lnw, lnb, rw_t, rb):
    t = x.shape[0]
    tm = min(512, t)
    row = lambda i: (i, 0)
    col = lambda i: (0, i)
    const = lambda i: (0, 0)
    return pl.pallas_call(
        _outproj_router_kernel,
        grid=(t // tm,),
        in_specs=[pl.BlockSpec((tm, D_RWKV), row), pl.BlockSpec((tm, D_GLA), row),
                  pl.BlockSpec((tm, D_MODEL), row), pl.BlockSpec((D_MODEL, D_MODEL), const),
                  pl.BlockSpec((1, D_MODEL), const), pl.BlockSpec((1, D_MODEL), const),
                  pl.BlockSpec((N_EXPERTS, D_MODEL), const), pl.BlockSpec((N_EXPERTS, 1), const)],
        out_specs=[pl.BlockSpec((tm * ROW_R, LANE), row),
                   pl.BlockSpec((TOP_K, tm), col), pl.BlockSpec((TOP_K, tm), col),
                   pl.BlockSpec((TOP_K, tm), col), pl.BlockSpec((N_EXPERTS, LANE), const)],
        out_shape=[jax.ShapeDtypeStruct((t * ROW_R, LANE), F32),
                   jax.ShapeDtypeStruct((TOP_K, t), jnp.int32), jax.ShapeDtypeStruct((TOP_K, t), F32),
                   jax.ShapeDtypeStruct((TOP_K, t), jnp.int32),
                   jax.ShapeDtypeStruct((N_EXPERTS, LANE), jnp.int32)],
        scratch_shapes=[pltpu.VMEM((N_EXPERTS, LANE), F32)],
        compiler_params=_cparams(1),
        name="outproj_router",
    )(yr, yg, x, wo, lnw, lnb, rw_t, rb)


def _dispatch_kernel(dest_ref, gaps_ref, x_ref, xs_ref, zero_ref, sem, zsem):
    te = x_ref.shape[0] // ROW_R
    base = pl.program_id(0) * (TOP_K * te)

    @pl.when(pl.program_id(0) == 0)
    def _():
        zero_ref[...] = jnp.zeros_like(zero_ref)

        def zero_copy(d):
            return pltpu.make_async_copy(zero_ref, _row_tile(xs_ref, d), zsem)

        for g in range(N_EXPERTS + 1):
            lo = gaps_ref[2 * g]
            hi = gaps_ref[2 * g + 1]
            lax.fori_loop(lo, hi, lambda d, c: (zero_copy(d).start(), c)[1], 0)
        for g in range(N_EXPERTS + 1):
            lo = gaps_ref[2 * g]
            hi = gaps_ref[2 * g + 1]
            lax.fori_loop(lo, hi, lambda d, c: (zero_copy(d).wait(), c)[1], 0)

    def issue(r, carry):
        for kq in range(TOP_K):
            d = dest_ref[base + kq * te + r]
            pltpu.make_async_copy(_row_tile(x_ref, r), _row_tile(xs_ref, d), sem).start(priority=kq % 2)
        return carry

    lax.fori_loop(0, te, issue, 0, unroll=8)
    for kq in range(TOP_K):
        pltpu.make_async_copy(x_ref, x_ref, sem).wait()


def _dispatch(x1, dest_tiles, gaps, m_pad, te):
    t = x1.shape[0] // ROW_R
    return pl.pallas_call(
        _dispatch_kernel,
        grid_spec=pltpu.PrefetchScalarGridSpec(
            num_scalar_prefetch=2,
            grid=(t // te,),
            in_specs=[pl.BlockSpec((te * ROW_R, LANE), lambda i, d, g: (i, 0))],
            out_specs=pl.BlockSpec(memory_space=pl.ANY),
            scratch_shapes=[pltpu.VMEM((ROW_R, LANE), F32), pltpu.SemaphoreType.DMA(()),
                            pltpu.SemaphoreType.DMA(())],
        ),
        out_shape=jax.ShapeDtypeStruct((m_pad * ROW_R, LANE), F32),
        compiler_params=_cparams(1),
        name="moe_dispatch",
    )(dest_tiles, gaps, x1)


def _expert_kernel(be_ref, nv_ref, xs_ref, w1_ref, b1_ref, w2_ref, b2_ref, ys_ref, act_ref, w1b_ref, w2b_ref):
    j = pl.program_id(0)
    n_valid = nv_ref[j]

    @pl.when((j == 0) | (be_ref[j] != be_ref[jnp.maximum(j - 1, 0)]))
    def _():
        rows = 128
        for c in range(D_MODEL // rows):
            rs = slice(c * rows, (c + 1) * rows)
            w1b_ref[rs, :] = w1_ref[0, 0, rs, :].astype(BF16)
            w2b_ref[rs, :] = w2_ref[0, 0, rs, :].astype(BF16)

    @pl.when(n_valid == 0)
    def _():
        ys_ref[...] = jnp.zeros_like(ys_ref)

    @pl.when(n_valid > 0)
    def _():
        xb = _rows_load(xs_ref, MOE_BLOCK).astype(BF16)
        n_chunk = 256
        for c in range(D_FF // n_chunk):
            cs = slice(c * n_chunk, (c + 1) * n_chunk)
            us = slice(D_FF + c * n_chunk, D_FF + (c + 1) * n_chunk)
            gt = _dot(xb, w1b_ref[:, cs]) + b1_ref[0, 0, :, cs]
            up = _dot(xb, w1b_ref[:, us]) + b1_ref[0, 0, :, us]
            gt = jnp.minimum(gt, SWIGLU_LIMIT)
            up = jnp.clip(up, -SWIGLU_LIMIT, SWIGLU_LIMIT)
            act_ref[:, cs] = ((up + 1.0) * gt * _sigmoid(SWIGLU_ALPHA * gt)).astype(BF16)
        _rows_store(ys_ref, _dot(act_ref[...], w2b_ref[...]) + b2_ref[0, 0])


def _experts(xs, block_e, n_valid, layer, w1, b1, w2, b2):
    m_pad = xs.shape[0] // ROW_R
    n_blocks = m_pad // MOE_BLOCK
    wmap = lambda j, be, nv: (layer, be[j], 0, 0)
    return pl.pallas_call(
        _expert_kernel,
        grid_spec=pltpu.PrefetchScalarGridSpec(
            num_scalar_prefetch=2,
            grid=(n_blocks,),
            in_specs=[pl.BlockSpec((MOE_BLOCK * ROW_R, LANE), lambda j, be, nv: (j, 0)),
                      pl.BlockSpec((1, 1, D_MODEL, 2 * D_FF), wmap),
                      pl.BlockSpec((1, 1, 1, 2 * D_FF), wmap),
                      pl.BlockSpec((1, 1, D_FF, D_MODEL), wmap),
                      pl.BlockSpec((1, 1, 1, D_MODEL), wmap)],
            out_specs=pl.BlockSpec((MOE_BLOCK * ROW_R, LANE), lambda j, be, nv: (j, 0)),
            scratch_shapes=[pltpu.VMEM((MOE_BLOCK, D_FF), BF16),
                            pltpu.VMEM((D_MODEL, 2 * D_FF), BF16), pltpu.VMEM((D_FF, D_MODEL), BF16)],
        ),
        out_shape=jax.ShapeDtypeStruct((m_pad * ROW_R, LANE), F32),
        compiler_params=_cparams(1),
        name="moe_experts",
    )(block_e, n_valid, xs, w1, b1, w2, b2)


def _combine_kernel(dest_ref, ys_ref, x1_ref, gate_ref, lnw_ref, lnb_ref, out_ref, buf_ref, sem):
    te = x1_ref.shape[0] // ROW_R
    i = pl.program_id(0)
    n_tiles = pl.num_programs(0)

    def gather(tile, slot):
        base = tile * (TOP_K * te)

        def issue(r, carry):
            for kq in range(TOP_K):
                d = dest_ref[base + kq * te + r]
                pltpu.make_async_copy(_row_tile(ys_ref, d), _row_tile(buf_ref.at[slot, kq], r),
                                      sem.at[slot]).start(priority=kq % 2)
            return carry

        lax.fori_loop(0, te, issue, 0, unroll=8)

    @pl.when(i == 0)
    def _():
        gather(0, 0)

    for slot in range(2):
        @pl.when((i + 1 < n_tiles) & ((i + 1) % 2 == slot))
        def _():
            gather(i + 1, slot)

    for slot in range(2):
        @pl.when(i % 2 == slot)
        def _():
            for kq in range(TOP_K):
                pltpu.make_async_copy(buf_ref.at[slot, kq], buf_ref.at[slot, kq], sem.at[slot]).wait()
            gates = gate_ref[...]
            z = DEEPNORM_ALPHA * _rows_load(x1_ref, te)
            for kq in range(TOP_K):
                z = z + _rows_load(buf_ref.at[slot, kq], te) * gates[:, kq:kq + 1]
            out_ref[...] = _layer_norm(z, lnw_ref[...], lnb_ref[...])


def _combine(ys, dest_tiles, x1, gates_t, lnw, lnb, te):
    t = x1.shape[0] // ROW_R
    return pl.pallas_call(
        _combine_kernel,
        grid_spec=pltpu.PrefetchScalarGridSpec(
            num_scalar_prefetch=1,
            grid=(t // te,),
            in_specs=[pl.BlockSpec(memory_space=pl.ANY),
                      pl.BlockSpec((te * ROW_R, LANE), lambda i, d: (i, 0)),
                      pl.BlockSpec((te, TOP_K), lambda i, d: (i, 0)),
                      pl.BlockSpec((1, D_MODEL), lambda i, d: (0, 0)),
                      pl.BlockSpec((1, D_MODEL), lambda i, d: (0, 0))],
            out_specs=pl.BlockSpec((te, D_MODEL), lambda i, d: (i, 0)),
            scratch_shapes=[pltpu.VMEM((2, TOP_K, te * ROW_R, LANE), F32), pltpu.SemaphoreType.DMA((2,))],
        ),
        out_shape=jax.ShapeDtypeStruct((t, D_MODEL), F32),
        compiler_params=_cparams(1),
        name="moe_combine",
    )(dest_tiles, ys, x1, gates_t, lnw, lnb)


def _moe(x1, idx, gate, rank, counts, layer, w1, b1, w2, b2, lnw, lnb):
    t = x1.shape[0] // ROW_R
    te = min(256, t)
    n_blocks = -(-(t * TOP_K) // MOE_BLOCK) + N_EXPERTS
    m_pad = n_blocks * MOE_BLOCK
    padded = ((counts + MOE_BLOCK - 1) // MOE_BLOCK) * MOE_BLOCK
    pend = jnp.cumsum(padded)
    pstart = pend - padded
    dest = rank
    for e in range(N_EXPERTS):
        dest = dest + jnp.where(idx == e, pstart[e], 0)
    dest_tiles = dest.reshape(TOP_K, t // te, te).transpose(1, 0, 2).reshape(-1)
    blk_start = jnp.arange(n_blocks, dtype=jnp.int32) * MOE_BLOCK
    block_e = jnp.minimum(jnp.sum(pend[None, :] <= blk_start[:, None], axis=1), N_EXPERTS - 1).astype(jnp.int32)
    is_e = block_e[:, None] == jnp.arange(N_EXPERTS, dtype=jnp.int32)[None, :]
    blk_end = jnp.sum(jnp.where(is_e, (pstart + counts)[None, :], 0), axis=1)
    n_valid = jnp.clip(blk_end - blk_start, 0, MOE_BLOCK).astype(jnp.int32)

    gap_lo = jnp.concatenate([pstart + counts, pend[-1:]])
    gap_hi = jnp.concatenate([pend, jnp.full((1,), m_pad, pend.dtype)])
    gaps = jnp.stack([gap_lo, gap_hi], axis=1).reshape(-1).astype(jnp.int32)

    xs = _dispatch(x1, dest_tiles, gaps, m_pad, te)
    ys = _experts(xs, block_e, n_valid, layer, w1, b1, w2, b2)
    return _combine(ys, dest_tiles, x1, gate.T, lnw, lnb, te)


def _pad_cols(w, width):
    return jnp.pad(w, ((0, 0), (0, width - w.shape[1])))


def _pad_rows(w, height):
    return jnp.pad(w, ((0, height - w.shape[0]), (0, 0)))


def _pack_inproj(w_in_l, mu_l):
    d3 = 3 * D_RWKV
    o_ad = d3 + W_LORA
    o_gd = o_ad + A_LORA
    o_gla = o_gd + G_LORA
    wr = jnp.concatenate([w_in_l[:, :d3],
                          _pad_cols(w_in_l[:, d3:o_ad], LANE),
                          _pad_cols(w_in_l[:, o_ad:o_gd], LANE),
                          _pad_cols(w_in_l[:, o_gd:o_gla], 2 * LANE)], axis=1)
    mu = mu_l[None, :]
    mu_p = jnp.concatenate([mu[:, :d3], _pad_cols(mu[:, d3:o_ad], LANE), _pad_cols(mu[:, o_ad:o_gd], LANE),
                            _pad_cols(mu[:, o_gd:o_gla], 2 * LANE)], axis=1)
    g = w_in_l[:, o_gla:]
    wg = jnp.concatenate([g[:, :GL_GK], _pad_cols(g[:, GL_GK:GL_GK + GLA_GATE_RANK], LANE),
                          g[:, GL_GK + GLA_GATE_RANK:]], axis=1)
    return wr.astype(BF16), wg.astype(BF16), mu_p


def kernel(x, ln_in_w, ln_in_b, w_in, rwkv_mu, rwkv_w0, rwkv_w_up, rwkv_a0, rwkv_a_up, rwkv_g_up, rwkv_k_k, rwkv_k_a, rwkv_r_k, rwkv_gn_w, rwkv_gn_b, rwkv_v0, rwkv_v_down, rwkv_v_up, gla_gk_up, gla_gk_b, gla_norm_w, w_out, ln1_w, ln1_b, router_w, router_b, exp_w1, exp_b1, exp_w2, exp_b2, ln2_w, ln2_b):
    bsz, seq, d = x.shape
    t = bsz * seq
    xf = x.reshape(t, d)
    vfirst = None
    for l in range(DEPTH):
        wr, wg, mu_p = _pack_inproj(w_in[l], rwkv_mu[l])
        if l == 0:
            xf, p_r, p_g = _inproj(xf, ln_in_w[None, :], ln_in_b[None, :], wr, wg, apply_ln=True)
        else:
            p_r, p_g = _inproj(xf, ln_in_w[None, :], ln_in_b[None, :], wr, wg, apply_ln=False)
        rprm = {
            "mu": mu_p, "w0": rwkv_w0[l][None, :], "w_up": _pad_rows(rwkv_w_up[l], LANE).astype(BF16),
            "a0": rwkv_a0[l][None, :], "a_up": _pad_rows(rwkv_a_up[l], LANE).astype(BF16),
            "g_up": _pad_rows(rwkv_g_up[l], 2 * LANE).astype(BF16),
            "k_k": rwkv_k_k[l][None, :], "k_a": rwkv_k_a[l][None, :], "r_k": rwkv_r_k[l][None, :],
            "gn_w": rwkv_gn_w[l][None, :], "gn_b": rwkv_gn_b[l][None, :],
        }
        if l > 0:
            rprm["v0"] = rwkv_v0[l - 1][None, :]
            rprm["v_down"] = _pad_cols(rwkv_v_down[l - 1], LANE).astype(BF16)
            rprm["v_up"] = _pad_rows(rwkv_v_up[l - 1], LANE).astype(BF16)
        y_r, vfirst = _rwkv_mixer(p_r.reshape(bsz, seq, RW_WIDTH), vfirst, rprm, has_vres=l > 0)
        gprm = {"gk_up": _pad_rows(gla_gk_up[l], LANE).astype(BF16), "gk_b": gla_gk_b[l][None, :],
                "norm_w": jnp.tile(gla_norm_w[l], GLA_HEADS)[None, :]}
        y_g = _gla_mixer(p_g.reshape(bsz, seq, GL_WIDTH), gprm)
        x1, idx, gate, rank, cnt = _outproj_router(
            y_r.reshape(t, D_RWKV), y_g.reshape(t, D_GLA), xf, w_out[l].astype(BF16),
            ln1_w[l][None, :], ln1_b[l][None, :], router_w[l].T, router_b[l][:, None])
        xf = _moe(x1, idx, gate, rank, cnt[:, 0], l, exp_w1, exp_b1[:, :, None, :],
                  exp_w2, exp_b2[:, :, None, :], ln2_w[l][None, :], ln2_b[l][None, :])
    return xf.reshape(bsz, seq, d)
```

```python
import functools

import jax
import jax.numpy as jnp
from jax import lax
from jax.experimental import pallas as pl
from jax.experimental.pallas import tpu as pltpu

F32 = jnp.float32
BF16 = jnp.bfloat16
HIGHEST = lax.Precision.HIGHEST

D_MODEL = 1024
DEPTH = 2
CHUNK = 64
RWKV_HEAD = 64
D_RWKV = 512
RWKV_HEADS = 8
W_LORA = 64
A_LORA = 64
V_LORA = 32
G_LORA = 160
RWKV_GN_EPS = 64e-5
D_GLA = 512
GLA_HEADS = 4
GLA_DV = 128
GLA_DK = 64
GLA_GATE_RANK = 16
GLA_GATE_NORMALIZER = 16.0
RMS_EPS = 1e-6
N_EXPERTS = 32
TOP_K = 4
D_FF = 1024
SWIGLU_LIMIT = 7.0
SWIGLU_ALPHA = 1.702
MOE_BLOCK = 512
DEEPNORM_ALPHA = (2 * DEPTH) ** 0.25
LN_EPS = 1e-5

LANE = 128
SUBLANE = 8
ROW_R = D_MODEL // LANE
assert ROW_R == SUBLANE
RW_WD = 3 * D_RWKV
RW_AD = RW_WD + LANE
RW_GD = RW_AD + LANE
RW_WIDTH = RW_GD + 2 * LANE
GL_K = GLA_HEADS * GLA_DK
GL_V = 2 * GL_K
GL_GK = GL_V + D_GLA
GL_G = GL_GK + LANE
GL_WIDTH = GL_G + D_GLA

VMEM_LIMIT = 56 * 1024 * 1024


def _cparams(n_axes):
    return pltpu.CompilerParams(dimension_semantics=("arbitrary",) * n_axes,
                                vmem_limit_bytes=VMEM_LIMIT)


def _dot(a, b):
    return jnp.dot(a, b, preferred_element_type=F32)


def _dot_nt(a, b):
    return lax.dot_general(a, b, (((1,), (1,)), ((), ())), preferred_element_type=F32)


def _dot_tn(a, b):
    return lax.dot_general(a, b, (((0,), (0,)), ((), ())), preferred_element_type=F32)


def _split_bf16(x):
    hi = x.astype(BF16)
    lo = (x - hi.astype(F32)).astype(BF16)
    return hi, lo


def _layer_norm(z, w, b):
    mu = jnp.mean(z, axis=-1, keepdims=True)
    zc = z - mu
    var = jnp.mean(zc * zc, axis=-1, keepdims=True)
    return zc * lax.rsqrt(var + LN_EPS) * w + b


def _sigmoid(x):
    return 1.0 / (1.0 + jnp.exp(-x))


def _softplus(x):
    return jnp.maximum(x, 0.0) + jnp.log(1.0 + jnp.exp(-jnp.abs(x)))


def _rows_load(ref, n):
    return jnp.concatenate([ref[pl.ds(s, n, stride=ROW_R), :] for s in range(ROW_R)], axis=1)


def _rows_store(ref, val):
    for s in range(ROW_R):
        ref[pl.ds(s, val.shape[0], stride=ROW_R), :] = val[:, s * LANE:(s + 1) * LANE]


def _row_tile(ref, row):
    return ref.at[pl.ds(pl.multiple_of(row * ROW_R, ROW_R), ROW_R), :]


def _tril_mask(n, strict):
    r = lax.broadcasted_iota(jnp.int32, (n, n), 0)
    c = lax.broadcasted_iota(jnp.int32, (n, n), 1)
    return (c < r) if strict else (c <= r)


def _inproj_kernel(x_ref, lnw_ref, lnb_ref, wr_ref, wg_ref, x0_ref, pr_ref, pg_ref):
    x = _layer_norm(x_ref[...], lnw_ref[...], lnb_ref[...])
    x0_ref[...] = x
    xb = x.astype(BF16)
    pr_ref[...] = _dot(xb, wr_ref[...]).astype(BF16)
    pg_ref[...] = _dot(xb, wg_ref[...]).astype(BF16)


def _inproj(x, lnw, lnb, wr, wg):
    t = x.shape[0]
    tm = min(512, t)
    row = lambda i: (i, 0)
    const = lambda i: (0, 0)
    return pl.pallas_call(
        _inproj_kernel,
        grid=(t // tm,),
        in_specs=[pl.BlockSpec((tm, D_MODEL), row),
                  pl.BlockSpec((1, D_MODEL), const), pl.BlockSpec((1, D_MODEL), const),
                  pl.BlockSpec((D_MODEL, RW_WIDTH), const), pl.BlockSpec((D_MODEL, GL_WIDTH), const)],
        out_specs=[pl.BlockSpec((tm, D_MODEL), row), pl.BlockSpec((tm, RW_WIDTH), row),
                   pl.BlockSpec((tm, GL_WIDTH), row)],
        out_shape=[jax.ShapeDtypeStruct((t, D_MODEL), F32), jax.ShapeDtypeStruct((t, RW_WIDTH), BF16),
                   jax.ShapeDtypeStruct((t, GL_WIDTH), BF16)],
        compiler_params=_cparams(1),
        name="inproj_ln",
    )(x, lnw, lnb, wr, wg)


def _head_sum(x, ones_blk):
    hi, lo = _split_bf16(x)
    return _dot(hi, ones_blk) + _dot(lo, ones_blk)


def _rwkv_kernel(has_vres, *refs):
    if has_vres:
        (p_ref, vfirst_ref, mu_ref, w0_ref, wup_ref, a0_ref, aup_ref, gup_ref, kk_ref, ka_ref, rk_ref,
         gnw_ref, gnb_ref, v0_ref, vdown_ref, vup_ref, y_ref, state_ref, prev_ref) = refs
    else:
        (p_ref, mu_ref, w0_ref, wup_ref, a0_ref, aup_ref, gup_ref, kk_ref, ka_ref, rk_ref,
         gnw_ref, gnb_ref, y_ref, vfirst_out_ref, state_ref, prev_ref) = refs

    @pl.when(pl.program_id(1) == 0)
    def _():
        state_ref[...] = jnp.zeros_like(state_ref)
        prev_ref[...] = jnp.zeros_like(prev_ref)

    p = p_ref[0].astype(F32)
    n_tok = p.shape[0]
    row = lax.broadcasted_iota(jnp.int32, (n_tok, 1), 0)
    prev = jnp.where(row == 0, prev_ref[...], pltpu.roll(p, 1, axis=0))
    prev_ref[...] = p[n_tok - 1:n_tok, :]
    xs = p + (prev - p) * mu_ref[...]
    r = xs[:, 0:D_RWKV]
    k = xs[:, D_RWKV:2 * D_RWKV]
    v = xs[:, 2 * D_RWKV:3 * D_RWKV]
    wd = xs[:, RW_WD:RW_AD]
    ad = xs[:, RW_AD:RW_GD]
    gd = xs[:, RW_GD:RW_WIDTH]

    w = w0_ref[...] + _dot(jnp.tanh(wd).astype(BF16), wup_ref[...])
    w = -_softplus(-w) - 0.5
    logw = -jnp.exp(w)
    a = _sigmoid(a0_ref[...] + _dot(ad.astype(BF16), aup_ref[...]))
    g = _dot(_sigmoid(gd).astype(BF16), gup_ref[...])
    if has_vres:
        vmix = _dot(_dot(v.astype(BF16), vdown_ref[...]).astype(BF16), vup_ref[...])
        v = v + (vfirst_ref[0] - v) * _sigmoid(v0_ref[...] + vmix)
    else:
        vfirst_out_ref[0] = v

    hr = lax.broadcasted_iota(jnp.int32, (D_RWKV, D_RWKV), 0) // RWKV_HEAD
    hc = lax.broadcasted_iota(jnp.int32, (D_RWKV, D_RWKV), 1) // RWKV_HEAD
    ones_blk = (hr == hc).astype(BF16)

    kk = k * kk_ref[...]
    kk = kk * lax.rsqrt(jnp.maximum(_head_sum(kk * kk, ones_blk), 1e-24))
    k = k * (1.0 + (a - 1.0) * ka_ref[...])
    bonus = _head_sum(r * k * rk_ref[...], ones_blk)

    n_chunks = n_tok // CHUNK
    c2 = 2 * CHUNK
    r_i = lax.broadcasted_iota(jnp.int32, (CHUNK, c2), 0)
    c_i = lax.broadcasted_iota(jnp.int32, (CHUNK, c2), 1)
    c_half = jnp.where(c_i >= CHUNK, c_i - CHUNK, c_i)
    m_strict_l = ((c_i < CHUNK) & (c_i < r_i)).astype(F32)
    m_strict_r = ((c_i >= CHUNK) & (c_half < r_i)).astype(F32)
    m_lower2 = (c_half <= r_i).astype(F32)
    rr = lax.broadcasted_iota(jnp.int32, (CHUNK, CHUNK), 0)
    cc = lax.broadcasted_iota(jnp.int32, (CHUNK, CHUNK), 1)
    tril_f = (cc <= rr).astype(F32)
    eye = (rr == cc).astype(F32)
    m_diag2 = ((rr // 2) == (cc // 2)).astype(F32)
    lvl_masks = []
    size = 2
    while size < CHUNK:
        lvl_masks.append((((rr // (2 * size)) == (cc // (2 * size))) & ((rr // size) != (cc // size))).astype(F32))
        size *= 2

    items = []
    g_ends = []
    for c in range(n_chunks):
        sl = slice(c * CHUNK, (c + 1) * CHUNK)
        lw = logw[sl]
        b = jnp.dot(tril_f, lw, precision=HIGHEST, preferred_element_type=F32)
        eb = jnp.exp(b)
        enb = jnp.exp(-b)
        ebx = jnp.exp(b - lw)
        rh = (r[sl] * eb).astype(BF16)
        kh = (k[sl] * enb).astype(BF16)
        ah = (-kk[sl] * ebx).astype(BF16)
        bh = (kk[sl] * a[sl] * enb).astype(BF16)
        vb = v[sl].astype(BF16)
        g_ends.append(eb[CHUNK - 1:CHUNK, :])
        for h in range(RWKV_HEADS):
            hs = slice(h * RWKV_HEAD, (h + 1) * RWKV_HEAD)
            items.append(dict(ah=ah[:, hs], rh=rh[:, hs], v=vb[:, hs],
                              bk=jnp.concatenate([bh[:, hs], kh[:, hs]], axis=0)))

    for it in items:
        it["s2"] = _dot_nt(jnp.concatenate([it["ah"], it["rh"]], axis=0), it["bk"])
    for it in items:
        top = it["s2"][:CHUNK]
        it["a_ab"] = (top * m_strict_l)[:, :CHUNK]
        it["ak"] = (top * m_strict_r).astype(BF16)
        it["rbk"] = (it["s2"][CHUNK:] * m_lower2).astype(BF16)
        it["d"] = eye + it["a_ab"] * m_diag2
        del it["s2"]
    for mk in lvl_masks:
        for it in items:
            it["db"] = it["d"].astype(BF16)
            it["m"] = _dot((it["a_ab"] * mk).astype(BF16), it["db"])
        for it in items:
            it["d"] = it["d"] + _dot(it["db"], it["m"].astype(BF16))
    for it in items:
        it["akv"] = _dot(it["ak"], jnp.concatenate([it["v"], it["v"]], axis=0))
        it["tinv"] = it["d"].astype(BF16)
    for it in items:
        it["w"] = _dot(it["tinv"], it["ah"])
        it["ut"] = _dot(it["tinv"], it["akv"].astype(BF16))

    states = [state_ref[h] for h in range(RWKV_HEADS)]
    y_chunks = []
    for c in range(n_chunks):
        its = items[c * RWKV_HEADS:(c + 1) * RWKV_HEADS]
        for h, it in enumerate(its):
            it["ws"] = _dot_nt(jnp.concatenate([it["w"].astype(BF16), it["rh"]], axis=0), states[h].astype(BF16))
        for h, it in enumerate(its):
            it["uv"] = jnp.concatenate([(it["ws"][:CHUNK] + it["ut"]).astype(BF16), it["v"]], axis=0)
            hs = slice(h * RWKV_HEAD, (h + 1) * RWKV_HEAD)
            states[h] = (states[h] + _dot_tn(it["uv"], it["bk"])) * g_ends[c][:, hs]
        y_heads = [it["ws"][CHUNK:] + _dot(it["rbk"], it["uv"]) for it in its]
        y_chunks.append(jnp.concatenate(y_heads, axis=1))
    for h in range(RWKV_HEADS):
        state_ref[h] = states[h]
    y = jnp.concatenate(y_chunks, axis=0) if len(y_chunks) > 1 else y_chunks[0]

    inv_n = 1.0 / RWKV_HEAD
    m = _head_sum(y, ones_blk) * inv_n
    yc = y - m
    var = _head_sum(yc * yc, ones_blk) * inv_n
    y = yc * lax.rsqrt(var + RWKV_GN_EPS) * gnw_ref[...] + gnb_ref[...]
    y = y + bonus * v
    y_ref[0] = (y * g).astype(BF16)


def _rwkv_mixer(p_r, vfirst, prm, has_vres):
    bsz, seq, _ = p_r.shape
    n_tok = min(256, seq)
    tok = lambda b, i: (b, i, 0)
    const = lambda b, i: (0, 0)
    vec = pl.BlockSpec((1, D_RWKV), const)
    in_specs = [pl.BlockSpec((1, n_tok, RW_WIDTH), tok)]
    args = [p_r]
    if has_vres:
        in_specs.append(pl.BlockSpec((1, n_tok, D_RWKV), tok))
        args.append(vfirst)
    in_specs += [pl.BlockSpec((1, RW_WIDTH), const), vec, pl.BlockSpec((LANE, D_RWKV), const), vec,
                 pl.BlockSpec((LANE, D_RWKV), const), pl.BlockSpec((2 * LANE, D_RWKV), const),
                 vec, vec, vec, vec, vec]
    args += [prm["mu"], prm["w0"], prm["w_up"], prm["a0"], prm["a_up"], prm["g_up"],
             prm["k_k"], prm["k_a"], prm["r_k"], prm["gn_w"], prm["gn_b"]]
    out_shape = [jax.ShapeDtypeStruct((bsz, seq, D_RWKV), BF16)]
    out_specs = [pl.BlockSpec((1, n_tok, D_RWKV), tok)]
    if has_vres:
        in_specs += [vec, pl.BlockSpec((D_RWKV, LANE), const), pl.BlockSpec((LANE, D_RWKV), const)]
        args += [prm["v0"], prm["v_down"], prm["v_up"]]
    else:
        out_shape.append(jax.ShapeDtypeStruct((bsz, seq, D_RWKV), F32))
        out_specs.append(pl.BlockSpec((1, n_tok, D_RWKV), tok))
    res = pl.pallas_call(
        functools.partial(_rwkv_kernel, has_vres),
        grid=(bsz, seq // n_tok),
        in_specs=in_specs,
        out_specs=out_specs,
        out_shape=out_shape,
        scratch_shapes=[pltpu.VMEM((RWKV_HEADS, RWKV_HEAD, RWKV_HEAD), F32),
                        pltpu.VMEM((1, RW_WIDTH), F32)],
        compiler_params=_cparams(2),
        name="rwkv7_vres" if has_vres else "rwkv7",
    )(*args)
    if has_vres:
        return res[0], vfirst
    return res[0], res[1]


def _gla_kernel(p_ref, gkup_ref, gkb_ref, nw_ref, y_ref, state_ref):
    @pl.when(pl.program_id(1) == 0)
    def _():
        state_ref[...] = jnp.zeros_like(state_ref)

    p = p_ref[0].astype(F32)
    n_tok = p.shape[0]
    q = p[:, 0:GL_K]
    k = p[:, GL_K:GL_V]
    v = p[:, GL_V:GL_GK]
    gkd = p[:, GL_GK:GL_G]
    g = p[:, GL_G:GL_WIDTH]
    z = _dot(gkd.astype(BF16), gkup_ref[...]) + gkb_ref[...]
    gk = -_softplus(-z) * (1.0 / GLA_GATE_NORMALIZER)

    lower = _tril_mask(CHUNK, strict=False)
    tril_f = lower.astype(F32)
    n_chunks = n_tok // CHUNK
    items = []
    decs = []
    for c in range(n_chunks):
        sl = slice(c * CHUNK, (c + 1) * CHUNK)
        b = jnp.dot(tril_f, gk[sl], precision=HIGHEST, preferred_element_type=F32)
        b_last = b[CHUNK - 1:CHUNK, :]
        q_e = (q[sl] * jnp.exp(b) * (GLA_DK ** -0.5)).astype(BF16)
        k_e = (k[sl] * jnp.exp(-b)).astype(BF16)
        k_end = (k[sl] * jnp.exp(b_last - b)).astype(BF16)
        decs.append(jnp.exp(b_last))
        vb = v[sl].astype(BF16)
        for h in range(GLA_HEADS):
            ks = slice(h * GLA_DK, (h + 1) * GLA_DK)
            vs = slice(h * GLA_DV, (h + 1) * GLA_DV)
            items.append(dict(q=q_e[:, ks], k=k_e[:, ks], kend=k_end[:, ks], v=vb[:, vs]))
    for it in items:
        it["sc"] = jnp.where(lower, _dot_nt(it["q"], it["k"]), 0.0).astype(BF16)
    for it in items:
        it["intra"] = _dot(it["sc"], it["v"])
        it["kv"] = _dot_tn(it["v"], it["kend"])
    states = [state_ref[h] for h in range(GLA_HEADS)]
    o_chunks = []
    for c in range(n_chunks):
        o_heads = []
        for h in range(GLA_HEADS):
            it = items[c * GLA_HEADS + h]
            ks = slice(h * GLA_DK, (h + 1) * GLA_DK)
            o_h = it["intra"] + _dot_nt(it["q"], states[h].astype(BF16))
            states[h] = states[h] * decs[c][:, ks] + it["kv"]
            o_heads.append(o_h * lax.rsqrt(jnp.mean(o_h * o_h, axis=-1, keepdims=True) + RMS_EPS))
        o_chunks.append(jnp.concatenate(o_heads, axis=1))
    for h in range(GLA_HEADS):
        state_ref[h] = states[h]
    o = jnp.concatenate(o_chunks, axis=0) if len(o_chunks) > 1 else o_chunks[0]
    silu_g = g * _sigmoid(g)
    y_ref[0] = (o * nw_ref[...] * silu_g).astype(BF16)


def _gla_mixer(p_g, prm):
    bsz, seq, _ = p_g.shape
    n_tok = min(256, seq)
    tok = lambda b, i: (b, i, 0)
    const = lambda b, i: (0, 0)
    return pl.pallas_call(
        _gla_kernel,
        grid=(bsz, seq // n_tok),
        in_specs=[pl.BlockSpec((1, n_tok, GL_WIDTH), tok),
                  pl.BlockSpec((LANE, GL_K), const), pl.BlockSpec((1, GL_K), const),
                  pl.BlockSpec((1, D_GLA), const)],
        out_specs=pl.BlockSpec((1, n_tok, D_GLA), tok),
        out_shape=jax.ShapeDtypeStruct((bsz, seq, D_GLA), BF16),
        scratch_shapes=[pltpu.VMEM((GLA_HEADS, GLA_DV, GLA_DK), F32)],
        compiler_params=_cparams(2),
        name="gla",
    )(p_g, prm["gk_up"], prm["gk_b"], prm["norm_w"])


def _outproj_router_kernel(yr_ref, yg_ref, x_ref, wo_ref, lnw_ref, lnb_ref, rw_ref, rb_ref,
                           x1_ref, idx_ref, gate_ref, rank_ref, cnt_ref, carry_ref):
    @pl.when(pl.program_id(0) == 0)
    def _():
        carry_ref[...] = jnp.zeros_like(carry_ref)

    mix = _dot(yr_ref[...], wo_ref[0:D_RWKV, :]) + _dot(yg_ref[...], wo_ref[D_RWKV:, :])
    x1 = _layer_norm(DEEPNORM_ALPHA * x_ref[...] + mix, lnw_ref[...], lnb_ref[...])
    _rows_store(x1_ref, x1)
    tm = x1.shape[0]

    xh, xl = _split_bf16(x1)
    rw = rw_ref[...]
    wh, wl = _split_bf16(rw)
    logits = _dot_nt(wh, xh) + _dot_nt(wh, xl) + _dot_nt(wl, xh) + rb_ref[...]

    e_iota = lax.broadcasted_iota(jnp.int32, (N_EXPERTS, tm), 0)
    work = logits
    vals, idxs = [], []
    member = jnp.zeros((N_EXPERTS, tm), F32)
    for _ in range(TOP_K):
        mx = jnp.max(work, axis=0, keepdims=True)
        ix = jnp.min(jnp.where(work == mx, e_iota, N_EXPERTS), axis=0, keepdims=True)
        sel = e_iota == ix
        work = jnp.where(sel, -jnp.inf, work)
        member = jnp.where(sel, 1.0, member)
        vals.append(mx)
        idxs.append(ix)
    exps = [jnp.exp(vv - vals[0]) for vv in vals]
    inv_den = 1.0 / (exps[0] + exps[1] + exps[2] + exps[3])

    tr = lax.broadcasted_iota(jnp.int32, (tm, tm), 0)
    tc = lax.broadcasted_iota(jnp.int32, (tm, tm), 1)
    before = (tr < tc).astype(BF16)
    cex = _dot(member.astype(BF16), before) + carry_ref[...][:, 0:1]
    for kq in range(TOP_K):
        sel = e_iota == idxs[kq]
        idx_ref[kq:kq + 1, :] = idxs[kq]
        gate_ref[kq:kq + 1, :] = exps[kq] * inv_den
        rank_ref[kq:kq + 1, :] = jnp.sum(jnp.where(sel, cex, 0.0), axis=0, keepdims=True).astype(jnp.int32)
    carry_ref[...] = carry_ref[...] + jnp.sum(member, axis=1, keepdims=True)
    cnt_ref[...] = carry_ref[...].astype(jnp.int32)


def _outproj_router(yr, yg, x, wo, lnw, lnb, rw_t, rb):
    t = x.shape[0]
    tm = min(512, t)
    row = lambda i: (i, 0)
    col = lambda i: (0, i)
    const = lambda i: (0, 0)
    return pl.pallas_call(
        _outproj_router_kernel,
        grid=(t // tm,),
        in_specs=[pl.BlockSpec((tm, D_RWKV), row), pl.BlockSpec((tm, D_GLA), row),
                  pl.BlockSpec((tm, D_MODEL), row), pl.BlockSpec((D_MODEL, D_MODEL), const),
                  pl.BlockSpec((1, D_MODEL), const), pl.BlockSpec((1, D_MODEL), const),
                  pl.BlockSpec((N_EXPERTS, D_MODEL), const), pl.BlockSpec((N_EXPERTS, 1), const)],
        out_specs=[pl.BlockSpec((tm * ROW_R, LANE), row),
                   pl.BlockSpec((TOP_K, tm), col), pl.BlockSpec((TOP_K, tm), col),
                   pl.BlockSpec((TOP_K, tm), col), pl.BlockSpec((N_EXPERTS, LANE), const)],
        out_shape=[jax.ShapeDtypeStruct((t * ROW_R, LANE), F32),
                   jax.ShapeDtypeStruct((TOP_K, t), jnp.int32), jax.ShapeDtypeStruct((TOP_K, t), F32),
                   jax.ShapeDtypeStruct((TOP_K, t), jnp.int32),
                   jax.ShapeDtypeStruct((N_EXPERTS, LANE), jnp.int32)],
        scratch_shapes=[pltpu.VMEM((N_EXPERTS, LANE), F32)],
        compiler_params=_cparams(1),
        name="outproj_router",
    )(yr, yg, x, wo, lnw, lnb, rw_t, rb)


def _dispatch_kernel(dest_ref, gaps_ref, x_ref, xs_ref, zero_ref, sem, zsem):
    te = x_ref.shape[0] // ROW_R
    base = pl.program_id(0) * (TOP_K * te)

    @pl.when(pl.program_id(0) == 0)
    def _():
        zero_ref[...] = jnp.zeros_like(zero_ref)

        def zero_copy(d):
            return pltpu.make_async_copy(zero_ref, _row_tile(xs_ref, d), zsem)

        for g in range(N_EXPERTS + 1):
            lo = gaps_ref[2 * g]
            hi = gaps_ref[2 * g + 1]
            lax.fori_loop(lo, hi, lambda d, c: (zero_copy(d).start(), c)[1], 0)
        for g in range(N_EXPERTS + 1):
            lo = gaps_ref[2 * g]
            hi = gaps_ref[2 * g + 1]
            lax.fori_loop(lo, hi, lambda d, c: (zero_copy(d).wait(), c)[1], 0)

    def issue(r, carry):
        for kq in range(TOP_K):
            d = dest_ref[base + kq * te + r]
            pltpu.make_async_copy(_row_tile(x_ref, r), _row_tile(xs_ref, d), sem).start(priority=kq % 2)
        return carry

    lax.fori_loop(0, te, issue, 0, unroll=8)
    for kq in range(TOP_K):
        pltpu.make_async_copy(x_ref, x_ref, sem).wait()


def _dispatch(x1, dest_tiles, gaps, m_pad, te):
    t = x1.shape[0] // ROW_R
    return pl.pallas_call(
        _dispatch_kernel,
        grid_spec=pltpu.PrefetchScalarGridSpec(
            num_scalar_prefetch=2,
            grid=(t // te,),
            in_specs=[pl.BlockSpec((te * ROW_R, LANE), lambda i, d, g: (i, 0))],
            out_specs=pl.BlockSpec(memory_space=pl.ANY),
            scratch_shapes=[pltpu.VMEM((ROW_R, LANE), F32), pltpu.SemaphoreType.DMA(()),
                            pltpu.SemaphoreType.DMA(())],
        ),
        out_shape=jax.ShapeDtypeStruct((m_pad * ROW_R, LANE), F32),
        compiler_params=_cparams(1),
        name="moe_dispatch",
    )(dest_tiles, gaps, x1)


def _expert_kernel(be_ref, nv_ref, xs_ref, w1_ref, b1_ref, w2_ref, b2_ref, ys_ref, act_ref, w1b_ref, w2b_ref):
    j = pl.program_id(0)
    n_valid = nv_ref[j]

    @pl.when((j == 0) | (be_ref[j] != be_ref[jnp.maximum(j - 1, 0)]))
    def _():
        rows = 128
        for c in range(D_MODEL // rows):
            rs = slice(c * rows, (c + 1) * rows)
            w1b_ref[rs, :] = w1_ref[0, 0, rs, :].astype(BF16)
            w2b_ref[rs, :] = w2_ref[0, 0, rs, :].astype(BF16)

    @pl.when(n_valid == 0)
    def _():
        ys_ref[...] = jnp.zeros_like(ys_ref)

    @pl.when(n_valid > 0)
    def _():
        xb = _rows_load(xs_ref, MOE_BLOCK).astype(BF16)
        n_chunk = 256
        for c in range(D_FF // n_chunk):
            cs = slice(c * n_chunk, (c + 1) * n_chunk)
            us = slice(D_FF + c * n_chunk, D_FF + (c + 1) * n_chunk)
            gt = _dot(xb, w1b_ref[:, cs]) + b1_ref[0, 0, :, cs]
            up = _dot(xb, w1b_ref[:, us]) + b1_ref[0, 0, :, us]
            gt = jnp.minimum(gt, SWIGLU_LIMIT)
            up = jnp.clip(up, -SWIGLU_LIMIT, SWIGLU_LIMIT)
            act_ref[:, cs] = ((up + 1.0) * gt * _sigmoid(SWIGLU_ALPHA * gt)).astype(BF16)
        _rows_store(ys_ref, _dot(act_ref[...], w2b_ref[...]) + b2_ref[0, 0])


def _experts(xs, block_e, n_valid, layer, w1, b1, w2, b2):
    m_pad = xs.shape[0] // ROW_R
    n_blocks = m_pad // MOE_BLOCK
    wmap = lambda j, be, nv: (layer, be[j], 0, 0)
    return pl.pallas_call(
        _expert_kernel,
        grid_spec=pltpu.PrefetchScalarGridSpec(
            num_scalar_prefetch=2,
            grid=(n_blocks,),
            in_specs=[pl.BlockSpec((MOE_BLOCK * ROW_R, LANE), lambda j, be, nv: (j, 0)),
                      pl.BlockSpec((1, 1, D_MODEL, 2 * D_FF), wmap),
                      pl.BlockSpec((1, 1, 1, 2 * D_FF), wmap),
                      pl.BlockSpec((1, 1, D_FF, D_MODEL), wmap),
                      pl.BlockSpec((1, 1, 1, D_MODEL), wmap)],
            out_specs=pl.BlockSpec((MOE_BLOCK * ROW_R, LANE), lambda j, be, nv: (j, 0)),
            scratch_shapes=[pltpu.VMEM((MOE_BLOCK, D_FF), BF16),
                            pltpu.VMEM((D_MODEL, 2 * D_FF), BF16), pltpu.VMEM((D_FF, D_MODEL), BF16)],
        ),
        out_shape=jax.ShapeDtypeStruct((m_pad * ROW_R, LANE), F32),
        compiler_params=_cparams(1),
        name="moe_experts",
    )(block_e, n_valid, xs, w1, b1, w2, b2)


def _combine_kernel(with_inproj, dest_ref, ys_ref, x1_ref, gate_ref, lnw_ref, lnb_ref, *refs):
    if with_inproj:
        wr_ref, wg_ref, out_ref, pr_ref, pg_ref, buf_ref, sem = refs
    else:
        out_ref, buf_ref, sem = refs
    te = x1_ref.shape[0] // ROW_R
    i = pl.program_id(0)
    n_tiles = pl.num_programs(0)

    def gather(tile, slot):
        base = tile * (TOP_K * te)

        def issue(r, carry):
            for kq in range(TOP_K):
                d = dest_ref[base + kq * te + r]
                pltpu.make_async_copy(_row_tile(ys_ref, d), _row_tile(buf_ref.at[slot, kq], r),
                                      sem.at[slot]).start(priority=kq % 2)
            return carry

        lax.fori_loop(0, te, issue, 0, unroll=8)

    @pl.when(i == 0)
    def _():
        gather(0, 0)

    for slot in range(2):
        @pl.when((i + 1 < n_tiles) & ((i + 1) % 2 == slot))
        def _():
            gather(i + 1, slot)

    for slot in range(2):
        @pl.when(i % 2 == slot)
        def _():
            for kq in range(TOP_K):
                pltpu.make_async_copy(buf_ref.at[slot, kq], buf_ref.at[slot, kq], sem.at[slot]).wait()
            gates = gate_ref[...]
            z = DEEPNORM_ALPHA * _rows_load(x1_ref, te)
            for kq in range(TOP_K):
                z = z + _rows_load(buf_ref.at[slot, kq], te) * gates[:, kq:kq + 1]
            xn = _layer_norm(z, lnw_ref[...], lnb_ref[...])
            out_ref[...] = xn
            if with_inproj:
                xb = xn.astype(BF16)
                pr_ref[...] = _dot(xb, wr_ref[...]).astype(BF16)
                pg_ref[...] = _dot(xb, wg_ref[...]).astype(BF16)


def _combine(ys, dest_tiles, x1, gates_t, lnw, lnb, te, next_inproj=None):
    t = x1.shape[0] // ROW_R
    row = lambda i, d: (i, 0)
    const = lambda i, d: (0, 0)
    in_specs = [pl.BlockSpec(memory_space=pl.ANY),
                pl.BlockSpec((te * ROW_R, LANE), row),
                pl.BlockSpec((te, TOP_K), row),
                pl.BlockSpec((1, D_MODEL), const),
                pl.BlockSpec((1, D_MODEL), const)]
    out_specs = [pl.BlockSpec((te, D_MODEL), row)]
    out_shape = [jax.ShapeDtypeStruct((t, D_MODEL), F32)]
    args = [dest_tiles, ys, x1, gates_t, lnw, lnb]
    if next_inproj is not None:
        in_specs += [pl.BlockSpec((D_MODEL, RW_WIDTH), const), pl.BlockSpec((D_MODEL, GL_WIDTH), const)]
        out_specs += [pl.BlockSpec((te, RW_WIDTH), row), pl.BlockSpec((te, GL_WIDTH), row)]
        out_shape += [jax.ShapeDtypeStruct((t, RW_WIDTH), BF16), jax.ShapeDtypeStruct((t, GL_WIDTH), BF16)]
        args += list(next_inproj)
    return pl.pallas_call(
        functools.partial(_combine_kernel, next_inproj is not None),
        grid_spec=pltpu.PrefetchScalarGridSpec(
            num_scalar_prefetch=1,
            grid=(t // te,),
            in_specs=in_specs,
            out_specs=out_specs,
            scratch_shapes=[pltpu.VMEM((2, TOP_K, te * ROW_R, LANE), F32), pltpu.SemaphoreType.DMA((2,))],
        ),
        out_shape=out_shape,
        compiler_params=_cparams(1),
        name="moe_combine_inproj" if next_inproj is not None else "moe_combine",
    )(*args)


def _moe(x1, idx, gate, rank, counts, layer, w1, b1, w2, b2, lnw, lnb, next_inproj=None):
    t = x1.shape[0] // ROW_R
    te = min(256, t)
    n_blocks = -(-(t * TOP_K) // MOE_BLOCK) + N_EXPERTS
    m_pad = n_blocks * MOE_BLOCK
    padded = ((counts + MOE_BLOCK - 1) // MOE_BLOCK) * MOE_BLOCK
    pend = jnp.cumsum(padded)
    pstart = pend - padded
    dest = rank
    for e in range(N_EXPERTS):
        dest = dest + jnp.where(idx == e, pstart[e], 0)
    dest_tiles = dest.reshape(TOP_K, t // te, te).transpose(1, 0, 2).reshape(-1)
    blk_start = jnp.arange(n_blocks, dtype=jnp.int32) * MOE_BLOCK
    block_e = jnp.minimum(jnp.sum(pend[None, :] <= blk_start[:, None], axis=1), N_EXPERTS - 1).astype(jnp.int32)
    is_e = block_e[:, None] == jnp.arange(N_EXPERTS, dtype=jnp.int32)[None, :]
    blk_end = jnp.sum(jnp.where(is_e, (pstart + counts)[None, :], 0), axis=1)
    n_valid = jnp.clip(blk_end - blk_start, 0, MOE_BLOCK).astype(jnp.int32)

    gap_lo = jnp.concatenate([pstart + counts, pend[-1:]])
    gap_hi = jnp.concatenate([pend, jnp.full((1,), m_pad, pend.dtype)])
    gaps = jnp.stack([gap_lo, gap_hi], axis=1).reshape(-1).astype(jnp.int32)

    xs = _dispatch(x1, dest_tiles, gaps, m_pad, te)
    ys = _experts(xs, block_e, n_valid, layer, w1, b1, w2, b2)
    return _combine(ys, dest_tiles, x1, gate.T, lnw, lnb, te, next_inproj)


def _pad_cols(w, width):
    return jnp.pad(w, ((0, 0), (0, width - w.shape[1])))


def _pad_rows(w, height):
    return jnp.pad(w, ((0, height - w.shape[0]), (0, 0)))


def _pack_inproj(w_in_l, mu_l):
    d3 = 3 * D_RWKV
    o_ad = d3 + W_LORA
    o_gd = o_ad + A_LORA
    o_gla = o_gd + G_LORA
    wr = jnp.concatenate([w_in_l[:, :d3],
                          _pad_cols(w_in_l[:, d3:o_ad], LANE),
                          _pad_cols(w_in_l[:, o_ad:o_gd], LANE),
                          _pad_cols(w_in_l[:, o_gd:o_gla], 2 * LANE)], axis=1)
    mu = mu_l[None, :]
    mu_p = jnp.concatenate([mu[:, :d3], _pad_cols(mu[:, d3:o_ad], LANE), _pad_cols(mu[:, o_ad:o_gd], LANE),
                            _pad_cols(mu[:, o_gd:o_gla], 2 * LANE)], axis=1)
    g = w_in_l[:, o_gla:]
    wg = jnp.concatenate([g[:, :GL_GK], _pad_cols(g[:, GL_GK:GL_GK + GLA_GATE_RANK], LANE),
                          g[:, GL_GK + GLA_GATE_RANK:]], axis=1)
    return wr.astype(BF16), wg.astype(BF16), mu_p


def kernel(x, ln_in_w, ln_in_b, w_in, rwkv_mu, rwkv_w0, rwkv_w_up, rwkv_a0, rwkv_a_up, rwkv_g_up, rwkv_k_k, rwkv_k_a, rwkv_r_k, rwkv_gn_w, rwkv_gn_b, rwkv_v0, rwkv_v_down, rwkv_v_up, gla_gk_up, gla_gk_b, gla_norm_w, w_out, ln1_w, ln1_b, router_w, router_b, exp_w1, exp_b1, exp_w2, exp_b2, ln2_w, ln2_b):
    bsz, seq, d = x.shape
    t = bsz * seq
    xf = x.reshape(t, d)
    vfirst = None
    packed = [_pack_inproj(w_in[l], rwkv_mu[l]) for l in range(DEPTH)]
    xf, p_r, p_g = _inproj(xf, ln_in_w[None, :], ln_in_b[None, :], packed[0][0], packed[0][1])
    for l in range(DEPTH):
        mu_p = packed[l][2]
        rprm = {
            "mu": mu_p, "w0": rwkv_w0[l][None, :], "w_up": _pad_rows(rwkv_w_up[l], LANE).astype(BF16),
            "a0": rwkv_a0[l][None, :], "a_up": _pad_rows(rwkv_a_up[l], LANE).astype(BF16),
            "g_up": _pad_rows(rwkv_g_up[l], 2 * LANE).astype(BF16),
            "k_k": rwkv_k_k[l][None, :], "k_a": rwkv_k_a[l][None, :], "r_k": rwkv_r_k[l][None, :],
            "gn_w": rwkv_gn_w[l][None, :], "gn_b": rwkv_gn_b[l][None, :],
        }
        if l > 0:
            rprm["v0"] = rwkv_v0[l - 1][None, :]
            rprm["v_down"] = _pad_cols(rwkv_v_down[l - 1], LANE).astype(BF16)
            rprm["v_up"] = _pad_rows(rwkv_v_up[l - 1], LANE).astype(BF16)
        y_r, vfirst = _rwkv_mixer(p_r.reshape(bsz, seq, RW_WIDTH), vfirst, rprm, has_vres=l > 0)
        gprm = {"gk_up": _pad_rows(gla_gk_up[l], LANE).astype(BF16), "gk_b": gla_gk_b[l][None, :],
                "norm_w": jnp.tile(gla_norm_w[l], GLA_HEADS)[None, :]}
        y_g = _gla_mixer(p_g.reshape(bsz, seq, GL_WIDTH), gprm)
        x1, idx, gate, rank, cnt = _outproj_router(
            y_r.reshape(t, D_RWKV), y_g.reshape(t, D_GLA), xf, w_out[l].astype(BF16),
            ln1_w[l][None, :], ln1_b[l][None, :], router_w[l].T, router_b[l][:, None])
        nxt = packed[l + 1][:2] if l + 1 < DEPTH else None
        res = _moe(x1, idx, gate, rank, cnt[:, 0], l, exp_w1, exp_b1[:, :, None, :],
                   exp_w2, exp_b2[:, :, None, :], ln2_w[l][None, :], ln2_b[l][None, :], nxt)
        if nxt is not None:
            xf, p_r, p_g = res
        else:
            xf = res[0]
    return xf.reshape(bsz, seq, d)
```

```python
import functools

import jax
import jax.numpy as jnp
from jax import lax
from jax.experimental import pallas as pl
from jax.experimental.pallas import tpu as pltpu

F32 = jnp.float32
BF16 = jnp.bfloat16
HIGHEST = lax.Precision.HIGHEST

D_MODEL = 1024
DEPTH = 2
CHUNK = 64
RWKV_HEAD = 64
D_RWKV = 512
RWKV_HEADS = 8
W_LORA = 64
A_LORA = 64
V_LORA = 32
G_LORA = 160
RWKV_GN_EPS = 64e-5
D_GLA = 512
GLA_HEADS = 4
GLA_DV = 128
GLA_DK = 64
GLA_GATE_RANK = 16
GLA_GATE_NORMALIZER = 16.0
RMS_EPS = 1e-6
N_EXPERTS = 32
TOP_K = 4
D_FF = 1024
SWIGLU_LIMIT = 7.0
SWIGLU_ALPHA = 1.702
MOE_BLOCK = 512
DEEPNORM_ALPHA = (2 * DEPTH) ** 0.25
LN_EPS = 1e-5

LANE = 128
SUBLANE = 8
ROW_R = D_MODEL // LANE
assert ROW_R == SUBLANE
RW_WD = 3 * D_RWKV
RW_AD = RW_WD + LANE
RW_GD = RW_AD + LANE
RW_WIDTH = RW_GD + 2 * LANE
GL_K = GLA_HEADS * GLA_DK
GL_V = 2 * GL_K
GL_GK = GL_V + D_GLA
GL_G = GL_GK + LANE
GL_WIDTH = GL_G + D_GLA

VMEM_LIMIT = 56 * 1024 * 1024


def _cparams(n_axes):
    return pltpu.CompilerParams(dimension_semantics=("arbitrary",) * n_axes,
                                vmem_limit_bytes=VMEM_LIMIT)


def _dot(a, b):
    return jnp.dot(a, b, preferred_element_type=F32)


def _dot_nt(a, b):
    return lax.dot_general(a, b, (((1,), (1,)), ((), ())), preferred_element_type=F32)


def _dot_tn(a, b):
    return lax.dot_general(a, b, (((0,), (0,)), ((), ())), preferred_element_type=F32)


def _split_bf16(x):
    hi = x.astype(BF16)
    lo = (x - hi.astype(F32)).astype(BF16)
    return hi, lo


def _layer_norm(z, w, b):
    mu = jnp.mean(z, axis=-1, keepdims=True)
    zc = z - mu
    var = jnp.mean(zc * zc, axis=-1, keepdims=True)
    return zc * lax.rsqrt(var + LN_EPS) * w + b


def _sigmoid(x):
    return 1.0 / (1.0 + jnp.exp(-x))


def _softplus(x):
    return jnp.maximum(x, 0.0) + jnp.log(1.0 + jnp.exp(-jnp.abs(x)))


def _rows_load(ref, n):
    return jnp.concatenate([ref[pl.ds(s, n, stride=ROW_R), :] for s in range(ROW_R)], axis=1)


def _rows_store(ref, val):
    for s in range(ROW_R):
        ref[pl.ds(s, val.shape[0], stride=ROW_R), :] = val[:, s * LANE:(s + 1) * LANE]


def _row_tile(ref, row):
    return ref.at[pl.ds(pl.multiple_of(row * ROW_R, ROW_R), ROW_R), :]


def _tril_mask(n, strict):
    r = lax.broadcasted_iota(jnp.int32, (n, n), 0)
    c = lax.broadcasted_iota(jnp.int32, (n, n), 1)
    return (c < r) if strict else (c <= r)


def _inproj_kernel(x_ref, lnw_ref, lnb_ref, wr_ref, wg_ref, x0_ref, pr_ref, pg_ref):
    x = _layer_norm(x_ref[...], lnw_ref[...], lnb_ref[...])
    x0_ref[...] = x
    xb = x.astype(BF16)
    pr_ref[...] = _dot(xb, wr_ref[...]).astype(BF16)
    pg_ref[...] = _dot(xb, wg_ref[...]).astype(BF16)


def _inproj(x, lnw, lnb, wr, wg):
    t = x.shape[0]
    tm = min(512, t)
    row = lambda i: (i, 0)
    const = lambda i: (0, 0)
    return pl.pallas_call(
        _inproj_kernel,
        grid=(t // tm,),
        in_specs=[pl.BlockSpec((tm, D_MODEL), row),
                  pl.BlockSpec((1, D_MODEL), const), pl.BlockSpec((1, D_MODEL), const),
                  pl.BlockSpec((D_MODEL, RW_WIDTH), const), pl.BlockSpec((D_MODEL, GL_WIDTH), const)],
        out_specs=[pl.BlockSpec((tm, D_MODEL), row), pl.BlockSpec((tm, RW_WIDTH), row),
                   pl.BlockSpec((tm, GL_WIDTH), row)],
        out_shape=[jax.ShapeDtypeStruct((t, D_MODEL), F32), jax.ShapeDtypeStruct((t, RW_WIDTH), BF16),
                   jax.ShapeDtypeStruct((t, GL_WIDTH), BF16)],
        compiler_params=_cparams(1),
        name="inproj_ln",
    )(x, lnw, lnb, wr, wg)


def _head_sum(x, ones_blk):
    hi, lo = _split_bf16(x)
    return _dot(hi, ones_blk) + _dot(lo, ones_blk)


def _rwkv_kernel(has_vres, *refs):
    if has_vres:
        (p_ref, vfirst_ref, mu_ref, w0_ref, wup_ref, a0_ref, aup_ref, gup_ref, kk_ref, ka_ref, rk_ref,
         gnw_ref, gnb_ref, v0_ref, vdown_ref, vup_ref, y_ref, state_ref, prev_ref) = refs
    else:
        (p_ref, mu_ref, w0_ref, wup_ref, a0_ref, aup_ref, gup_ref, kk_ref, ka_ref, rk_ref,
         gnw_ref, gnb_ref, y_ref, vfirst_out_ref, state_ref, prev_ref) = refs

    @pl.when(pl.program_id(1) == 0)
    def _():
        state_ref[...] = jnp.zeros_like(state_ref)
        prev_ref[...] = jnp.zeros_like(prev_ref)

    p = p_ref[0].astype(F32)
    n_tok = p.shape[0]
    row = lax.broadcasted_iota(jnp.int32, (n_tok, 1), 0)
    prev = jnp.where(row == 0, prev_ref[...], pltpu.roll(p, 1, axis=0))
    prev_ref[...] = p[n_tok - 1:n_tok, :]
    xs = p + (prev - p) * mu_ref[...]
    r = xs[:, 0:D_RWKV]
    k = xs[:, D_RWKV:2 * D_RWKV]
    v = xs[:, 2 * D_RWKV:3 * D_RWKV]
    wd = xs[:, RW_WD:RW_AD]
    ad = xs[:, RW_AD:RW_GD]
    gd = xs[:, RW_GD:RW_WIDTH]

    w = w0_ref[...] + _dot(jnp.tanh(wd).astype(BF16), wup_ref[...])
    w = -_softplus(-w) - 0.5
    logw = -jnp.exp(w)
    a = _sigmoid(a0_ref[...] + _dot(ad.astype(BF16), aup_ref[...]))
    g = _dot(_sigmoid(gd).astype(BF16), gup_ref[...])
    if has_vres:
        vmix = _dot(_dot(v.astype(BF16), vdown_ref[...]).astype(BF16), vup_ref[...])
        v = v + (vfirst_ref[0] - v) * _sigmoid(v0_ref[...] + vmix)
    else:
        vfirst_out_ref[0] = v

    hr = lax.broadcasted_iota(jnp.int32, (D_RWKV, D_RWKV), 0) // RWKV_HEAD
    hc = lax.broadcasted_iota(jnp.int32, (D_RWKV, D_RWKV), 1) // RWKV_HEAD
    ones_blk = (hr == hc).astype(BF16)

    kk = k * kk_ref[...]
    kk = kk * lax.rsqrt(jnp.maximum(_head_sum(kk * kk, ones_blk), 1e-24))
    k = k * (1.0 + (a - 1.0) * ka_ref[...])
    bonus = _head_sum(r * k * rk_ref[...], ones_blk)

    n_chunks = n_tok // CHUNK
    c2 = 2 * CHUNK
    r_i = lax.broadcasted_iota(jnp.int32, (CHUNK, c2), 0)
    c_i = lax.broadcasted_iota(jnp.int32, (CHUNK, c2), 1)
    c_half = jnp.where(c_i >= CHUNK, c_i - CHUNK, c_i)
    m_strict_l = ((c_i < CHUNK) & (c_i < r_i)).astype(F32)
    m_strict_r = ((c_i >= CHUNK) & (c_half < r_i)).astype(F32)
    m_lower2 = (c_half <= r_i).astype(F32)
    rr = lax.broadcasted_iota(jnp.int32, (CHUNK, CHUNK), 0)
    cc = lax.broadcasted_iota(jnp.int32, (CHUNK, CHUNK), 1)
    tril_f = (cc <= rr).astype(F32)
    eye = (rr == cc).astype(F32)
    m_diag2 = ((rr // 2) == (cc // 2)).astype(F32)
    lvl_masks = []
    size = 2
    while size < CHUNK:
        lvl_masks.append((((rr // (2 * size)) == (cc // (2 * size))) & ((rr // size) != (cc // size))).astype(F32))
        size *= 2

    items = []
    g_ends = []
    for c in range(n_chunks):
        sl = slice(c * CHUNK, (c + 1) * CHUNK)
        lw = logw[sl]
        b = jnp.dot(tril_f, lw, precision=HIGHEST, preferred_element_type=F32)
        eb = jnp.exp(b)
        enb = jnp.exp(-b)
        ebx = jnp.exp(b - lw)
        rh = (r[sl] * eb).astype(BF16)
        kh = (k[sl] * enb).astype(BF16)
        ah = (-kk[sl] * ebx).astype(BF16)
        bh = (kk[sl] * a[sl] * enb).astype(BF16)
        vb = v[sl].astype(BF16)
        g_ends.append(eb[CHUNK - 1:CHUNK, :])
        for h in range(RWKV_HEADS):
            hs = slice(h * RWKV_HEAD, (h + 1) * RWKV_HEAD)
            items.append(dict(ah=ah[:, hs], rh=rh[:, hs], v=vb[:, hs],
                              bk=jnp.concatenate([bh[:, hs], kh[:, hs]], axis=0)))

    for it in items:
        it["s2"] = _dot_nt(jnp.concatenate([it["ah"], it["rh"]], axis=0), it["bk"])
    for it in items:
        top = it["s2"][:CHUNK]
        it["a_ab"] = (top * m_strict_l)[:, :CHUNK]
        it["ak"] = (top * m_strict_r).astype(BF16)
        it["rbk"] = (it["s2"][CHUNK:] * m_lower2).astype(BF16)
        it["d"] = eye + it["a_ab"] * m_diag2
        del it["s2"]
    for mk in lvl_masks:
        for it in items:
            it["db"] = it["d"].astype(BF16)
            it["m"] = _dot((it["a_ab"] * mk).astype(BF16), it["db"])
        for it in items:
            it["d"] = it["d"] + _dot(it["db"], it["m"].astype(BF16))
    for it in items:
        it["akv"] = _dot(it["ak"], jnp.concatenate([it["v"], it["v"]], axis=0))
        it["tinv"] = it["d"].astype(BF16)
    for it in items:
        it["w"] = _dot(it["tinv"], it["ah"])
        it["ut"] = _dot(it["tinv"], it["akv"].astype(BF16))

    states = [state_ref[h] for h in range(RWKV_HEADS)]
    y_chunks = []
    for c in range(n_chunks):
        its = items[c * RWKV_HEADS:(c + 1) * RWKV_HEADS]
        for h, it in enumerate(its):
            it["ws"] = _dot_nt(jnp.concatenate([it["w"].astype(BF16), it["rh"]], axis=0), states[h].astype(BF16))
        for h, it in enumerate(its):
            it["uv"] = jnp.concatenate([(it["ws"][:CHUNK] + it["ut"]).astype(BF16), it["v"]], axis=0)
            hs = slice(h * RWKV_HEAD, (h + 1) * RWKV_HEAD)
            states[h] = (states[h] + _dot_tn(it["uv"], it["bk"])) * g_ends[c][:, hs]
        y_heads = [it["ws"][CHUNK:] + _dot(it["rbk"], it["uv"]) for it in its]
        y_chunks.append(jnp.concatenate(y_heads, axis=1))
    for h in range(RWKV_HEADS):
        state_ref[h] = states[h]
    y = jnp.concatenate(y_chunks, axis=0) if len(y_chunks) > 1 else y_chunks[0]

    inv_n = 1.0 / RWKV_HEAD
    m = _head_sum(y, ones_blk) * inv_n
    yc = y - m
    var = _head_sum(yc * yc, ones_blk) * inv_n
    y = yc * lax.rsqrt(var + RWKV_GN_EPS) * gnw_ref[...] + gnb_ref[...]
    y = y + bonus * v
    y_ref[0] = (y * g).astype(BF16)


def _rwkv_mixer(p_r, vfirst, prm, has_vres):
    bsz, seq, _ = p_r.shape
    n_tok = min(256, seq)
    tok = lambda b, i: (b, i, 0)
    const = lambda b, i: (0, 0)
    vec = pl.BlockSpec((1, D_RWKV), const)
    in_specs = [pl.BlockSpec((1, n_tok, RW_WIDTH), tok)]
    args = [p_r]
    if has_vres:
        in_specs.append(pl.BlockSpec((1, n_tok, D_RWKV), tok))
        args.append(vfirst)
    in_specs += [pl.BlockSpec((1, RW_WIDTH), const), vec, pl.BlockSpec((LANE, D_RWKV), const), vec,
                 pl.BlockSpec((LANE, D_RWKV), const), pl.BlockSpec((2 * LANE, D_RWKV), const),
                 vec, vec, vec, vec, vec]
    args += [prm["mu"], prm["w0"], prm["w_up"], prm["a0"], prm["a_up"], prm["g_up"],
             prm["k_k"], prm["k_a"], prm["r_k"], prm["gn_w"], prm["gn_b"]]
    out_shape = [jax.ShapeDtypeStruct((bsz, seq, D_RWKV), BF16)]
    out_specs = [pl.BlockSpec((1, n_tok, D_RWKV), tok)]
    if has_vres:
        in_specs += [vec, pl.BlockSpec((D_RWKV, LANE), const), pl.BlockSpec((LANE, D_RWKV), const)]
        args += [prm["v0"], prm["v_down"], prm["v_up"]]
    else:
        out_shape.append(jax.ShapeDtypeStruct((bsz, seq, D_RWKV), F32))
        out_specs.append(pl.BlockSpec((1, n_tok, D_RWKV), tok))
    res = pl.pallas_call(
        functools.partial(_rwkv_kernel, has_vres),
        grid=(bsz, seq // n_tok),
        in_specs=in_specs,
        out_specs=out_specs,
        out_shape=out_shape,
        scratch_shapes=[pltpu.VMEM((RWKV_HEADS, RWKV_HEAD, RWKV_HEAD), F32),
                        pltpu.VMEM((1, RW_WIDTH), F32)],
        compiler_params=_cparams(2),
        name="rwkv7_vres" if has_vres else "rwkv7",
    )(*args)
    if has_vres:
        return res[0], vfirst
    return res[0], res[1]


def _gla_kernel(p_ref, gkup_ref, gkb_ref, nw_ref, y_ref, state_ref):
    @pl.when(pl.program_id(1) == 0)
    def _():
        state_ref[...] = jnp.zeros_like(state_ref)

    p = p_ref[0].astype(F32)
    n_tok = p.shape[0]
    q = p[:, 0:GL_K]
    k = p[:, GL_K:GL_V]
    v = p[:, GL_V:GL_GK]
    gkd = p[:, GL_GK:GL_G]
    g = p[:, GL_G:GL_WIDTH]
    z = _dot(gkd.astype(BF16), gkup_ref[...]) + gkb_ref[...]
    gk = -_softplus(-z) * (1.0 / GLA_GATE_NORMALIZER)

    lower = _tril_mask(CHUNK, strict=False)
    tril_f = lower.astype(F32)
    n_chunks = n_tok // CHUNK
    items = []
    decs = []
    for c in range(n_chunks):
        sl = slice(c * CHUNK, (c + 1) * CHUNK)
        b = jnp.dot(tril_f, gk[sl], precision=HIGHEST, preferred_element_type=F32)
        b_last = b[CHUNK - 1:CHUNK, :]
        q_e = (q[sl] * jnp.exp(b) * (GLA_DK ** -0.5)).astype(BF16)
        k_e = (k[sl] * jnp.exp(-b)).astype(BF16)
        k_end = (k[sl] * jnp.exp(b_last - b)).astype(BF16)
        decs.append(jnp.exp(b_last))
        vb = v[sl].astype(BF16)
        for h in range(GLA_HEADS):
            ks = slice(h * GLA_DK, (h + 1) * GLA_DK)
            vs = slice(h * GLA_DV, (h + 1) * GLA_DV)
            items.append(dict(q=q_e[:, ks], k=k_e[:, ks], kend=k_end[:, ks], v=vb[:, vs]))
    for it in items:
        it["sc"] = jnp.where(lower, _dot_nt(it["q"], it["k"]), 0.0).astype(BF16)
    for it in items:
        it["intra"] = _dot(it["sc"], it["v"])
        it["kv"] = _dot_tn(it["v"], it["kend"])
    states = [state_ref[h] for h in range(GLA_HEADS)]
    o_chunks = []
    for c in range(n_chunks):
        o_heads = []
        for h in range(GLA_HEADS):
            it = items[c * GLA_HEADS + h]
            ks = slice(h * GLA_DK, (h + 1) * GLA_DK)
            o_h = it["intra"] + _dot_nt(it["q"], states[h].astype(BF16))
            states[h] = states[h] * decs[c][:, ks] + it["kv"]
            o_heads.append(o_h * lax.rsqrt(jnp.mean(o_h * o_h, axis=-1, keepdims=True) + RMS_EPS))
        o_chunks.append(jnp.concatenate(o_heads, axis=1))
    for h in range(GLA_HEADS):
        state_ref[h] = states[h]
    o = jnp.concatenate(o_chunks, axis=0) if len(o_chunks) > 1 else o_chunks[0]
    silu_g = g * _sigmoid(g)
    y_ref[0] = (o * nw_ref[...] * silu_g).astype(BF16)


def _gla_mixer(p_g, prm):
    bsz, seq, _ = p_g.shape
    n_tok = min(256, seq)
    tok = lambda b, i: (b, i, 0)
    const = lambda b, i: (0, 0)
    return pl.pallas_call(
        _gla_kernel,
        grid=(bsz, seq // n_tok),
        in_specs=[pl.BlockSpec((1, n_tok, GL_WIDTH), tok),
                  pl.BlockSpec((LANE, GL_K), const), pl.BlockSpec((1, GL_K), const),
                  pl.BlockSpec((1, D_GLA), const)],
        out_specs=pl.BlockSpec((1, n_tok, D_GLA), tok),
        out_shape=jax.ShapeDtypeStruct((bsz, seq, D_GLA), BF16),
        scratch_shapes=[pltpu.VMEM((GLA_HEADS, GLA_DV, GLA_DK), F32)],
        compiler_params=_cparams(2),
        name="gla",
    )(p_g, prm["gk_up"], prm["gk_b"], prm["norm_w"])


def _outproj_router_kernel(yr_ref, yg_ref, x_ref, wo_ref, lnw_ref, lnb_ref, rw_ref, rb_ref,
                           x1_ref, idx_ref, gate_ref, rank_ref, cnt_ref, carry_ref):
    @pl.when(pl.program_id(0) == 0)
    def _():
        carry_ref[...] = jnp.zeros_like(carry_ref)

    mix = _dot(yr_ref[...], wo_ref[0:D_RWKV, :]) + _dot(yg_ref[...], wo_ref[D_RWKV:, :])
    x1 = _layer_norm(DEEPNORM_ALPHA * x_ref[...] + mix, lnw_ref[...], lnb_ref[...])
    _rows_store(x1_ref, x1)
    tm = x1.shape[0]

    xh, xl = _split_bf16(x1)
    rw = rw_ref[...]
    wh, wl = _split_bf16(rw)
    logits = _dot_nt(wh, xh) + _dot_nt(wh, xl) + _dot_nt(wl, xh) + rb_ref[...]

    e_iota = lax.broadcasted_iota(jnp.int32, (N_EXPERTS, tm), 0)
    work = logits
    vals, idxs = [], []
    member = jnp.zeros((N_EXPERTS, tm), F32)
    for _ in range(TOP_K):
        mx = jnp.max(work, axis=0, keepdims=True)
        ix = jnp.min(jnp.where(work == mx, e_iota, N_EXPERTS), axis=0, keepdims=True)
        sel = e_iota == ix
        work = jnp.where(sel, -jnp.inf, work)
        member = jnp.where(sel, 1.0, member)
        vals.append(mx)
        idxs.append(ix)
    exps = [jnp.exp(vv - vals[0]) for vv in vals]
    inv_den = 1.0 / (exps[0] + exps[1] + exps[2] + exps[3])

    tr = lax.broadcasted_iota(jnp.int32, (tm, tm), 0)
    tc = lax.broadcasted_iota(jnp.int32, (tm, tm), 1)
    before = (tr < tc).astype(BF16)
    cex = _dot(member.astype(BF16), before) + carry_ref[...][:, 0:1]
    for kq in range(TOP_K):
        sel = e_iota == idxs[kq]
        idx_ref[kq:kq + 1, :] = idxs[kq]
        gate_ref[kq:kq + 1, :] = exps[kq] * inv_den
        rank_ref[kq:kq + 1, :] = jnp.sum(jnp.where(sel, cex, 0.0), axis=0, keepdims=True).astype(jnp.int32)
    carry_ref[...] = carry_ref[...] + jnp.sum(member, axis=1, keepdims=True)
    cnt_ref[...] = carry_ref[...].astype(jnp.int32)


def _outproj_router(yr, yg, x, wo, lnw, lnb, rw_t, rb):
    t = x.shape[0]
    tm = min(512, t)
    row = lambda i: (i, 0)
    col = lambda i: (0, i)
    const = lambda i: (0, 0)
    return pl.pallas_call(
        _outproj_router_kernel,
        grid=(t // tm,),
        in_specs=[pl.BlockSpec((tm, D_RWKV), row), pl.BlockSpec((tm, D_GLA), row),
                  pl.BlockSpec((tm, D_MODEL), row), pl.BlockSpec((D_MODEL, D_MODEL), const),
                  pl.BlockSpec((1, D_MODEL), const), pl.BlockSpec((1, D_MODEL), const),
                  pl.BlockSpec((N_EXPERTS, D_MODEL), const), pl.BlockSpec((N_EXPERTS, 1), const)],
        out_specs=[pl.BlockSpec((tm * ROW_R, LANE), row),
                   pl.BlockSpec((TOP_K, tm), col), pl.BlockSpec((TOP_K, tm), col),
                   pl.BlockSpec((TOP_K, tm), col), pl.BlockSpec((N_EXPERTS, LANE), const)],
        out_shape=[jax.ShapeDtypeStruct((t * ROW_R, LANE), F32),
                   jax.ShapeDtypeStruct((TOP_K, t), jnp.int32), jax.ShapeDtypeStruct((TOP_K, t), F32),
                   jax.ShapeDtypeStruct((TOP_K, t), jnp.int32),
                   jax.ShapeDtypeStruct((N_EXPERTS, LANE), jnp.int32)],
        scratch_shapes=[pltpu.VMEM((N_EXPERTS, LANE), F32)],
        compiler_params=_cparams(1),
        name="outproj_router",
    )(yr, yg, x, wo, lnw, lnb, rw_t, rb)


def _dispatch_kernel(dest_ref, gaps_ref, x_ref, xs_ref, zero_ref, sem, zsem):
    te = x_ref.shape[0] // ROW_R
    base = pl.program_id(0) * (TOP_K * te)

    @pl.when(pl.program_id(0) == 0)
    def _():
        zero_ref[...] = jnp.zeros_like(zero_ref)

        def zero_copy(d):
            return pltpu.make_async_copy(zero_ref, _row_tile(xs_ref, d), zsem)

        for g in range(N_EXPERTS + 1):
            lo = gaps_ref[2 * g]
            hi = gaps_ref[2 * g + 1]
            lax.fori_loop(lo, hi, lambda d, c: (zero_copy(d).start(), c)[1], 0)
        blk = pl.ds(0, MOE_BLOCK * ROW_R)
        for g in range(N_EXPERTS):
            pltpu.make_async_copy(xs_ref.at[blk, :], xs_ref.at[blk, :], zsem).wait()

    def issue(r, carry):
        for kq in range(TOP_K):
            d = dest_ref[base + kq * te + r]
            pltpu.make_async_copy(_row_tile(x_ref, r), _row_tile(xs_ref, d), sem).start(priority=kq % 2)
        return carry

    lax.fori_loop(0, te, issue, 0, unroll=8)
    for kq in range(TOP_K):
        pltpu.make_async_copy(x_ref, x_ref, sem).wait()


def _dispatch(x1, dest_tiles, gaps, m_pad, te):
    t = x1.shape[0] // ROW_R
    return pl.pallas_call(
        _dispatch_kernel,
        grid_spec=pltpu.PrefetchScalarGridSpec(
            num_scalar_prefetch=2,
            grid=(t // te,),
            in_specs=[pl.BlockSpec((te * ROW_R, LANE), lambda i, d, g: (i, 0))],
            out_specs=pl.BlockSpec(memory_space=pl.ANY),
            scratch_shapes=[pltpu.VMEM((ROW_R, LANE), F32), pltpu.SemaphoreType.DMA(()),
                            pltpu.SemaphoreType.DMA(())],
        ),
        out_shape=jax.ShapeDtypeStruct((m_pad * ROW_R, LANE), F32),
        compiler_params=_cparams(1),
        name="moe_dispatch",
    )(dest_tiles, gaps, x1)


def _expert_kernel(be_ref, nv_ref, xs_ref, w1_ref, b1_ref, w2_ref, b2_ref, ys_ref, act_ref, w1b_ref, w2b_ref):
    j = pl.program_id(0)
    n_valid = nv_ref[j]

    @pl.when((j == 0) | (be_ref[j] != be_ref[jnp.maximum(j - 1, 0)]))
    def _():
        rows = 128
        for c in range(D_MODEL // rows):
            rs = slice(c * rows, (c + 1) * rows)
            w1b_ref[rs, :] = w1_ref[0, 0, rs, :].astype(BF16)
            w2b_ref[rs, :] = w2_ref[0, 0, rs, :].astype(BF16)

    @pl.when(n_valid == 0)
    def _():
        ys_ref[...] = jnp.zeros_like(ys_ref)

    @pl.when(n_valid > 0)
    def _():
        xb = _rows_load(xs_ref, MOE_BLOCK).astype(BF16)
        n_chunk = 256
        for c in range(D_FF // n_chunk):
            cs = slice(c * n_chunk, (c + 1) * n_chunk)
            us = slice(D_FF + c * n_chunk, D_FF + (c + 1) * n_chunk)
            gt = _dot(xb, w1b_ref[:, cs]) + b1_ref[0, 0, :, cs]
            up = _dot(xb, w1b_ref[:, us]) + b1_ref[0, 0, :, us]
            gt = jnp.minimum(gt, SWIGLU_LIMIT)
            up = jnp.clip(up, -SWIGLU_LIMIT, SWIGLU_LIMIT)
            act_ref[:, cs] = ((up + 1.0) * gt * _sigmoid(SWIGLU_ALPHA * gt)).astype(BF16)
        _rows_store(ys_ref, _dot(act_ref[...], w2b_ref[...]) + b2_ref[0, 0])


def _experts(xs, block_e, n_valid, layer, w1, b1, w2, b2):
    m_pad = xs.shape[0] // ROW_R
    n_blocks = m_pad // MOE_BLOCK
    wmap = lambda j, be, nv: (layer, be[j], 0, 0)
    return pl.pallas_call(
        _expert_kernel,
        grid_spec=pltpu.PrefetchScalarGridSpec(
            num_scalar_prefetch=2,
            grid=(n_blocks,),
            in_specs=[pl.BlockSpec((MOE_BLOCK * ROW_R, LANE), lambda j, be, nv: (j, 0)),
                      pl.BlockSpec((1, 1, D_MODEL, 2 * D_FF), wmap),
                      pl.BlockSpec((1, 1, 1, 2 * D_FF), wmap),
                      pl.BlockSpec((1, 1, D_FF, D_MODEL), wmap),
                      pl.BlockSpec((1, 1, 1, D_MODEL), wmap)],
            out_specs=pl.BlockSpec((MOE_BLOCK * ROW_R, LANE), lambda j, be, nv: (j, 0)),
            scratch_shapes=[pltpu.VMEM((MOE_BLOCK, D_FF), BF16),
                            pltpu.VMEM((D_MODEL, 2 * D_FF), BF16), pltpu.VMEM((D_FF, D_MODEL), BF16)],
        ),
        out_shape=jax.ShapeDtypeStruct((m_pad * ROW_R, LANE), F32),
        compiler_params=_cparams(1),
        name="moe_experts",
    )(block_e, n_valid, xs, w1, b1, w2, b2)


def _combine_kernel(with_inproj, dest_ref, ys_ref, x1_ref, gate_ref, lnw_ref, lnb_ref, *refs):
    if with_inproj:
        wr_ref, wg_ref, out_ref, pr_ref, pg_ref, buf_ref, sem = refs
    else:
        out_ref, buf_ref, sem = refs
    te = x1_ref.shape[0] // ROW_R
    i = pl.program_id(0)
    n_tiles = pl.num_programs(0)

    def gather(tile, slot):
        base = tile * (TOP_K * te)

        def issue(r, carry):
            for kq in range(TOP_K):
                d = dest_ref[base + kq * te + r]
                pltpu.make_async_copy(_row_tile(ys_ref, d), _row_tile(buf_ref.at[slot, kq], r),
                                      sem.at[slot]).start(priority=kq % 2)
            return carry

        lax.fori_loop(0, te, issue, 0, unroll=8)

    @pl.when(i == 0)
    def _():
        gather(0, 0)

    for slot in range(2):
        @pl.when((i + 1 < n_tiles) & ((i + 1) % 2 == slot))
        def _():
            gather(i + 1, slot)

    for slot in range(2):
        @pl.when(i % 2 == slot)
        def _():
            for kq in range(TOP_K):
                pltpu.make_async_copy(buf_ref.at[slot, kq], buf_ref.at[slot, kq], sem.at[slot]).wait()
            gates = gate_ref[...]
            z = DEEPNORM_ALPHA * _rows_load(x1_ref, te)
            for kq in range(TOP_K):
                z = z + _rows_load(buf_ref.at[slot, kq], te) * gates[:, kq:kq + 1]
            xn = _layer_norm(z, lnw_ref[...], lnb_ref[...])
            out_ref[...] = xn
            if with_inproj:
                xb = xn.astype(BF16)
                pr_ref[...] = _dot(xb, wr_ref[...]).astype(BF16)
                pg_ref[...] = _dot(xb, wg_ref[...]).astype(BF16)


def _combine(ys, dest_tiles, x1, gates_t, lnw, lnb, te, next_inproj=None):
    t = x1.shape[0] // ROW_R
    row = lambda i, d: (i, 0)
    const = lambda i, d: (0, 0)
    in_specs = [pl.BlockSpec(memory_space=pl.ANY),
                pl.BlockSpec((te * ROW_R, LANE), row),
                pl.BlockSpec((te, TOP_K), row),
                pl.BlockSpec((1, D_MODEL), const),
                pl.BlockSpec((1, D_MODEL), const)]
    out_specs = [pl.BlockSpec((te, D_MODEL), row)]
    out_shape = [jax.ShapeDtypeStruct((t, D_MODEL), F32)]
    args = [dest_tiles, ys, x1, gates_t, lnw, lnb]
    if next_inproj is not None:
        in_specs += [pl.BlockSpec((D_MODEL, RW_WIDTH), const), pl.BlockSpec((D_MODEL, GL_WIDTH), const)]
        out_specs += [pl.BlockSpec((te, RW_WIDTH), row), pl.BlockSpec((te, GL_WIDTH), row)]
        out_shape += [jax.ShapeDtypeStruct((t, RW_WIDTH), BF16), jax.ShapeDtypeStruct((t, GL_WIDTH), BF16)]
        args += list(next_inproj)
    return pl.pallas_call(
        functools.partial(_combine_kernel, next_inproj is not None),
        grid_spec=pltpu.PrefetchScalarGridSpec(
            num_scalar_prefetch=1,
            grid=(t // te,),
            in_specs=in_specs,
            out_specs=out_specs,
            scratch_shapes=[pltpu.VMEM((2, TOP_K, te * ROW_R, LANE), F32), pltpu.SemaphoreType.DMA((2,))],
        ),
        out_shape=out_shape,
        compiler_params=_cparams(1),
        name="moe_combine_inproj" if next_inproj is not None else "moe_combine",
    )(*args)


def _moe(x1, idx, gate, rank, counts, layer, w1, b1, w2, b2, lnw, lnb, next_inproj=None):
    t = x1.shape[0] // ROW_R
    te = min(512, t)
    assert (t * TOP_K) % MOE_BLOCK == 0
    n_blocks = (t * TOP_K) // MOE_BLOCK + N_EXPERTS
    m_pad = n_blocks * MOE_BLOCK
    padded = ((counts + MOE_BLOCK - 1) // MOE_BLOCK) * MOE_BLOCK
    pend = jnp.cumsum(padded)
    pstart = pend - padded
    dest = rank
    for e in range(N_EXPERTS):
        dest = dest + jnp.where(idx == e, pstart[e], 0)
    dest_tiles = dest.reshape(TOP_K, t // te, te).transpose(1, 0, 2).reshape(-1)
    blk_start = jnp.arange(n_blocks, dtype=jnp.int32) * MOE_BLOCK
    block_e = jnp.minimum(jnp.sum(pend[None, :] <= blk_start[:, None], axis=1), N_EXPERTS - 1).astype(jnp.int32)
    is_e = block_e[:, None] == jnp.arange(N_EXPERTS, dtype=jnp.int32)[None, :]
    blk_end = jnp.sum(jnp.where(is_e, (pstart + counts)[None, :], 0), axis=1)
    n_valid = jnp.clip(blk_end - blk_start, 0, MOE_BLOCK).astype(jnp.int32)

    gap_lo = jnp.concatenate([pstart + counts, pend[-1:]])
    gap_hi = jnp.concatenate([pend, jnp.full((1,), m_pad, pend.dtype)])
    gaps = jnp.stack([gap_lo, gap_hi], axis=1).reshape(-1).astype(jnp.int32)

    xs = _dispatch(x1, dest_tiles, gaps, m_pad, te)
    ys = _experts(xs, block_e, n_valid, layer, w1, b1, w2, b2)
    return _combine(ys, dest_tiles, x1, gate.T, lnw, lnb, te, next_inproj)


def _pad_cols(w, width):
    return jnp.pad(w, ((0, 0), (0, width - w.shape[1])))


def _pad_rows(w, height):
    return jnp.pad(w, ((0, height - w.shape[0]), (0, 0)))


def _pack_inproj(w_in_l, mu_l):
    d3 = 3 * D_RWKV
    o_ad = d3 + W_LORA
    o_gd = o_ad + A_LORA
    o_gla = o_gd + G_LORA
    wr = jnp.concatenate([w_in_l[:, :d3],
                          _pad_cols(w_in_l[:, d3:o_ad], LANE),
                          _pad_cols(w_in_l[:, o_ad:o_gd], LANE),
                          _pad_cols(w_in_l[:, o_gd:o_gla], 2 * LANE)], axis=1)
    mu = mu_l[None, :]
    mu_p = jnp.concatenate([mu[:, :d3], _pad_cols(mu[:, d3:o_ad], LANE), _pad_cols(mu[:, o_ad:o_gd], LANE),
                            _pad_cols(mu[:, o_gd:o_gla], 2 * LANE)], axis=1)
    g = w_in_l[:, o_gla:]
    wg = jnp.concatenate([g[:, :GL_GK], _pad_cols(g[:, GL_GK:GL_GK + GLA_GATE_RANK], LANE),
                          g[:, GL_GK + GLA_GATE_RANK:]], axis=1)
    return wr.astype(BF16), wg.astype(BF16), mu_p


def kernel(x, ln_in_w, ln_in_b, w_in, rwkv_mu, rwkv_w0, rwkv_w_up, rwkv_a0, rwkv_a_up, rwkv_g_up, rwkv_k_k, rwkv_k_a, rwkv_r_k, rwkv_gn_w, rwkv_gn_b, rwkv_v0, rwkv_v_down, rwkv_v_up, gla_gk_up, gla_gk_b, gla_norm_w, w_out, ln1_w, ln1_b, router_w, router_b, exp_w1, exp_b1, exp_w2, exp_b2, ln2_w, ln2_b):
    bsz, seq, d = x.shape
    t = bsz * seq
    xf = x.reshape(t, d)
    vfirst = None
    packed = [_pack_inproj(w_in[l], rwkv_mu[l]) for l in range(DEPTH)]
    xf, p_r, p_g = _inproj(xf, ln_in_w[None, :], ln_in_b[None, :], packed[0][0], packed[0][1])
    for l in range(DEPTH):
        mu_p = packed[l][2]
        rprm = {
            "mu": mu_p, "w0": rwkv_w0[l][None, :], "w_up": _pad_rows(rwkv_w_up[l], LANE).astype(BF16),
            "a0": rwkv_a0[l][None, :], "a_up": _pad_rows(rwkv_a_up[l], LANE).astype(BF16),
            "g_up": _pad_rows(rwkv_g_up[l], 2 * LANE).astype(BF16),
            "k_k": rwkv_k_k[l][None, :], "k_a": rwkv_k_a[l][None, :], "r_k": rwkv_r_k[l][None, :],
            "gn_w": rwkv_gn_w[l][None, :], "gn_b": rwkv_gn_b[l][None, :],
        }
        if l > 0:
            rprm["v0"] = rwkv_v0[l - 1][None, :]
            rprm["v_down"] = _pad_cols(rwkv_v_down[l - 1], LANE).astype(BF16)
            rprm["v_up"] = _pad_rows(rwkv_v_up[l - 1], LANE).astype(BF16)
        y_r, vfirst = _rwkv_mixer(p_r.reshape(bsz, seq, RW_WIDTH), vfirst, rprm, has_vres=l > 0)
        gprm = {"gk_up": _pad_rows(gla_gk_up[l], LANE).astype(BF16), "gk_b": gla_gk_b[l][None, :],
                "norm_w": jnp.tile(gla_norm_w[l], GLA_HEADS)[None, :]}
        y_g = _gla_mixer(p_g.reshape(bsz, seq, GL_WIDTH), gprm)
        x1, idx, gate, rank, cnt = _outproj_router(
            y_r.reshape(t, D_RWKV), y_g.reshape(t, D_GLA), xf, w_out[l].astype(BF16),
            ln1_w[l][None, :], ln1_b[l][None, :], router_w[l].T, router_b[l][:, None])
        nxt = packed[l + 1][:2] if l + 1 < DEPTH else None
        res = _moe(x1, idx, gate, rank, cnt[:, 0], l, exp_w1, exp_b1[:, :, None, :],
                   exp_w2, exp_b2[:, :, None, :], ln2_w[l][None, :], ln2_b[l][None, :], nxt)
        if nxt is not None:
            xf, p_r, p_g = res
        else:
            xf = res[0]
    return xf.reshape(bsz, seq, d)
```

```python
import functools

import jax
import jax.numpy as jnp
from jax import lax
from jax.experimental import pallas as pl
from jax.experimental.pallas import tpu as pltpu

F32 = jnp.float32
BF16 = jnp.bfloat16
HIGHEST = lax.Precision.HIGHEST

D_MODEL = 1024
DEPTH = 2
CHUNK = 64
RWKV_HEAD = 64
D_RWKV = 512
RWKV_HEADS = 8
W_LORA = 64
A_LORA = 64
V_LORA = 32
G_LORA = 160
RWKV_GN_EPS = 64e-5
D_GLA = 512
GLA_HEADS = 4
GLA_DV = 128
GLA_DK = 64
GLA_GATE_RANK = 16
GLA_GATE_NORMALIZER = 16.0
RMS_EPS = 1e-6
N_EXPERTS = 32
TOP_K = 4
D_FF = 1024
SWIGLU_LIMIT = 7.0
SWIGLU_ALPHA = 1.702
MOE_BLOCK = 512
DEEPNORM_ALPHA = (2 * DEPTH) ** 0.25
LN_EPS = 1e-5

LANE = 128
SUBLANE = 8
ROW_R = D_MODEL // LANE
assert ROW_R == SUBLANE
RW_WD = 3 * D_RWKV
RW_AD = RW_WD + LANE
RW_GD = RW_AD + LANE
RW_WIDTH = RW_GD + 2 * LANE
GL_K = GLA_HEADS * GLA_DK
GL_V = 2 * GL_K
GL_GK = GL_V + D_GLA
GL_G = GL_GK + LANE
GL_WIDTH = GL_G + D_GLA

VMEM_LIMIT = 56 * 1024 * 1024


def _cparams(n_axes):
    return pltpu.CompilerParams(dimension_semantics=("arbitrary",) * n_axes,
                                vmem_limit_bytes=VMEM_LIMIT)


def _dot(a, b):
    return jnp.dot(a, b, preferred_element_type=F32)


def _dot_nt(a, b):
    return lax.dot_general(a, b, (((1,), (1,)), ((), ())), preferred_element_type=F32)


def _dot_tn(a, b):
    return lax.dot_general(a, b, (((0,), (0,)), ((), ())), preferred_element_type=F32)


def _split_bf16(x):
    hi = x.astype(BF16)
    lo = (x - hi.astype(F32)).astype(BF16)
    return hi, lo


def _layer_norm(z, w, b):
    mu = jnp.mean(z, axis=-1, keepdims=True)
    zc = z - mu
    var = jnp.mean(zc * zc, axis=-1, keepdims=True)
    return zc * lax.rsqrt(var + LN_EPS) * w + b


def _sigmoid(x):
    return 1.0 / (1.0 + jnp.exp(-x))


def _softplus(x):
    return jnp.maximum(x, 0.0) + jnp.log(1.0 + jnp.exp(-jnp.abs(x)))


def _rows_load(ref, n):
    return jnp.concatenate([ref[pl.ds(s, n, stride=ROW_R), :] for s in range(ROW_R)], axis=1)


def _rows_store(ref, val):
    for s in range(ROW_R):
        ref[pl.ds(s, val.shape[0], stride=ROW_R), :] = val[:, s * LANE:(s + 1) * LANE]


def _row_tile(ref, row):
    return ref.at[pl.ds(pl.multiple_of(row * ROW_R, ROW_R), ROW_R), :]


def _tril_mask(n, strict):
    r = lax.broadcasted_iota(jnp.int32, (n, n), 0)
    c = lax.broadcasted_iota(jnp.int32, (n, n), 1)
    return (c < r) if strict else (c <= r)


def _inproj_kernel(x_ref, lnw_ref, lnb_ref, wr_ref, wg_ref, x0_ref, pr_ref, pg_ref):
    x = _layer_norm(x_ref[...], lnw_ref[...], lnb_ref[...])
    x0_ref[...] = x
    xb = x.astype(BF16)
    pr_ref[...] = _dot(xb, wr_ref[...]).astype(BF16)
    pg_ref[...] = _dot(xb, wg_ref[...]).astype(BF16)


def _inproj(x, lnw, lnb, wr, wg):
    t = x.shape[0]
    tm = min(512, t)
    row = lambda i: (i, 0)
    const = lambda i: (0, 0)
    return pl.pallas_call(
        _inproj_kernel,
        grid=(t // tm,),
        in_specs=[pl.BlockSpec((tm, D_MODEL), row),
                  pl.BlockSpec((1, D_MODEL), const), pl.BlockSpec((1, D_MODEL), const),
                  pl.BlockSpec((D_MODEL, RW_WIDTH), const), pl.BlockSpec((D_MODEL, GL_WIDTH), const)],
        out_specs=[pl.BlockSpec((tm, D_MODEL), row), pl.BlockSpec((tm, RW_WIDTH), row),
                   pl.BlockSpec((tm, GL_WIDTH), row)],
        out_shape=[jax.ShapeDtypeStruct((t, D_MODEL), F32), jax.ShapeDtypeStruct((t, RW_WIDTH), BF16),
                   jax.ShapeDtypeStruct((t, GL_WIDTH), BF16)],
        compiler_params=_cparams(1),
        name="inproj_ln",
    )(x, lnw, lnb, wr, wg)


def _head_sum(x, ones_pair):
    n, tiles = x.shape[0], x.shape[1] // LANE
    st = jnp.concatenate([x[:, j * LANE:(j + 1) * LANE] for j in range(tiles)], axis=0)
    hi, lo = _split_bf16(st)
    s = _dot(jnp.concatenate([hi, lo], axis=0), ones_pair)
    s = s[:tiles * n] + s[tiles * n:]
    return jnp.concatenate([s[j * n:(j + 1) * n] for j in range(tiles)], axis=1)


def _rwkv_kernel(has_vres, *refs):
    if has_vres:
        (p_ref, vfirst_ref, mu_ref, w0_ref, wup_ref, a0_ref, aup_ref, gup_ref, kk_ref, ka_ref, rk_ref,
         gnw_ref, gnb_ref, v0_ref, vdown_ref, vup_ref, y_ref, state_ref, prev_ref) = refs
    else:
        (p_ref, mu_ref, w0_ref, wup_ref, a0_ref, aup_ref, gup_ref, kk_ref, ka_ref, rk_ref,
         gnw_ref, gnb_ref, y_ref, vfirst_out_ref, state_ref, prev_ref) = refs

    @pl.when(pl.program_id(1) == 0)
    def _():
        state_ref[...] = jnp.zeros_like(state_ref)
        prev_ref[...] = jnp.zeros_like(prev_ref)

    p = p_ref[0].astype(F32)
    n_tok = p.shape[0]
    row = lax.broadcasted_iota(jnp.int32, (n_tok, 1), 0)
    prev = jnp.where(row == 0, prev_ref[...], pltpu.roll(p, 1, axis=0))
    prev_ref[...] = p[n_tok - 1:n_tok, :]
    xs = p + (prev - p) * mu_ref[...]
    r = xs[:, 0:D_RWKV]
    k = xs[:, D_RWKV:2 * D_RWKV]
    v = xs[:, 2 * D_RWKV:3 * D_RWKV]
    wd = xs[:, RW_WD:RW_AD]
    ad = xs[:, RW_AD:RW_GD]
    gd = xs[:, RW_GD:RW_WIDTH]

    w = w0_ref[...] + _dot(jnp.tanh(wd).astype(BF16), wup_ref[...])
    w = -_softplus(-w) - 0.5
    logw = -jnp.exp(w)
    a = _sigmoid(a0_ref[...] + _dot(ad.astype(BF16), aup_ref[...]))
    g = _dot(_sigmoid(gd).astype(BF16), gup_ref[...])
    if has_vres:
        vmix = _dot(_dot(v.astype(BF16), vdown_ref[...]).astype(BF16), vup_ref[...])
        v = v + (vfirst_ref[0] - v) * _sigmoid(v0_ref[...] + vmix)
    else:
        vfirst_out_ref[0] = v

    hr = lax.broadcasted_iota(jnp.int32, (LANE, LANE), 0) // RWKV_HEAD
    hc = lax.broadcasted_iota(jnp.int32, (LANE, LANE), 1) // RWKV_HEAD
    ones_blk = (hr == hc).astype(BF16)

    kk = k * kk_ref[...]
    kk = kk * lax.rsqrt(jnp.maximum(_head_sum(kk * kk, ones_blk), 1e-24))
    k = k * (1.0 + (a - 1.0) * ka_ref[...])
    bonus = _head_sum(r * k * rk_ref[...], ones_blk)

    n_chunks = n_tok // CHUNK
    n_pairs = RWKV_HEADS // 2
    assert 2 * RWKV_HEAD == LANE and CHUNK == RWKV_HEAD
    rr = lax.broadcasted_iota(jnp.int32, (CHUNK, CHUNK), 0)
    cc = lax.broadcasted_iota(jnp.int32, (CHUNK, CHUNK), 1)
    tril_f = (cc <= rr).astype(F32)
    r_p = lax.broadcasted_iota(jnp.int32, (CHUNK, LANE), 0)
    l_p = lax.broadcasted_iota(jnp.int32, (CHUNK, LANE), 1)
    c_p = jnp.where(l_p >= RWKV_HEAD, l_p - RWKV_HEAD, l_p)
    m_left = (l_p < RWKV_HEAD).astype(F32)
    m_right = (l_p >= RWKV_HEAD).astype(F32)
    m_left_b = m_left.astype(BF16)
    m_right_b = m_right.astype(BF16)
    m_strict = (c_p < r_p).astype(F32)
    m_lower = (c_p <= r_p).astype(F32)
    eye_p = (c_p == r_p).astype(F32)
    m_diag2 = ((r_p // 2) == (c_p // 2)).astype(F32)
    lvl_masks = []
    size = 2
    while size < CHUNK:
        lvl_masks.append((((r_p // (2 * size)) == (c_p // (2 * size))) & ((r_p // size) != (c_p // size))).astype(F32))
        size *= 2

    def blockdiag(xb):
        return jnp.concatenate([xb * m_left_b, xb * m_right_b], axis=0)

    items = []
    g_ends = []
    for c in range(n_chunks):
        sl = slice(c * CHUNK, (c + 1) * CHUNK)
        lw = logw[sl]
        b = jnp.dot(tril_f, lw, precision=HIGHEST, preferred_element_type=F32)
        eb = jnp.exp(b)
        enb = jnp.exp(-b)
        ebx = jnp.exp(b - lw)
        rh = (r[sl] * eb).astype(BF16)
        kh = (k[sl] * enb).astype(BF16)
        ah = (-kk[sl] * ebx).astype(BF16)
        bh = (kk[sl] * a[sl] * enb).astype(BF16)
        vb = v[sl].astype(BF16)
        g_ends.append(eb[CHUNK - 1:CHUNK, :])
        for j in range(n_pairs):
            ls = slice(j * LANE, (j + 1) * LANE)
            items.append(dict(ah=ah[:, ls], rh=rh[:, ls], bh=bh[:, ls], kh=kh[:, ls], v=vb[:, ls]))

    for it in items:
        ar = jnp.concatenate([it["ah"], it["rh"]], axis=0)
        it["sb"] = _dot_nt(ar, blockdiag(it["bh"]))
        it["sk"] = _dot_nt(ar, blockdiag(it["kh"]))
    for it in items:
        it["a_ab"] = it["sb"][:CHUNK] * m_strict
        it["ak"] = (it["sk"][:CHUNK] * m_strict).astype(BF16)
        it["rbk"] = jnp.concatenate([(it["sb"][CHUNK:] * m_lower).astype(BF16),
                                     (it["sk"][CHUNK:] * m_lower).astype(BF16)], axis=1)
        it["d"] = eye_p + it["a_ab"] * m_diag2
        del it["sb"], it["sk"]
    for mk in lvl_masks:
        for it in items:
            it["db"] = it["d"].astype(BF16)
            it["m"] = _dot((it["a_ab"] * mk).astype(BF16), blockdiag(it["db"]))
        for it in items:
            it["d"] = it["d"] + _dot(it["db"], blockdiag(it["m"].astype(BF16)))
    for it in items:
        it["akv"] = _dot(it["ak"], blockdiag(it["v"]))
        it["tinv"] = it["d"].astype(BF16)
    for it in items:
        it["w"] = _dot(it["tinv"], blockdiag(it["ah"]))
        it["ut"] = _dot(it["tinv"], blockdiag(it["akv"].astype(BF16)))

    states = [state_ref[j] for j in range(n_pairs)]
    y_chunks = []
    for c in range(n_chunks):
        its = items[c * n_pairs:(c + 1) * n_pairs]
        for j, it in enumerate(its):
            lhs = jnp.concatenate([it["w"].astype(BF16), it["rh"]], axis=0)
            it["ws"] = _dot_nt(lhs, blockdiag(states[j].astype(BF16)))
        for j, it in enumerate(its):
            it["ub"] = (it["ws"][:CHUNK] + it["ut"]).astype(BF16)
            uv = jnp.concatenate([it["ub"], it["v"]], axis=0)
            upd = _dot_tn(uv, jnp.concatenate([it["bh"], it["kh"]], axis=0))
            ls = slice(j * LANE, (j + 1) * LANE)
            states[j] = (states[j] + upd[:RWKV_HEAD] * m_left + upd[RWKV_HEAD:] * m_right) * g_ends[c][:, ls]
        y_pairs = [it["ws"][CHUNK:] + _dot(it["rbk"], jnp.concatenate([blockdiag(it["ub"]), blockdiag(it["v"])], axis=0))
                   for it in its]
        y_chunks.append(jnp.concatenate(y_pairs, axis=1))
    for j in range(n_pairs):
        state_ref[j] = states[j]
    y = jnp.concatenate(y_chunks, axis=0) if len(y_chunks) > 1 else y_chunks[0]

    inv_n = 1.0 / RWKV_HEAD
    m = _head_sum(y, ones_blk) * inv_n
    yc = y - m
    var = _head_sum(yc * yc, ones_blk) * inv_n
    y = yc * lax.rsqrt(var + RWKV_GN_EPS) * gnw_ref[...] + gnb_ref[...]
    y = y + bonus * v
    y_ref[0] = (y * g).astype(BF16)


def _rwkv_mixer(p_r, vfirst, prm, has_vres):
    bsz, seq, _ = p_r.shape
    n_tok = min(256, seq)
    tok = lambda b, i: (b, i, 0)
    const = lambda b, i: (0, 0)
    vec = pl.BlockSpec((1, D_RWKV), const)
    in_specs = [pl.BlockSpec((1, n_tok, RW_WIDTH), tok)]
    args = [p_r]
    if has_vres:
        in_specs.append(pl.BlockSpec((1, n_tok, D_RWKV), tok))
        args.append(vfirst)
    in_specs += [pl.BlockSpec((1, RW_WIDTH), const), vec, pl.BlockSpec((LANE, D_RWKV), const), vec,
                 pl.BlockSpec((LANE, D_RWKV), const), pl.BlockSpec((2 * LANE, D_RWKV), const),
                 vec, vec, vec, vec, vec]
    args += [prm["mu"], prm["w0"], prm["w_up"], prm["a0"], prm["a_up"], prm["g_up"],
             prm["k_k"], prm["k_a"], prm["r_k"], prm["gn_w"], prm["gn_b"]]
    out_shape = [jax.ShapeDtypeStruct((bsz, seq, D_RWKV), BF16)]
    out_specs = [pl.BlockSpec((1, n_tok, D_RWKV), tok)]
    if has_vres:
        in_specs += [vec, pl.BlockSpec((D_RWKV, LANE), const), pl.BlockSpec((LANE, D_RWKV), const)]
        args += [prm["v0"], prm["v_down"], prm["v_up"]]
    else:
        out_shape.append(jax.ShapeDtypeStruct((bsz, seq, D_RWKV), F32))
        out_specs.append(pl.BlockSpec((1, n_tok, D_RWKV), tok))
    res = pl.pallas_call(
        functools.partial(_rwkv_kernel, has_vres),
        grid=(bsz, seq // n_tok),
        in_specs=in_specs,
        out_specs=out_specs,
        out_shape=out_shape,
        scratch_shapes=[pltpu.VMEM((RWKV_HEADS // 2, RWKV_HEAD, 2 * RWKV_HEAD), F32),
                        pltpu.VMEM((1, RW_WIDTH), F32)],
        compiler_params=_cparams(2),
        name="rwkv7_vres" if has_vres else "rwkv7",
    )(*args)
    if has_vres:
        return res[0], vfirst
    return res[0], res[1]


def _gla_kernel(p_ref, gkup_ref, gkb_ref, nw_ref, y_ref, state_ref):
    @pl.when(pl.program_id(1) == 0)
    def _():
        state_ref[...] = jnp.zeros_like(state_ref)

    p = p_ref[0].astype(F32)
    n_tok = p.shape[0]
    q = p[:, 0:GL_K]
    k = p[:, GL_K:GL_V]
    v = p[:, GL_V:GL_GK]
    gkd = p[:, GL_GK:GL_G]
    g = p[:, GL_G:GL_WIDTH]
    z = _dot(gkd.astype(BF16), gkup_ref[...]) + gkb_ref[...]
    gk = -_softplus(-z) * (1.0 / GLA_GATE_NORMALIZER)

    lower = _tril_mask(CHUNK, strict=False)
    tril_f = lower.astype(F32)
    n_chunks = n_tok // CHUNK
    items = []
    decs = []
    for c in range(n_chunks):
        sl = slice(c * CHUNK, (c + 1) * CHUNK)
        b = jnp.dot(tril_f, gk[sl], precision=HIGHEST, preferred_element_type=F32)
        b_last = b[CHUNK - 1:CHUNK, :]
        q_e = (q[sl] * jnp.exp(b) * (GLA_DK ** -0.5)).astype(BF16)
        k_e = (k[sl] * jnp.exp(-b)).astype(BF16)
        k_end = (k[sl] * jnp.exp(b_last - b)).astype(BF16)
        decs.append(jnp.exp(b_last))
        vb = v[sl].astype(BF16)
        for h in range(GLA_HEADS):
            ks = slice(h * GLA_DK, (h + 1) * GLA_DK)
            vs = slice(h * GLA_DV, (h + 1) * GLA_DV)
            items.append(dict(q=q_e[:, ks], k=k_e[:, ks], kend=k_end[:, ks], v=vb[:, vs]))
    for it in items:
        it["sc"] = jnp.where(lower, _dot_nt(it["q"], it["k"]), 0.0).astype(BF16)
    for it in items:
        it["intra"] = _dot(it["sc"], it["v"])
        it["kv"] = _dot_tn(it["v"], it["kend"])
    states = [state_ref[h] for h in range(GLA_HEADS)]
    o_chunks = []
    for c in range(n_chunks):
        o_heads = []
        for h in range(GLA_HEADS):
            it = items[c * GLA_HEADS + h]
            ks = slice(h * GLA_DK, (h + 1) * GLA_DK)
            o_h = it["intra"] + _dot_nt(it["q"], states[h].astype(BF16))
            states[h] = states[h] * decs[c][:, ks] + it["kv"]
            o_heads.append(o_h * lax.rsqrt(jnp.mean(o_h * o_h, axis=-1, keepdims=True) + RMS_EPS))
        o_chunks.append(jnp.concatenate(o_heads, axis=1))
    for h in range(GLA_HEADS):
        state_ref[h] = states[h]
    o = jnp.concatenate(o_chunks, axis=0) if len(o_chunks) > 1 else o_chunks[0]
    silu_g = g * _sigmoid(g)
    y_ref[0] = (o * nw_ref[...] * silu_g).astype(BF16)


def _gla_mixer(p_g, prm):
    bsz, seq, _ = p_g.shape
    n_tok = min(256, seq)
    tok = lambda b, i: (b, i, 0)
    const = lambda b, i: (0, 0)
    return pl.pallas_call(
        _gla_kernel,
        grid=(bsz, seq // n_tok),
        in_specs=[pl.BlockSpec((1, n_tok, GL_WIDTH), tok),
                  pl.BlockSpec((LANE, GL_K), const), pl.BlockSpec((1, GL_K), const),
                  pl.BlockSpec((1, D_GLA), const)],
        out_specs=pl.BlockSpec((1, n_tok, D_GLA), tok),
        out_shape=jax.ShapeDtypeStruct((bsz, seq, D_GLA), BF16),
        scratch_shapes=[pltpu.VMEM((GLA_HEADS, GLA_DV, GLA_DK), F32)],
        compiler_params=_cparams(2),
        name="gla",
    )(p_g, prm["gk_up"], prm["gk_b"], prm["norm_w"])


def _outproj_router_kernel(yr_ref, yg_ref, x_ref, wo_ref, lnw_ref, lnb_ref, rw_ref, rb_ref,
                           x1_ref, idx_ref, gate_ref, rank_ref, cnt_ref, carry_ref):
    @pl.when(pl.program_id(0) == 0)
    def _():
        carry_ref[...] = jnp.zeros_like(carry_ref)

    mix = _dot(yr_ref[...], wo_ref[0:D_RWKV, :]) + _dot(yg_ref[...], wo_ref[D_RWKV:, :])
    x1 = _layer_norm(DEEPNORM_ALPHA * x_ref[...] + mix, lnw_ref[...], lnb_ref[...])
    _rows_store(x1_ref, x1)
    tm = x1.shape[0]

    xh, xl = _split_bf16(x1)
    rw = rw_ref[...]
    wh, wl = _split_bf16(rw)
    logits = _dot_nt(wh, xh) + _dot_nt(wh, xl) + _dot_nt(wl, xh) + rb_ref[...]

    e_iota = lax.broadcasted_iota(jnp.int32, (N_EXPERTS, tm), 0)
    work = logits
    vals, idxs = [], []
    member = jnp.zeros((N_EXPERTS, tm), F32)
    for _ in range(TOP_K):
        mx = jnp.max(work, axis=0, keepdims=True)
        ix = jnp.min(jnp.where(work == mx, e_iota, N_EXPERTS), axis=0, keepdims=True)
        sel = e_iota == ix
        work = jnp.where(sel, -jnp.inf, work)
        member = jnp.where(sel, 1.0, member)
        vals.append(mx)
        idxs.append(ix)
    exps = [jnp.exp(vv - vals[0]) for vv in vals]
    inv_den = 1.0 / (exps[0] + exps[1] + exps[2] + exps[3])

    tr = lax.broadcasted_iota(jnp.int32, (tm, tm), 0)
    tc = lax.broadcasted_iota(jnp.int32, (tm, tm), 1)
    before = (tr < tc).astype(BF16)
    cex = _dot(member.astype(BF16), before) + carry_ref[...][:, 0:1]
    for kq in range(TOP_K):
        sel = e_iota == idxs[kq]
        idx_ref[kq:kq + 1, :] = idxs[kq]
        gate_ref[kq:kq + 1, :] = exps[kq] * inv_den
        rank_ref[kq:kq + 1, :] = jnp.sum(jnp.where(sel, cex, 0.0), axis=0, keepdims=True).astype(jnp.int32)
    carry_ref[...] = carry_ref[...] + jnp.sum(member, axis=1, keepdims=True)
    cnt_ref[...] = carry_ref[...].astype(jnp.int32)


def _outproj_router(yr, yg, x, wo, lnw, lnb, rw_t, rb):
    t = x.shape[0]
    tm = min(512, t)
    row = lambda i: (i, 0)
    col = lambda i: (0, i)
    const = lambda i: (0, 0)
    return pl.pallas_call(
        _outproj_router_kernel,
        grid=(t // tm,),
        in_specs=[pl.BlockSpec((tm, D_RWKV), row), pl.BlockSpec((tm, D_GLA), row),
                  pl.BlockSpec((tm, D_MODEL), row), pl.BlockSpec((D_MODEL, D_MODEL), const),
                  pl.BlockSpec((1, D_MODEL), const), pl.BlockSpec((1, D_MODEL), const),
                  pl.BlockSpec((N_EXPERTS, D_MODEL), const), pl.BlockSpec((N_EXPERTS, 1), const)],
        out_specs=[pl.BlockSpec((tm * ROW_R, LANE), row),
                   pl.BlockSpec((TOP_K, tm), col), pl.BlockSpec((TOP_K, tm), col),
                   pl.BlockSpec((TOP_K, tm), col), pl.BlockSpec((N_EXPERTS, LANE), const)],
        out_shape=[jax.ShapeDtypeStruct((t * ROW_R, LANE), F32),
                   jax.ShapeDtypeStruct((TOP_K, t), jnp.int32), jax.ShapeDtypeStruct((TOP_K, t), F32),
                   jax.ShapeDtypeStruct((TOP_K, t), jnp.int32),
                   jax.ShapeDtypeStruct((N_EXPERTS, LANE), jnp.int32)],
        scratch_shapes=[pltpu.VMEM((N_EXPERTS, LANE), F32)],
        compiler_params=_cparams(1),
        name="outproj_router",
    )(yr, yg, x, wo, lnw, lnb, rw_t, rb)


def _dispatch_kernel(dest_ref, gaps_ref, x_ref, xs_ref, zero_ref, sem, zsem):
    te = x_ref.shape[0] // ROW_R
    base = pl.program_id(0) * (TOP_K * te)

    @pl.when(pl.program_id(0) == 0)
    def _():
        zero_ref[...] = jnp.zeros_like(zero_ref)

        def zero_copy(d):
            return pltpu.make_async_copy(zero_ref, _row_tile(xs_ref, d), zsem)

        for g in range(N_EXPERTS + 1):
            lo = gaps_ref[2 * g]
            hi = gaps_ref[2 * g + 1]
            lax.fori_loop(lo, hi, lambda d, c: (zero_copy(d).start(), c)[1], 0)
        blk = pl.ds(0, MOE_BLOCK * ROW_R)
        for g in range(N_EXPERTS):
            pltpu.make_async_copy(xs_ref.at[blk, :], xs_ref.at[blk, :], zsem).wait()

    def issue(r, carry):
        for kq in range(TOP_K):
            d = dest_ref[base + kq * te + r]
            pltpu.make_async_copy(_row_tile(x_ref, r), _row_tile(xs_ref, d), sem).start(priority=kq % 2)
        return carry

    lax.fori_loop(0, te, issue, 0, unroll=8)
    for kq in range(TOP_K):
        pltpu.make_async_copy(x_ref, x_ref, sem).wait()


def _dispatch(x1, dest_tiles, gaps, m_pad, te):
    t = x1.shape[0] // ROW_R
    return pl.pallas_call(
        _dispatch_kernel,
        grid_spec=pltpu.PrefetchScalarGridSpec(
            num_scalar_prefetch=2,
            grid=(t // te,),
            in_specs=[pl.BlockSpec((te * ROW_R, LANE), lambda i, d, g: (i, 0))],
            out_specs=pl.BlockSpec(memory_space=pl.ANY),
            scratch_shapes=[pltpu.VMEM((ROW_R, LANE), F32), pltpu.SemaphoreType.DMA(()),
                            pltpu.SemaphoreType.DMA(())],
        ),
        out_shape=jax.ShapeDtypeStruct((m_pad * ROW_R, LANE), F32),
        compiler_params=_cparams(1),
        name="moe_dispatch",
    )(dest_tiles, gaps, x1)


def _expert_kernel(be_ref, nv_ref, xs_ref, w1_ref, b1_ref, w2_ref, b2_ref, ys_ref, act_ref, w1b_ref, w2b_ref):
    j = pl.program_id(0)
    n_valid = nv_ref[j]

    @pl.when((j == 0) | (be_ref[j] != be_ref[jnp.maximum(j - 1, 0)]))
    def _():
        rows = 128
        for c in range(D_MODEL // rows):
            rs = slice(c * rows, (c + 1) * rows)
            w1b_ref[rs, :] = w1_ref[0, 0, rs, :].astype(BF16)
            w2b_ref[rs, :] = w2_ref[0, 0, rs, :].astype(BF16)

    @pl.when(n_valid == 0)
    def _():
        ys_ref[...] = jnp.zeros_like(ys_ref)

    @pl.when(n_valid > 0)
    def _():
        xb = _rows_load(xs_ref, MOE_BLOCK).astype(BF16)
        n_chunk = 256
        for c in range(D_FF // n_chunk):
            cs = slice(c * n_chunk, (c + 1) * n_chunk)
            us = slice(D_FF + c * n_chunk, D_FF + (c + 1) * n_chunk)
            gt = _dot(xb, w1b_ref[:, cs]) + b1_ref[0, 0, :, cs]
            up = _dot(xb, w1b_ref[:, us]) + b1_ref[0, 0, :, us]
            gt = jnp.minimum(gt, SWIGLU_LIMIT)
            up = jnp.clip(up, -SWIGLU_LIMIT, SWIGLU_LIMIT)
            act_ref[:, cs] = ((up + 1.0) * gt * _sigmoid(SWIGLU_ALPHA * gt)).astype(BF16)
        _rows_store(ys_ref, _dot(act_ref[...], w2b_ref[...]) + b2_ref[0, 0])


def _experts(xs, block_e, n_valid, layer, w1, b1, w2, b2):
    m_pad = xs.shape[0] // ROW_R
    n_blocks = m_pad // MOE_BLOCK
    wmap = lambda j, be, nv: (layer, be[j], 0, 0)
    return pl.pallas_call(
        _expert_kernel,
        grid_spec=pltpu.PrefetchScalarGridSpec(
            num_scalar_prefetch=2,
            grid=(n_blocks,),
            in_specs=[pl.BlockSpec((MOE_BLOCK * ROW_R, LANE), lambda j, be, nv: (j, 0)),
                      pl.BlockSpec((1, 1, D_MODEL, 2 * D_FF), wmap),
                      pl.BlockSpec((1, 1, 1, 2 * D_FF), wmap),
                      pl.BlockSpec((1, 1, D_FF, D_MODEL), wmap),
                      pl.BlockSpec((1, 1, 1, D_MODEL), wmap)],
            out_specs=pl.BlockSpec((MOE_BLOCK * ROW_R, LANE), lambda j, be, nv: (j, 0)),
            scratch_shapes=[pltpu.VMEM((MOE_BLOCK, D_FF), BF16),
                            pltpu.VMEM((D_MODEL, 2 * D_FF), BF16), pltpu.VMEM((D_FF, D_MODEL), BF16)],
        ),
        out_shape=jax.ShapeDtypeStruct((m_pad * ROW_R, LANE), F32),
        compiler_params=_cparams(1),
        name="moe_experts",
    )(block_e, n_valid, xs, w1, b1, w2, b2)


def _combine_kernel(with_inproj, dest_ref, ys_ref, x1_ref, gate_ref, lnw_ref, lnb_ref, *refs):
    if with_inproj:
        wr_ref, wg_ref, out_ref, pr_ref, pg_ref, buf_ref, sem = refs
    else:
        out_ref, buf_ref, sem = refs
    te = x1_ref.shape[0] // ROW_R
    i = pl.program_id(0)
    n_tiles = pl.num_programs(0)

    def gather(tile, slot):
        base = tile * (TOP_K * te)

        def issue(r, carry):
            for kq in range(TOP_K):
                d = dest_ref[base + kq * te + r]
                pltpu.make_async_copy(_row_tile(ys_ref, d), _row_tile(buf_ref.at[slot, kq], r),
                                      sem.at[slot]).start(priority=kq % 2)
            return carry

        lax.fori_loop(0, te, issue, 0, unroll=8)

    @pl.when(i == 0)
    def _():
        gather(0, 0)

    for slot in range(2):
        @pl.when((i + 1 < n_tiles) & ((i + 1) % 2 == slot))
        def _():
            gather(i + 1, slot)

    for slot in range(2):
        @pl.when(i % 2 == slot)
        def _():
            for kq in range(TOP_K):
                pltpu.make_async_copy(buf_ref.at[slot, kq], buf_ref.at[slot, kq], sem.at[slot]).wait()
            gates = gate_ref[...]
            z = DEEPNORM_ALPHA * _rows_load(x1_ref, te)
            for kq in range(TOP_K):
                z = z + _rows_load(buf_ref.at[slot, kq], te) * gates[:, kq:kq + 1]
            xn = _layer_norm(z, lnw_ref[...], lnb_ref[...])
            out_ref[...] = xn
            if with_inproj:
                xb = xn.astype(BF16)
                pr_ref[...] = _dot(xb, wr_ref[...]).astype(BF16)
                pg_ref[...] = _dot(xb, wg_ref[...]).astype(BF16)


def _combine(ys, dest_tiles, x1, gates_t, lnw, lnb, te, next_inproj=None):
    t = x1.shape[0] // ROW_R
    row = lambda i, d: (i, 0)
    const = lambda i, d: (0, 0)
    in_specs = [pl.BlockSpec(memory_space=pl.ANY),
                pl.BlockSpec((te * ROW_R, LANE), row),
                pl.BlockSpec((te, TOP_K), row),
                pl.BlockSpec((1, D_MODEL), const),
                pl.BlockSpec((1, D_MODEL), const)]
    out_specs = [pl.BlockSpec((te, D_MODEL), row)]
    out_shape = [jax.ShapeDtypeStruct((t, D_MODEL), F32)]
    args = [dest_tiles, ys, x1, gates_t, lnw, lnb]
    if next_inproj is not None:
        in_specs += [pl.BlockSpec((D_MODEL, RW_WIDTH), const), pl.BlockSpec((D_MODEL, GL_WIDTH), const)]
        out_specs += [pl.BlockSpec((te, RW_WIDTH), row), pl.BlockSpec((te, GL_WIDTH), row)]
        out_shape += [jax.ShapeDtypeStruct((t, RW_WIDTH), BF16), jax.ShapeDtypeStruct((t, GL_WIDTH), BF16)]
        args += list(next_inproj)
    return pl.pallas_call(
        functools.partial(_combine_kernel, next_inproj is not None),
        grid_spec=pltpu.PrefetchScalarGridSpec(
            num_scalar_prefetch=1,
            grid=(t // te,),
            in_specs=in_specs,
            out_specs=out_specs,
            scratch_shapes=[pltpu.VMEM((2, TOP_K, te * ROW_R, LANE), F32), pltpu.SemaphoreType.DMA((2,))],
        ),
        out_shape=out_shape,
        compiler_params=_cparams(1),
        name="moe_combine_inproj" if next_inproj is not None else "moe_combine",
    )(*args)


def _moe(x1, idx, gate, rank, counts, layer, w1, b1, w2, b2, lnw, lnb, next_inproj=None):
    t = x1.shape[0] // ROW_R
    te = min(256, t)
    assert (t * TOP_K) % MOE_BLOCK == 0
    n_blocks = (t * TOP_K) // MOE_BLOCK + N_EXPERTS
    m_pad = n_blocks * MOE_BLOCK
    padded = ((counts + MOE_BLOCK - 1) // MOE_BLOCK) * MOE_BLOCK
    pend = jnp.cumsum(padded)
    pstart = pend - padded
    dest = rank
    for e in range(N_EXPERTS):
        dest = dest + jnp.where(idx == e, pstart[e], 0)
    dest_tiles = dest.reshape(TOP_K, t // te, te).transpose(1, 0, 2).reshape(-1)
    blk_start = jnp.arange(n_blocks, dtype=jnp.int32) * MOE_BLOCK
    block_e = jnp.minimum(jnp.sum(pend[None, :] <= blk_start[:, None], axis=1), N_EXPERTS - 1).astype(jnp.int32)
    is_e = block_e[:, None] == jnp.arange(N_EXPERTS, dtype=jnp.int32)[None, :]
    blk_end = jnp.sum(jnp.where(is_e, (pstart + counts)[None, :], 0), axis=1)
    n_valid = jnp.clip(blk_end - blk_start, 0, MOE_BLOCK).astype(jnp.int32)

    gap_lo = jnp.concatenate([pstart + counts, pend[-1:]])
    gap_hi = jnp.concatenate([pend, jnp.full((1,), m_pad, pend.dtype)])
    gaps = jnp.stack([gap_lo, gap_hi], axis=1).reshape(-1).astype(jnp.int32)

    xs = _dispatch(x1, dest_tiles, gaps, m_pad, te)
    ys = _experts(xs, block_e, n_valid, layer, w1, b1, w2, b2)
    return _combine(ys, dest_tiles, x1, gate.T, lnw, lnb, te, next_inproj)


def _pad_cols(w, width):
    return jnp.pad(w, ((0, 0), (0, width - w.shape[1])))


def _pad_rows(w, height):
    return jnp.pad(w, ((0, height - w.shape[0]), (0, 0)))


def _pack_inproj(w_in_l, mu_l):
    d3 = 3 * D_RWKV
    o_ad = d3 + W_LORA
    o_gd = o_ad + A_LORA
    o_gla = o_gd + G_LORA
    wr = jnp.concatenate([w_in_l[:, :d3],
                          _pad_cols(w_in_l[:, d3:o_ad], LANE),
                          _pad_cols(w_in_l[:, o_ad:o_gd], LANE),
                          _pad_cols(w_in_l[:, o_gd:o_gla], 2 * LANE)], axis=1)
    mu = mu_l[None, :]
    mu_p = jnp.concatenate([mu[:, :d3], _pad_cols(mu[:, d3:o_ad], LANE), _pad_cols(mu[:, o_ad:o_gd], LANE),
                            _pad_cols(mu[:, o_gd:o_gla], 2 * LANE)], axis=1)
    g = w_in_l[:, o_gla:]
    wg = jnp.concatenate([g[:, :GL_GK], _pad_cols(g[:, GL_GK:GL_GK + GLA_GATE_RANK], LANE),
                          g[:, GL_GK + GLA_GATE_RANK:]], axis=1)
    return wr.astype(BF16), wg.astype(BF16), mu_p


def kernel(x, ln_in_w, ln_in_b, w_in, rwkv_mu, rwkv_w0, rwkv_w_up, rwkv_a0, rwkv_a_up, rwkv_g_up, rwkv_k_k, rwkv_k_a, rwkv_r_k, rwkv_gn_w, rwkv_gn_b, rwkv_v0, rwkv_v_down, rwkv_v_up, gla_gk_up, gla_gk_b, gla_norm_w, w_out, ln1_w, ln1_b, router_w, router_b, exp_w1, exp_b1, exp_w2, exp_b2, ln2_w, ln2_b):
    bsz, seq, d = x.shape
    t = bsz * seq
    xf = x.reshape(t, d)
    vfirst = None
    packed = [_pack_inproj(w_in[l], rwkv_mu[l]) for l in range(DEPTH)]
    xf, p_r, p_g = _inproj(xf, ln_in_w[None, :], ln_in_b[None, :], packed[0][0], packed[0][1])
    for l in range(DEPTH):
        mu_p = packed[l][2]
        rprm = {
            "mu": mu_p, "w0": rwkv_w0[l][None, :], "w_up": _pad_rows(rwkv_w_up[l], LANE).astype(BF16),
            "a0": rwkv_a0[l][None, :], "a_up": _pad_rows(rwkv_a_up[l], LANE).astype(BF16),
            "g_up": _pad_rows(rwkv_g_up[l], 2 * LANE).astype(BF16),
            "k_k": rwkv_k_k[l][None, :], "k_a": rwkv_k_a[l][None, :], "r_k": rwkv_r_k[l][None, :],
            "gn_w": rwkv_gn_w[l][None, :], "gn_b": rwkv_gn_b[l][None, :],
        }
        if l > 0:
            rprm["v0"] = rwkv_v0[l - 1][None, :]
            rprm["v_down"] = _pad_cols(rwkv_v_down[l - 1], LANE).astype(BF16)
            rprm["v_up"] = _pad_rows(rwkv_v_up[l - 1], LANE).astype(BF16)
        y_r, vfirst = _rwkv_mixer(p_r.reshape(bsz, seq, RW_WIDTH), vfirst, rprm, has_vres=l > 0)
        gprm = {"gk_up": _pad_rows(gla_gk_up[l], LANE).astype(BF16), "gk_b": gla_gk_b[l][None, :],
                "norm_w": jnp.tile(gla_norm_w[l], GLA_HEADS)[None, :]}
        y_g = _gla_mixer(p_g.reshape(bsz, seq, GL_WIDTH), gprm)
        x1, idx, gate, rank, cnt = _outproj_router(
            y_r.reshape(t, D_RWKV), y_g.reshape(t, D_GLA), xf, w_out[l].astype(BF16),
            ln1_w[l][None, :], ln1_b[l][None, :], router_w[l].T, router_b[l][:, None])
        nxt = packed[l + 1][:2] if l + 1 < DEPTH else None
        res = _moe(x1, idx, gate, rank, cnt[:, 0], l, exp_w1, exp_b1[:, :, None, :],
                   exp_w2, exp_b2[:, :, None, :], ln2_w[l][None, :], ln2_b[l][None, :], nxt)
        if nxt is not None:
            xf, p_r, p_g = res
        else:
            xf = res[0]
    return xf.reshape(bsz, seq, d)
```

```python
import functools

import jax
import jax.numpy as jnp
from jax import lax
from jax.experimental import pallas as pl
from jax.experimental.pallas import tpu as pltpu

F32 = jnp.float32
BF16 = jnp.bfloat16
HIGHEST = lax.Precision.HIGHEST

D_MODEL = 1024
DEPTH = 2
CHUNK = 64
RWKV_HEAD = 64
D_RWKV = 512
RWKV_HEADS = 8
W_LORA = 64
A_LORA = 64
V_LORA = 32
G_LORA = 160
RWKV_GN_EPS = 64e-5
D_GLA = 512
GLA_HEADS = 4
GLA_DV = 128
GLA_DK = 64
GLA_GATE_RANK = 16
GLA_GATE_NORMALIZER = 16.0
RMS_EPS = 1e-6
N_EXPERTS = 32
TOP_K = 4
D_FF = 1024
SWIGLU_LIMIT = 7.0
SWIGLU_ALPHA = 1.702
MOE_BLOCK = 512
DEEPNORM_ALPHA = (2 * DEPTH) ** 0.25
LN_EPS = 1e-5

LANE = 128
SUBLANE = 8
ROW_R = D_MODEL // LANE
assert ROW_R == SUBLANE
RW_WD = 3 * D_RWKV
RW_AD = RW_WD + LANE
RW_GD = RW_AD + LANE
RW_WIDTH = RW_GD + 2 * LANE
GL_K = GLA_HEADS * GLA_DK
GL_V = 2 * GL_K
GL_GK = GL_V + D_GLA
GL_G = GL_GK + LANE
GL_WIDTH = GL_G + D_GLA

VMEM_LIMIT = 56 * 1024 * 1024


def _cparams(n_axes):
    return pltpu.CompilerParams(dimension_semantics=("arbitrary",) * n_axes,
                                vmem_limit_bytes=VMEM_LIMIT)


def _dot(a, b):
    return jnp.dot(a, b, preferred_element_type=F32)


def _dot_nt(a, b):
    return lax.dot_general(a, b, (((1,), (1,)), ((), ())), preferred_element_type=F32)


def _dot_tn(a, b):
    return lax.dot_general(a, b, (((0,), (0,)), ((), ())), preferred_element_type=F32)


def _split_bf16(x):
    hi = x.astype(BF16)
    lo = (x - hi.astype(F32)).astype(BF16)
    return hi, lo


def _layer_norm(z, w, b):
    mu = jnp.mean(z, axis=-1, keepdims=True)
    zc = z - mu
    var = jnp.mean(zc * zc, axis=-1, keepdims=True)
    return zc * lax.rsqrt(var + LN_EPS) * w + b


def _sigmoid(x):
    return 1.0 / (1.0 + jnp.exp(-x))


def _softplus(x):
    return jnp.maximum(x, 0.0) + jnp.log(1.0 + jnp.exp(-jnp.abs(x)))


def _rows_load(ref, n):
    return jnp.concatenate([ref[pl.ds(s, n, stride=ROW_R), :] for s in range(ROW_R)], axis=1)


def _rows_store(ref, val):
    for s in range(ROW_R):
        ref[pl.ds(s, val.shape[0], stride=ROW_R), :] = val[:, s * LANE:(s + 1) * LANE]


def _row_tile(ref, row):
    return ref.at[pl.ds(pl.multiple_of(row * ROW_R, ROW_R), ROW_R), :]


def _tril_mask(n, strict):
    r = lax.broadcasted_iota(jnp.int32, (n, n), 0)
    c = lax.broadcasted_iota(jnp.int32, (n, n), 1)
    return (c < r) if strict else (c <= r)


def _inproj_kernel(x_ref, lnw_ref, lnb_ref, wr_ref, wg_ref, x0_ref, pr_ref, pg_ref):
    x = _layer_norm(x_ref[...], lnw_ref[...], lnb_ref[...])
    x0_ref[...] = x
    xb = x.astype(BF16)
    pr_ref[...] = _dot(xb, wr_ref[...]).astype(BF16)
    pg_ref[...] = _dot(xb, wg_ref[...]).astype(BF16)


def _inproj(x, lnw, lnb, wr, wg):
    t = x.shape[0]
    tm = min(512, t)
    row = lambda i: (i, 0)
    const = lambda i: (0, 0)
    return pl.pallas_call(
        _inproj_kernel,
        grid=(t // tm,),
        in_specs=[pl.BlockSpec((tm, D_MODEL), row),
                  pl.BlockSpec((1, D_MODEL), const), pl.BlockSpec((1, D_MODEL), const),
                  pl.BlockSpec((D_MODEL, RW_WIDTH), const), pl.BlockSpec((D_MODEL, GL_WIDTH), const)],
        out_specs=[pl.BlockSpec((tm, D_MODEL), row), pl.BlockSpec((tm, RW_WIDTH), row),
                   pl.BlockSpec((tm, GL_WIDTH), row)],
        out_shape=[jax.ShapeDtypeStruct((t, D_MODEL), F32), jax.ShapeDtypeStruct((t, RW_WIDTH), BF16),
                   jax.ShapeDtypeStruct((t, GL_WIDTH), BF16)],
        compiler_params=_cparams(1),
        name="inproj_ln",
    )(x, lnw, lnb, wr, wg)


def _head_sum(x, ones_pair):
    n, tiles = x.shape[0], x.shape[1] // LANE
    st = jnp.concatenate([x[:, j * LANE:(j + 1) * LANE] for j in range(tiles)], axis=0)
    hi, lo = _split_bf16(st)
    s = _dot(jnp.concatenate([hi, lo], axis=0), ones_pair)
    s = s[:tiles * n] + s[tiles * n:]
    return jnp.concatenate([s[j * n:(j + 1) * n] for j in range(tiles)], axis=1)


def _rwkv_kernel(has_vres, *refs):
    if has_vres:
        (p_ref, vfirst_ref, mu_ref, w0_ref, wup_ref, a0_ref, aup_ref, gup_ref, kk_ref, ka_ref, rk_ref,
         gnw_ref, gnb_ref, v0_ref, vdown_ref, vup_ref, y_ref, state_ref, prev_ref) = refs
    else:
        (p_ref, mu_ref, w0_ref, wup_ref, a0_ref, aup_ref, gup_ref, kk_ref, ka_ref, rk_ref,
         gnw_ref, gnb_ref, y_ref, vfirst_out_ref, state_ref, prev_ref) = refs

    @pl.when(pl.program_id(1) == 0)
    def _():
        state_ref[...] = jnp.zeros_like(state_ref)
        prev_ref[...] = jnp.zeros_like(prev_ref)

    p = p_ref[0].astype(F32)
    n_tok = p.shape[0]
    row = lax.broadcasted_iota(jnp.int32, (n_tok, 1), 0)
    prev = jnp.where(row == 0, prev_ref[...], pltpu.roll(p, 1, axis=0))
    prev_ref[...] = p[n_tok - 1:n_tok, :]
    xs = p + (prev - p) * mu_ref[...]
    r = xs[:, 0:D_RWKV]
    k = xs[:, D_RWKV:2 * D_RWKV]
    v = xs[:, 2 * D_RWKV:3 * D_RWKV]
    wd = xs[:, RW_WD:RW_AD]
    ad = xs[:, RW_AD:RW_GD]
    gd = xs[:, RW_GD:RW_WIDTH]

    w = w0_ref[...] + _dot(jnp.tanh(wd).astype(BF16), wup_ref[...])
    w = -_softplus(-w) - 0.5
    logw = -jnp.exp(w)
    a = _sigmoid(a0_ref[...] + _dot(ad.astype(BF16), aup_ref[...]))
    g = _dot(_sigmoid(gd).astype(BF16), gup_ref[...])
    if has_vres:
        vmix = _dot(_dot(v.astype(BF16), vdown_ref[...]).astype(BF16), vup_ref[...])
        v = v + (vfirst_ref[0] - v) * _sigmoid(v0_ref[...] + vmix)
    else:
        vfirst_out_ref[0] = v

    hr = lax.broadcasted_iota(jnp.int32, (LANE, LANE), 0) // RWKV_HEAD
    hc = lax.broadcasted_iota(jnp.int32, (LANE, LANE), 1) // RWKV_HEAD
    ones_blk = (hr == hc).astype(BF16)

    kk = k * kk_ref[...]
    kk = kk * lax.rsqrt(jnp.maximum(_head_sum(kk * kk, ones_blk), 1e-24))
    k = k * (1.0 + (a - 1.0) * ka_ref[...])
    bonus = _head_sum(r * k * rk_ref[...], ones_blk)

    n_chunks = n_tok // CHUNK
    n_pairs = RWKV_HEADS // 2
    assert 2 * RWKV_HEAD == LANE and CHUNK == RWKV_HEAD
    rr = lax.broadcasted_iota(jnp.int32, (CHUNK, CHUNK), 0)
    cc = lax.broadcasted_iota(jnp.int32, (CHUNK, CHUNK), 1)
    tril_f = (cc <= rr).astype(F32)
    r_p = lax.broadcasted_iota(jnp.int32, (CHUNK, LANE), 0)
    l_p = lax.broadcasted_iota(jnp.int32, (CHUNK, LANE), 1)
    c_p = jnp.where(l_p >= RWKV_HEAD, l_p - RWKV_HEAD, l_p)
    m_left = (l_p < RWKV_HEAD).astype(F32)
    m_right = (l_p >= RWKV_HEAD).astype(F32)
    m_left_b = m_left.astype(BF16)
    m_right_b = m_right.astype(BF16)
    m_strict = (c_p < r_p).astype(F32)
    m_lower = (c_p <= r_p).astype(F32)
    eye_p = (c_p == r_p).astype(F32)
    m_diag2 = ((r_p // 2) == (c_p // 2)).astype(F32)
    lvl_masks = []
    size = 2
    while size < CHUNK:
        lvl_masks.append((((r_p // (2 * size)) == (c_p // (2 * size))) & ((r_p // size) != (c_p // size))).astype(F32))
        size *= 2

    def blockdiag(xb):
        return jnp.concatenate([xb * m_left_b, xb * m_right_b], axis=0)

    items = []
    g_ends = []
    for c in range(n_chunks):
        sl = slice(c * CHUNK, (c + 1) * CHUNK)
        lw = logw[sl]
        b = jnp.dot(tril_f, lw, precision=HIGHEST, preferred_element_type=F32)
        eb = jnp.exp(b)
        enb = jnp.exp(-b)
        ebx = jnp.exp(b - lw)
        rh = (r[sl] * eb).astype(BF16)
        kh = (k[sl] * enb).astype(BF16)
        ah = (-kk[sl] * ebx).astype(BF16)
        bh = (kk[sl] * a[sl] * enb).astype(BF16)
        vb = v[sl].astype(BF16)
        g_ends.append(eb[CHUNK - 1:CHUNK, :])
        for j in range(n_pairs):
            ls = slice(j * LANE, (j + 1) * LANE)
            items.append(dict(ah=ah[:, ls], rh=rh[:, ls], bh=bh[:, ls], kh=kh[:, ls], v=vb[:, ls]))

    for it in items:
        ar = jnp.concatenate([it["ah"], it["rh"]], axis=0)
        it["sb"] = _dot_nt(ar, blockdiag(it["bh"]))
        it["sk"] = _dot_nt(ar, blockdiag(it["kh"]))
    for it in items:
        it["a_ab"] = it["sb"][:CHUNK] * m_strict
        it["ak"] = (it["sk"][:CHUNK] * m_strict).astype(BF16)
        it["rbk"] = jnp.concatenate([(it["sb"][CHUNK:] * m_lower).astype(BF16),
                                     (it["sk"][CHUNK:] * m_lower).astype(BF16)], axis=1)
        it["d"] = eye_p + it["a_ab"] * m_diag2
        del it["sb"], it["sk"]
    for mk in lvl_masks:
        for it in items:
            it["db"] = it["d"].astype(BF16)
            it["m"] = _dot((it["a_ab"] * mk).astype(BF16), blockdiag(it["db"]))
        for it in items:
            it["d"] = it["d"] + _dot(it["db"], blockdiag(it["m"].astype(BF16)))
    for it in items:
        it["akv"] = _dot(it["ak"], blockdiag(it["v"]))
        it["tinv"] = it["d"].astype(BF16)
    for it in items:
        it["w"] = _dot(it["tinv"], blockdiag(it["ah"]))
        it["ut"] = _dot(it["tinv"], blockdiag(it["akv"].astype(BF16)))

    states = [state_ref[j] for j in range(n_pairs)]
    y_chunks = []
    for c in range(n_chunks):
        its = items[c * n_pairs:(c + 1) * n_pairs]
        for j, it in enumerate(its):
            lhs = jnp.concatenate([it["w"].astype(BF16), it["rh"]], axis=0)
            it["ws"] = _dot_nt(lhs, blockdiag(states[j].astype(BF16)))
        for j, it in enumerate(its):
            it["ub"] = (it["ws"][:CHUNK] + it["ut"]).astype(BF16)
            uv = jnp.concatenate([it["ub"], it["v"]], axis=0)
            upd = _dot_tn(uv, jnp.concatenate([it["bh"], it["kh"]], axis=0))
            ls = slice(j * LANE, (j + 1) * LANE)
            states[j] = (states[j] + upd[:RWKV_HEAD] * m_left + upd[RWKV_HEAD:] * m_right) * g_ends[c][:, ls]
        y_pairs = [it["ws"][CHUNK:] + _dot(it["rbk"], jnp.concatenate([blockdiag(it["ub"]), blockdiag(it["v"])], axis=0))
                   for it in its]
        y_chunks.append(jnp.concatenate(y_pairs, axis=1))
    for j in range(n_pairs):
        state_ref[j] = states[j]
    y = jnp.concatenate(y_chunks, axis=0) if len(y_chunks) > 1 else y_chunks[0]

    inv_n = 1.0 / RWKV_HEAD
    m = _head_sum(y, ones_blk) * inv_n
    yc = y - m
    var = _head_sum(yc * yc, ones_blk) * inv_n
    y = yc * lax.rsqrt(var + RWKV_GN_EPS) * gnw_ref[...] + gnb_ref[...]
    y = y + bonus * v
    y_ref[0] = (y * g).astype(BF16)


def _rwkv_mixer(p_r, vfirst, prm, has_vres):
    bsz, seq, _ = p_r.shape
    n_tok = min(256, seq)
    tok = lambda b, i: (b, i, 0)
    const = lambda b, i: (0, 0)
    vec = pl.BlockSpec((1, D_RWKV), const)
    in_specs = [pl.BlockSpec((1, n_tok, RW_WIDTH), tok)]
    args = [p_r]
    if has_vres:
        in_specs.append(pl.BlockSpec((1, n_tok, D_RWKV), tok))
        args.append(vfirst)
    in_specs += [pl.BlockSpec((1, RW_WIDTH), const), vec, pl.BlockSpec((LANE, D_RWKV), const), vec,
                 pl.BlockSpec((LANE, D_RWKV), const), pl.BlockSpec((2 * LANE, D_RWKV), const),
                 vec, vec, vec, vec, vec]
    args += [prm["mu"], prm["w0"], prm["w_up"], prm["a0"], prm["a_up"], prm["g_up"],
             prm["k_k"], prm["k_a"], prm["r_k"], prm["gn_w"], prm["gn_b"]]
    out_shape = [jax.ShapeDtypeStruct((bsz, seq, D_RWKV), BF16)]
    out_specs = [pl.BlockSpec((1, n_tok, D_RWKV), tok)]
    if has_vres:
        in_specs += [vec, pl.BlockSpec((D_RWKV, LANE), const), pl.BlockSpec((LANE, D_RWKV), const)]
        args += [prm["v0"], prm["v_down"], prm["v_up"]]
    else:
        out_shape.append(jax.ShapeDtypeStruct((bsz, seq, D_RWKV), F32))
        out_specs.append(pl.BlockSpec((1, n_tok, D_RWKV), tok))
    res = pl.pallas_call(
        functools.partial(_rwkv_kernel, has_vres),
        grid=(bsz, seq // n_tok),
        in_specs=in_specs,
        out_specs=out_specs,
        out_shape=out_shape,
        scratch_shapes=[pltpu.VMEM((RWKV_HEADS // 2, RWKV_HEAD, 2 * RWKV_HEAD), F32),
                        pltpu.VMEM((1, RW_WIDTH), F32)],
        compiler_params=_cparams(2),
        name="rwkv7_vres" if has_vres else "rwkv7",
    )(*args)
    if has_vres:
        return res[0], vfirst
    return res[0], res[1]


def _gla_kernel(p_ref, gkup_ref, gkb_ref, nw_ref, y_ref, state_ref):
    @pl.when(pl.program_id(1) == 0)
    def _():
        state_ref[...] = jnp.zeros_like(state_ref)

    p = p_ref[0].astype(F32)
    n_tok = p.shape[0]
    q = p[:, 0:GL_K]
    k = p[:, GL_K:GL_V]
    v = p[:, GL_V:GL_GK]
    gkd = p[:, GL_GK:GL_G]
    g = p[:, GL_G:GL_WIDTH]
    z = _dot(gkd.astype(BF16), gkup_ref[...]) + gkb_ref[...]
    gk = -_softplus(-z) * (1.0 / GLA_GATE_NORMALIZER)

    lower = _tril_mask(CHUNK, strict=False)
    tril_f = lower.astype(F32)
    n_chunks = n_tok // CHUNK
    items = []
    decs = []
    for c in range(n_chunks):
        sl = slice(c * CHUNK, (c + 1) * CHUNK)
        b = jnp.dot(tril_f, gk[sl], precision=HIGHEST, preferred_element_type=F32)
        b_last = b[CHUNK - 1:CHUNK, :]
        q_e = (q[sl] * jnp.exp(b) * (GLA_DK ** -0.5)).astype(BF16)
        k_e = (k[sl] * jnp.exp(-b)).astype(BF16)
        k_end = (k[sl] * jnp.exp(b_last - b)).astype(BF16)
        decs.append(jnp.exp(b_last))
        vb = v[sl].astype(BF16)
        for h in range(GLA_HEADS):
            ks = slice(h * GLA_DK, (h + 1) * GLA_DK)
            vs = slice(h * GLA_DV, (h + 1) * GLA_DV)
            items.append(dict(q=q_e[:, ks], k=k_e[:, ks], kend=k_end[:, ks], v=vb[:, vs]))
    for it in items:
        it["sc"] = jnp.where(lower, _dot_nt(it["q"], it["k"]), 0.0).astype(BF16)
    for it in items:
        it["intra"] = _dot(it["sc"], it["v"])
        it["kv"] = _dot_tn(it["v"], it["kend"])
    states = [state_ref[h] for h in range(GLA_HEADS)]
    o_chunks = []
    for c in range(n_chunks):
        o_heads = []
        for h in range(GLA_HEADS):
            it = items[c * GLA_HEADS + h]
            ks = slice(h * GLA_DK, (h + 1) * GLA_DK)
            o_h = it["intra"] + _dot_nt(it["q"], states[h].astype(BF16))
            states[h] = states[h] * decs[c][:, ks] + it["kv"]
            o_heads.append(o_h * lax.rsqrt(jnp.mean(o_h * o_h, axis=-1, keepdims=True) + RMS_EPS))
        o_chunks.append(jnp.concatenate(o_heads, axis=1))
    for h in range(GLA_HEADS):
        state_ref[h] = states[h]
    o = jnp.concatenate(o_chunks, axis=0) if len(o_chunks) > 1 else o_chunks[0]
    silu_g = g * _sigmoid(g)
    y_ref[0] = (o * nw_ref[...] * silu_g).astype(BF16)


def _gla_mixer(p_g, prm):
    bsz, seq, _ = p_g.shape
    n_tok = min(256, seq)
    tok = lambda b, i: (b, i, 0)
    const = lambda b, i: (0, 0)
    return pl.pallas_call(
        _gla_kernel,
        grid=(bsz, seq // n_tok),
        in_specs=[pl.BlockSpec((1, n_tok, GL_WIDTH), tok),
                  pl.BlockSpec((LANE, GL_K), const), pl.BlockSpec((1, GL_K), const),
                  pl.BlockSpec((1, D_GLA), const)],
        out_specs=pl.BlockSpec((1, n_tok, D_GLA), tok),
        out_shape=jax.ShapeDtypeStruct((bsz, seq, D_GLA), BF16),
        scratch_shapes=[pltpu.VMEM((GLA_HEADS, GLA_DV, GLA_DK), F32)],
        compiler_params=_cparams(2),
        name="gla",
    )(p_g, prm["gk_up"], prm["gk_b"], prm["norm_w"])


def _outproj_router_kernel(yr_ref, yg_ref, x_ref, wo_ref, lnw_ref, lnb_ref, rw_ref, rb_ref,
                           x1_ref, idx_ref, gate_ref, rank_ref, cnt_ref, carry_ref):
    @pl.when(pl.program_id(0) == 0)
    def _():
        carry_ref[...] = jnp.zeros_like(carry_ref)

    mix = _dot(yr_ref[...], wo_ref[0:D_RWKV, :]) + _dot(yg_ref[...], wo_ref[D_RWKV:, :])
    x1 = _layer_norm(DEEPNORM_ALPHA * x_ref[...] + mix, lnw_ref[...], lnb_ref[...])
    _rows_store(x1_ref, x1)
    tm = x1.shape[0]

    xh, xl = _split_bf16(x1)
    rw = rw_ref[...]
    wh, wl = _split_bf16(rw)
    logits = _dot_nt(wh, xh) + _dot_nt(wh, xl) + _dot_nt(wl, xh) + rb_ref[...]

    e_iota = lax.broadcasted_iota(jnp.int32, (N_EXPERTS, tm), 0)
    work = logits
    vals, idxs = [], []
    member = jnp.zeros((N_EXPERTS, tm), F32)
    for _ in range(TOP_K):
        mx = jnp.max(work, axis=0, keepdims=True)
        ix = jnp.min(jnp.where(work == mx, e_iota, N_EXPERTS), axis=0, keepdims=True)
        sel = e_iota == ix
        work = jnp.where(sel, -jnp.inf, work)
        member = jnp.where(sel, 1.0, member)
        vals.append(mx)
        idxs.append(ix)
    exps = [jnp.exp(vv - vals[0]) for vv in vals]
    inv_den = 1.0 / (exps[0] + exps[1] + exps[2] + exps[3])

    tr = lax.broadcasted_iota(jnp.int32, (tm, tm), 0)
    tc = lax.broadcasted_iota(jnp.int32, (tm, tm), 1)
    before = (tr < tc).astype(BF16)
    cex = _dot(member.astype(BF16), before) + carry_ref[...][:, 0:1]
    for kq in range(TOP_K):
        sel = e_iota == idxs[kq]
        idx_ref[kq:kq + 1, :] = idxs[kq]
        gate_ref[kq:kq + 1, :] = exps[kq] * inv_den
        rank_ref[kq:kq + 1, :] = jnp.sum(jnp.where(sel, cex, 0.0), axis=0, keepdims=True).astype(jnp.int32)
    carry_ref[...] = carry_ref[...] + jnp.sum(member, axis=1, keepdims=True)
    cnt_ref[...] = carry_ref[...].astype(jnp.int32)


def _outproj_router(yr, yg, x, wo, lnw, lnb, rw_t, rb):
    t = x.shape[0]
    tm = min(512, t)
    row = lambda i: (i, 0)
    col = lambda i: (0, i)
    const = lambda i: (0, 0)
    return pl.pallas_call(
        _outproj_router_kernel,
        grid=(t // tm,),
        in_specs=[pl.BlockSpec((tm, D_RWKV), row), pl.BlockSpec((tm, D_GLA), row),
                  pl.BlockSpec((tm, D_MODEL), row), pl.BlockSpec((D_MODEL, D_MODEL), const),
                  pl.BlockSpec((1, D_MODEL), const), pl.BlockSpec((1, D_MODEL), const),
                  pl.BlockSpec((N_EXPERTS, D_MODEL), const), pl.BlockSpec((N_EXPERTS, 1), const)],
        out_specs=[pl.BlockSpec((tm * ROW_R, LANE), row),
                   pl.BlockSpec((TOP_K, tm), col), pl.BlockSpec((TOP_K, tm), col),
                   pl.BlockSpec((TOP_K, tm), col), pl.BlockSpec((N_EXPERTS, LANE), const)],
        out_shape=[jax.ShapeDtypeStruct((t * ROW_R, LANE), F32),
                   jax.ShapeDtypeStruct((TOP_K, t), jnp.int32), jax.ShapeDtypeStruct((TOP_K, t), F32),
                   jax.ShapeDtypeStruct((TOP_K, t), jnp.int32),
                   jax.ShapeDtypeStruct((N_EXPERTS, LANE), jnp.int32)],
        scratch_shapes=[pltpu.VMEM((N_EXPERTS, LANE), F32)],
        compiler_params=_cparams(1),
        name="outproj_router",
    )(yr, yg, x, wo, lnw, lnb, rw_t, rb)


SEG_BIG = 64
SEG_MID = 8


def _dispatch_kernel(pos_ref, seg_ref, gaps_ref, x_ref, xs_ref, stage_ref, zero_ref, sem, zsem):
    te = x_ref.shape[0] // ROW_R
    i = pl.program_id(0)
    n_tiles = pl.num_programs(0)
    slot = i % 2
    base = i * (TOP_K * te)

    def drain(sl):
        for kq in range(TOP_K):
            pltpu.make_async_copy(x_ref, x_ref, sem.at[sl]).wait()

    @pl.when(pl.program_id(0) == 0)
    def _():
        zero_ref[...] = jnp.zeros_like(zero_ref)

        def zero_copy(d):
            return pltpu.make_async_copy(zero_ref, _row_tile(xs_ref, d), zsem)

        for g in range(N_EXPERTS + 1):
            lo = gaps_ref[2 * g]
            hi = gaps_ref[2 * g + 1]
            lax.fori_loop(lo, hi, lambda d, c: (zero_copy(d).start(), c)[1], 0)
        blk = pl.ds(0, MOE_BLOCK * ROW_R)
        for g in range(N_EXPERTS):
            pltpu.make_async_copy(xs_ref.at[blk, :], xs_ref.at[blk, :], zsem).wait()

    @pl.when(i >= 2)
    def _():
        drain(slot)

    def place(r, carry):
        row = x_ref[pl.ds(pl.multiple_of(r * ROW_R, ROW_R), ROW_R), :]
        for kq in range(TOP_K):
            p = pos_ref[base + kq * te + r]
            stage_ref[slot, pl.ds(pl.multiple_of(p * ROW_R, ROW_R), ROW_R), :] = row
        return carry

    lax.fori_loop(0, te, place, 0, unroll=4)

    for e in range(N_EXPERTS):
        o = (i * N_EXPERTS + e) * 3
        s, n, g = seg_ref[o], seg_ref[o + 1], seg_ref[o + 2]

        def copy(off, rows, s=s, g=g):
            src = stage_ref.at[slot, pl.ds(pl.multiple_of((s + off) * ROW_R, ROW_R), rows * ROW_R), :]
            dst = xs_ref.at[pl.ds(pl.multiple_of((g + off) * ROW_R, ROW_R), rows * ROW_R), :]
            pltpu.make_async_copy(src, dst, sem.at[slot]).start()

        n_big = lax.shift_right_logical(n, 6)
        n_mid = lax.shift_right_logical(n, 3) & 7
        n_one = n & 7
        off_mid = n_big * SEG_BIG
        off_one = off_mid + n_mid * SEG_MID
        lax.fori_loop(0, n_big, lambda q, c, copy=copy: (copy(q * SEG_BIG, SEG_BIG), c)[1], 0)
        lax.fori_loop(0, n_mid, lambda q, c, copy=copy, off_mid=off_mid: (copy(off_mid + q * SEG_MID, SEG_MID), c)[1], 0)
        lax.fori_loop(0, n_one, lambda q, c, copy=copy, off_one=off_one: (copy(off_one + q, 1), c)[1], 0)

    @pl.when(i == n_tiles - 1)
    def _():
        @pl.when(i >= 1)
        def _():
            drain(1 - slot)
        drain(slot)


def _dispatch(x1, pos_tiles, seg, gaps, m_pad, te):
    t = x1.shape[0] // ROW_R
    return pl.pallas_call(
        _dispatch_kernel,
        grid_spec=pltpu.PrefetchScalarGridSpec(
            num_scalar_prefetch=3,
            grid=(t // te,),
            in_specs=[pl.BlockSpec((te * ROW_R, LANE), lambda i, p, s, g: (i, 0))],
            out_specs=pl.BlockSpec(memory_space=pl.ANY),
            scratch_shapes=[pltpu.VMEM((2, TOP_K * te * ROW_R, LANE), F32), pltpu.VMEM((ROW_R, LANE), F32),
                            pltpu.SemaphoreType.DMA((2,)), pltpu.SemaphoreType.DMA(())],
        ),
        out_shape=jax.ShapeDtypeStruct((m_pad * ROW_R, LANE), F32),
        compiler_params=_cparams(1),
        name="moe_dispatch",
    )(pos_tiles, seg, gaps, x1)


def _expert_kernel(be_ref, nv_ref, xs_ref, w1_ref, b1_ref, w2_ref, b2_ref, ys_ref, act_ref, w1b_ref, w2b_ref):
    j = pl.program_id(0)
    n_valid = nv_ref[j]

    @pl.when((j == 0) | (be_ref[j] != be_ref[jnp.maximum(j - 1, 0)]))
    def _():
        rows = 128
        for c in range(D_MODEL // rows):
            rs = slice(c * rows, (c + 1) * rows)
            w1b_ref[rs, :] = w1_ref[0, 0, rs, :].astype(BF16)
            w2b_ref[rs, :] = w2_ref[0, 0, rs, :].astype(BF16)

    @pl.when(n_valid == 0)
    def _():
        ys_ref[...] = jnp.zeros_like(ys_ref)

    @pl.when(n_valid > 0)
    def _():
        xb = _rows_load(xs_ref, MOE_BLOCK).astype(BF16)
        n_chunk = 256
        for c in range(D_FF // n_chunk):
            cs = slice(c * n_chunk, (c + 1) * n_chunk)
            us = slice(D_FF + c * n_chunk, D_FF + (c + 1) * n_chunk)
            gt = _dot(xb, w1b_ref[:, cs]) + b1_ref[0, 0, :, cs]
            up = _dot(xb, w1b_ref[:, us]) + b1_ref[0, 0, :, us]
            gt = jnp.minimum(gt, SWIGLU_LIMIT)
            up = jnp.clip(up, -SWIGLU_LIMIT, SWIGLU_LIMIT)
            act_ref[:, cs] = ((up + 1.0) * gt * _sigmoid(SWIGLU_ALPHA * gt)).astype(BF16)
        _rows_store(ys_ref, _dot(act_ref[...], w2b_ref[...]) + b2_ref[0, 0])


def _experts(xs, block_e, n_valid, layer, w1, b1, w2, b2):
    m_pad = xs.shape[0] // ROW_R
    n_blocks = m_pad // MOE_BLOCK
    wmap = lambda j, be, nv: (layer, be[j], 0, 0)
    return pl.pallas_call(
        _expert_kernel,
        grid_spec=pltpu.PrefetchScalarGridSpec(
            num_scalar_prefetch=2,
            grid=(n_blocks,),
            in_specs=[pl.BlockSpec((MOE_BLOCK * ROW_R, LANE), lambda j, be, nv: (j, 0)),
                      pl.BlockSpec((1, 1, D_MODEL, 2 * D_FF), wmap),
                      pl.BlockSpec((1, 1, 1, 2 * D_FF), wmap),
                      pl.BlockSpec((1, 1, D_FF, D_MODEL), wmap),
                      pl.BlockSpec((1, 1, 1, D_MODEL), wmap)],
            out_specs=pl.BlockSpec((MOE_BLOCK * ROW_R, LANE), lambda j, be, nv: (j, 0)),
            scratch_shapes=[pltpu.VMEM((MOE_BLOCK, D_FF), BF16),
                            pltpu.VMEM((D_MODEL, 2 * D_FF), BF16), pltpu.VMEM((D_FF, D_MODEL), BF16)],
        ),
        out_shape=jax.ShapeDtypeStruct((m_pad * ROW_R, LANE), F32),
        compiler_params=_cparams(1),
        name="moe_experts",
    )(block_e, n_valid, xs, w1, b1, w2, b2)


def _combine_kernel(with_inproj, dest_ref, ys_ref, x1_ref, gate_ref, lnw_ref, lnb_ref, *refs):
    if with_inproj:
        wr_ref, wg_ref, out_ref, pr_ref, pg_ref, buf_ref, sem = refs
    else:
        out_ref, buf_ref, sem = refs
    te = x1_ref.shape[0] // ROW_R
    i = pl.program_id(0)
    n_tiles = pl.num_programs(0)

    def gather(tile, slot):
        base = tile * (TOP_K * te)

        def issue(r, carry):
            for kq in range(TOP_K):
                d = dest_ref[base + kq * te + r]
                pltpu.make_async_copy(_row_tile(ys_ref, d), _row_tile(buf_ref.at[slot, kq], r),
                                      sem.at[slot]).start(priority=kq % 2)
            return carry

        lax.fori_loop(0, te, issue, 0, unroll=8)

    @pl.when(i == 0)
    def _():
        gather(0, 0)

    for slot in range(2):
        @pl.when((i + 1 < n_tiles) & ((i + 1) % 2 == slot))
        def _():
            gather(i + 1, slot)

    for slot in range(2):
        @pl.when(i % 2 == slot)
        def _():
            for kq in range(TOP_K):
                pltpu.make_async_copy(buf_ref.at[slot, kq], buf_ref.at[slot, kq], sem.at[slot]).wait()
            gates = gate_ref[...]
            z = DEEPNORM_ALPHA * _rows_load(x1_ref, te)
            for kq in range(TOP_K):
                z = z + _rows_load(buf_ref.at[slot, kq], te) * gates[:, kq:kq + 1]
            xn = _layer_norm(z, lnw_ref[...], lnb_ref[...])
            out_ref[...] = xn
            if with_inproj:
                xb = xn.astype(BF16)
                pr_ref[...] = _dot(xb, wr_ref[...]).astype(BF16)
                pg_ref[...] = _dot(xb, wg_ref[...]).astype(BF16)


def _combine(ys, dest_tiles, x1, gates_t, lnw, lnb, te, next_inproj=None):
    t = x1.shape[0] // ROW_R
    row = lambda i, d: (i, 0)
    const = lambda i, d: (0, 0)
    in_specs = [pl.BlockSpec(memory_space=pl.ANY),
                pl.BlockSpec((te * ROW_R, LANE), row),
                pl.BlockSpec((te, TOP_K), row),
                pl.BlockSpec((1, D_MODEL), const),
                pl.BlockSpec((1, D_MODEL), const)]
    out_specs = [pl.BlockSpec((te, D_MODEL), row)]
    out_shape = [jax.ShapeDtypeStruct((t, D_MODEL), F32)]
    args = [dest_tiles, ys, x1, gates_t, lnw, lnb]
    if next_inproj is not None:
        in_specs += [pl.BlockSpec((D_MODEL, RW_WIDTH), const), pl.BlockSpec((D_MODEL, GL_WIDTH), const)]
        out_specs += [pl.BlockSpec((te, RW_WIDTH), row), pl.BlockSpec((te, GL_WIDTH), row)]
        out_shape += [jax.ShapeDtypeStruct((t, RW_WIDTH), BF16), jax.ShapeDtypeStruct((t, GL_WIDTH), BF16)]
        args += list(next_inproj)
    return pl.pallas_call(
        functools.partial(_combine_kernel, next_inproj is not None),
        grid_spec=pltpu.PrefetchScalarGridSpec(
            num_scalar_prefetch=1,
            grid=(t // te,),
            in_specs=in_specs,
            out_specs=out_specs,
            scratch_shapes=[pltpu.VMEM((2, TOP_K, te * ROW_R, LANE), F32), pltpu.SemaphoreType.DMA((2,))],
        ),
        out_shape=out_shape,
        compiler_params=_cparams(1),
        name="moe_combine_inproj" if next_inproj is not None else "moe_combine",
    )(*args)


def _moe(x1, idx, gate, rank, counts, layer, w1, b1, w2, b2, lnw, lnb, next_inproj=None):
    t = x1.shape[0] // ROW_R
    te = min(256, t)
    assert (t * TOP_K) % MOE_BLOCK == 0
    n_blocks = (t * TOP_K) // MOE_BLOCK + N_EXPERTS
    m_pad = n_blocks * MOE_BLOCK
    padded = ((counts + MOE_BLOCK - 1) // MOE_BLOCK) * MOE_BLOCK
    pend = jnp.cumsum(padded)
    pstart = pend - padded
    dest = rank
    for e in range(N_EXPERTS):
        dest = dest + jnp.where(idx == e, pstart[e], 0)
    dest_tiles = dest.reshape(TOP_K, t // te, te).transpose(1, 0, 2).reshape(-1)
    blk_start = jnp.arange(n_blocks, dtype=jnp.int32) * MOE_BLOCK
    block_e = jnp.minimum(jnp.sum(pend[None, :] <= blk_start[:, None], axis=1), N_EXPERTS - 1).astype(jnp.int32)
    is_e = block_e[:, None] == jnp.arange(N_EXPERTS, dtype=jnp.int32)[None, :]
    blk_end = jnp.sum(jnp.where(is_e, (pstart + counts)[None, :], 0), axis=1)
    n_valid = jnp.clip(blk_end - blk_start, 0, MOE_BLOCK).astype(jnp.int32)

    gap_lo = jnp.concatenate([pstart + counts, pend[-1:]])
    gap_hi = jnp.concatenate([pend, jnp.full((1,), m_pad, pend.dtype)])
    gaps = jnp.stack([gap_lo, gap_hi], axis=1).reshape(-1).astype(jnp.int32)

    td = min(512, t)
    n_td = t // td
    idx_t = idx.reshape(TOP_K, n_td, td)
    e_ids = jnp.arange(N_EXPERTS, dtype=jnp.int32)
    tcnt = jnp.sum((idx_t[..., None] == e_ids).astype(jnp.int32), axis=(0, 2))
    cum = jnp.cumsum(tcnt, axis=0) - tcnt
    toff = jnp.cumsum(tcnt, axis=1) - tcnt
    seg = jnp.stack([toff, tcnt, pstart[None, :] + cum], axis=-1).reshape(-1).astype(jnp.int32)
    delta = toff - cum
    pos = rank.reshape(TOP_K, n_td, td)
    for e in range(N_EXPERTS):
        pos = pos + jnp.where(idx_t == e, delta[None, :, e, None], 0)
    pos_tiles = pos.transpose(1, 0, 2).reshape(-1).astype(jnp.int32)

    xs = _dispatch(x1, pos_tiles, seg, gaps, m_pad, td)
    ys = _experts(xs, block_e, n_valid, layer, w1, b1, w2, b2)
    return _combine(ys, dest_tiles, x1, gate.T, lnw, lnb, te, next_inproj)


def _pad_cols(w, width):
    return jnp.pad(w, ((0, 0), (0, width - w.shape[1])))


def _pad_rows(w, height):
    return jnp.pad(w, ((0, height - w.shape[0]), (0, 0)))


def _pack_inproj(w_in_l, mu_l):
    d3 = 3 * D_RWKV
    o_ad = d3 + W_LORA
    o_gd = o_ad + A_LORA
    o_gla = o_gd + G_LORA
    wr = jnp.concatenate([w_in_l[:, :d3],
                          _pad_cols(w_in_l[:, d3:o_ad], LANE),
                          _pad_cols(w_in_l[:, o_ad:o_gd], LANE),
                          _pad_cols(w_in_l[:, o_gd:o_gla], 2 * LANE)], axis=1)
    mu = mu_l[None, :]
    mu_p = jnp.concatenate([mu[:, :d3], _pad_cols(mu[:, d3:o_ad], LANE), _pad_cols(mu[:, o_ad:o_gd], LANE),
                            _pad_cols(mu[:, o_gd:o_gla], 2 * LANE)], axis=1)
    g = w_in_l[:, o_gla:]
    wg = jnp.concatenate([g[:, :GL_GK], _pad_cols(g[:, GL_GK:GL_GK + GLA_GATE_RANK], LANE),
                          g[:, GL_GK + GLA_GATE_RANK:]], axis=1)
    return wr.astype(BF16), wg.astype(BF16), mu_p


def kernel(x, ln_in_w, ln_in_b, w_in, rwkv_mu, rwkv_w0, rwkv_w_up, rwkv_a0, rwkv_a_up, rwkv_g_up, rwkv_k_k, rwkv_k_a, rwkv_r_k, rwkv_gn_w, rwkv_gn_b, rwkv_v0, rwkv_v_down, rwkv_v_up, gla_gk_up, gla_gk_b, gla_norm_w, w_out, ln1_w, ln1_b, router_w, router_b, exp_w1, exp_b1, exp_w2, exp_b2, ln2_w, ln2_b):
    bsz, seq, d = x.shape
    t = bsz * seq
    xf = x.reshape(t, d)
    vfirst = None
    packed = [_pack_inproj(w_in[l], rwkv_mu[l]) for l in range(DEPTH)]
    xf, p_r, p_g = _inproj(xf, ln_in_w[None, :], ln_in_b[None, :], packed[0][0], packed[0][1])
    for l in range(DEPTH):
        mu_p = packed[l][2]
        rprm = {
            "mu": mu_p, "w0": rwkv_w0[l][None, :], "w_up": _pad_rows(rwkv_w_up[l], LANE).astype(BF16),
            "a0": rwkv_a0[l][None, :], "a_up": _pad_rows(rwkv_a_up[l], LANE).astype(BF16),
            "g_up": _pad_rows(rwkv_g_up[l], 2 * LANE).astype(BF16),
            "k_k": rwkv_k_k[l][None, :], "k_a": rwkv_k_a[l][None, :], "r_k": rwkv_r_k[l][None, :],
            "gn_w": rwkv_gn_w[l][None, :], "gn_b": rwkv_gn_b[l][None, :],
        }
        if l > 0:
            rprm["v0"] = rwkv_v0[l - 1][None, :]
            rprm["v_down"] = _pad_cols(rwkv_v_down[l - 1], LANE).astype(BF16)
            rprm["v_up"] = _pad_rows(rwkv_v_up[l - 1], LANE).astype(BF16)
        y_r, vfirst = _rwkv_mixer(p_r.reshape(bsz, seq, RW_WIDTH), vfirst, rprm, has_vres=l > 0)
        gprm = {"gk_up": _pad_rows(gla_gk_up[l], LANE).astype(BF16), "gk_b": gla_gk_b[l][None, :],
                "norm_w": jnp.tile(gla_norm_w[l], GLA_HEADS)[None, :]}
        y_g = _gla_mixer(p_g.reshape(bsz, seq, GL_WIDTH), gprm)
        x1, idx, gate, rank, cnt = _outproj_router(
            y_r.reshape(t, D_RWKV), y_g.reshape(t, D_GLA), xf, w_out[l].astype(BF16),
            ln1_w[l][None, :], ln1_b[l][None, :], router_w[l].T, router_b[l][:, None])
        nxt = packed[l + 1][:2] if l + 1 < DEPTH else None
        res = _moe(x1, idx, gate, rank, cnt[:, 0], l, exp_w1, exp_b1[:, :, None, :],
                   exp_w2, exp_b2[:, :, None, :], ln2_w[l][None, :], ln2_b[l][None, :], nxt)
        if nxt is not None:
            xf, p_r, p_g = res
        else:
            xf = res[0]
    return xf.reshape(bsz, seq, d)
```

```python
import functools

import jax
import jax.numpy as jnp
from jax import lax
from jax.experimental import pallas as pl
from jax.experimental.pallas import tpu as pltpu

F32 = jnp.float32
BF16 = jnp.bfloat16
HIGHEST = lax.Precision.HIGHEST

D_MODEL = 1024
DEPTH = 2
CHUNK = 64
RWKV_HEAD = 64
D_RWKV = 512
RWKV_HEADS = 8
W_LORA = 64
A_LORA = 64
V_LORA = 32
G_LORA = 160
RWKV_GN_EPS = 64e-5
D_GLA = 512
GLA_HEADS = 4
GLA_DV = 128
GLA_DK = 64
GLA_GATE_RANK = 16
GLA_GATE_NORMALIZER = 16.0
RMS_EPS = 1e-6
N_EXPERTS = 32
TOP_K = 4
D_FF = 1024
SWIGLU_LIMIT = 7.0
SWIGLU_ALPHA = 1.702
MOE_BLOCK = 512
DEEPNORM_ALPHA = (2 * DEPTH) ** 0.25
LN_EPS = 1e-5

LANE = 128
SUBLANE = 8
D_HALF = D_MODEL // 2
ROW_R = D_HALF // LANE
assert 2 * ROW_R == SUBLANE
U32 = jnp.uint32
RW_WD = 3 * D_RWKV
RW_AD = RW_WD + LANE
RW_GD = RW_AD + LANE
RW_WIDTH = RW_GD + 2 * LANE
GL_K = GLA_HEADS * GLA_DK
GL_V = 2 * GL_K
GL_GK = GL_V + D_GLA
GL_G = GL_GK + LANE
GL_WIDTH = GL_G + D_GLA

VMEM_LIMIT = 56 * 1024 * 1024


def _cparams(n_axes):
    return pltpu.CompilerParams(dimension_semantics=("arbitrary",) * n_axes,
                                vmem_limit_bytes=VMEM_LIMIT)


def _dot(a, b):
    return jnp.dot(a, b, preferred_element_type=F32)


def _dot_nt(a, b):
    return lax.dot_general(a, b, (((1,), (1,)), ((), ())), preferred_element_type=F32)


def _dot_tn(a, b):
    return lax.dot_general(a, b, (((0,), (0,)), ((), ())), preferred_element_type=F32)


def _split_bf16(x):
    hi = x.astype(BF16)
    lo = (x - hi.astype(F32)).astype(BF16)
    return hi, lo


def _layer_norm(z, w, b):
    mu = jnp.mean(z, axis=-1, keepdims=True)
    zc = z - mu
    var = jnp.mean(zc * zc, axis=-1, keepdims=True)
    return zc * lax.rsqrt(var + LN_EPS) * w + b


def _sigmoid(x):
    return 1.0 / (1.0 + jnp.exp(-x))


def _softplus(x):
    return jnp.maximum(x, 0.0) + jnp.log(1.0 + jnp.exp(-jnp.abs(x)))


def _rows_load(ref, n):
    w = jnp.concatenate([ref[pl.ds(s, n, stride=ROW_R), :] for s in range(ROW_R)], axis=1)
    left = pltpu.bitcast(lax.shift_left(w, U32(16)), F32)
    right = pltpu.bitcast(w & U32(0xFFFF0000), F32)
    return jnp.concatenate([left, right], axis=1)


def _rows_store(ref, val):
    bits = pltpu.bitcast(val.astype(BF16).astype(F32), U32)
    w = lax.shift_right_logical(bits[:, :D_HALF], U32(16)) | bits[:, D_HALF:]
    for s in range(ROW_R):
        ref[pl.ds(s, val.shape[0], stride=ROW_R), :] = w[:, s * LANE:(s + 1) * LANE]


def _row_tile(ref, row):
    return ref.at[pl.ds(pl.multiple_of(row * ROW_R, ROW_R), ROW_R), :]


def _tril_mask(n, strict):
    r = lax.broadcasted_iota(jnp.int32, (n, n), 0)
    c = lax.broadcasted_iota(jnp.int32, (n, n), 1)
    return (c < r) if strict else (c <= r)


def _inproj_kernel(x_ref, lnw_ref, lnb_ref, wr_ref, wg_ref, x0_ref, pr_ref, pg_ref):
    x = _layer_norm(x_ref[...], lnw_ref[...], lnb_ref[...])
    x0_ref[...] = x
    xb = x.astype(BF16)
    pr_ref[...] = _dot(xb, wr_ref[...]).astype(BF16)
    pg_ref[...] = _dot(xb, wg_ref[...]).astype(BF16)


def _inproj(x, lnw, lnb, wr, wg):
    t = x.shape[0]
    tm = min(512, t)
    row = lambda i: (i, 0)
    const = lambda i: (0, 0)
    return pl.pallas_call(
        _inproj_kernel,
        grid=(t // tm,),
        in_specs=[pl.BlockSpec((tm, D_MODEL), row),
                  pl.BlockSpec((1, D_MODEL), const), pl.BlockSpec((1, D_MODEL), const),
                  pl.BlockSpec((D_MODEL, RW_WIDTH), const), pl.BlockSpec((D_MODEL, GL_WIDTH), const)],
        out_specs=[pl.BlockSpec((tm, D_MODEL), row), pl.BlockSpec((tm, RW_WIDTH), row),
                   pl.BlockSpec((tm, GL_WIDTH), row)],
        out_shape=[jax.ShapeDtypeStruct((t, D_MODEL), F32), jax.ShapeDtypeStruct((t, RW_WIDTH), BF16),
                   jax.ShapeDtypeStruct((t, GL_WIDTH), BF16)],
        compiler_params=_cparams(1),
        name="inproj_ln",
    )(x, lnw, lnb, wr, wg)


def _head_sum(x, ones_pair):
    n, tiles = x.shape[0], x.shape[1] // LANE
    st = jnp.concatenate([x[:, j * LANE:(j + 1) * LANE] for j in range(tiles)], axis=0)
    hi, lo = _split_bf16(st)
    s = _dot(jnp.concatenate([hi, lo], axis=0), ones_pair)
    s = s[:tiles * n] + s[tiles * n:]
    return jnp.concatenate([s[j * n:(j + 1) * n] for j in range(tiles)], axis=1)


def _rwkv_kernel(has_vres, *refs):
    if has_vres:
        (p_ref, vfirst_ref, mu_ref, w0_ref, wup_ref, a0_ref, aup_ref, gup_ref, kk_ref, ka_ref, rk_ref,
         gnw_ref, gnb_ref, v0_ref, vdown_ref, vup_ref, y_ref, state_ref, prev_ref) = refs
    else:
        (p_ref, mu_ref, w0_ref, wup_ref, a0_ref, aup_ref, gup_ref, kk_ref, ka_ref, rk_ref,
         gnw_ref, gnb_ref, y_ref, vfirst_out_ref, state_ref, prev_ref) = refs

    @pl.when(pl.program_id(1) == 0)
    def _():
        state_ref[...] = jnp.zeros_like(state_ref)
        prev_ref[...] = jnp.zeros_like(prev_ref)

    p = p_ref[0].astype(F32)
    n_tok = p.shape[0]
    row = lax.broadcasted_iota(jnp.int32, (n_tok, 1), 0)
    prev = jnp.where(row == 0, prev_ref[...], pltpu.roll(p, 1, axis=0))
    prev_ref[...] = p[n_tok - 1:n_tok, :]
    xs = p + (prev - p) * mu_ref[...]
    r = xs[:, 0:D_RWKV]
    k = xs[:, D_RWKV:2 * D_RWKV]
    v = xs[:, 2 * D_RWKV:3 * D_RWKV]
    wd = xs[:, RW_WD:RW_AD]
    ad = xs[:, RW_AD:RW_GD]
    gd = xs[:, RW_GD:RW_WIDTH]

    w = w0_ref[...] + _dot(jnp.tanh(wd).astype(BF16), wup_ref[...])
    w = -_softplus(-w) - 0.5
    logw = -jnp.exp(w)
    a = _sigmoid(a0_ref[...] + _dot(ad.astype(BF16), aup_ref[...]))
    g = _dot(_sigmoid(gd).astype(BF16), gup_ref[...])
    if has_vres:
        vmix = _dot(_dot(v.astype(BF16), vdown_ref[...]).astype(BF16), vup_ref[...])
        v = v + (vfirst_ref[0] - v) * _sigmoid(v0_ref[...] + vmix)
    else:
        vfirst_out_ref[0] = v

    hr = lax.broadcasted_iota(jnp.int32, (LANE, LANE), 0) // RWKV_HEAD
    hc = lax.broadcasted_iota(jnp.int32, (LANE, LANE), 1) // RWKV_HEAD
    ones_blk = (hr == hc).astype(BF16)

    kk = k * kk_ref[...]
    kk = kk * lax.rsqrt(jnp.maximum(_head_sum(kk * kk, ones_blk), 1e-24))
    k = k * (1.0 + (a - 1.0) * ka_ref[...])
    bonus = _head_sum(r * k * rk_ref[...], ones_blk)

    n_chunks = n_tok // CHUNK
    n_pairs = RWKV_HEADS // 2
    assert 2 * RWKV_HEAD == LANE and CHUNK == RWKV_HEAD
    rr = lax.broadcasted_iota(jnp.int32, (CHUNK, CHUNK), 0)
    cc = lax.broadcasted_iota(jnp.int32, (CHUNK, CHUNK), 1)
    tril_f = (cc <= rr).astype(F32)
    r_p = lax.broadcasted_iota(jnp.int32, (CHUNK, LANE), 0)
    l_p = lax.broadcasted_iota(jnp.int32, (CHUNK, LANE), 1)
    c_p = jnp.where(l_p >= RWKV_HEAD, l_p - RWKV_HEAD, l_p)
    m_left = (l_p < RWKV_HEAD).astype(F32)
    m_right = (l_p >= RWKV_HEAD).astype(F32)
    m_left_b = m_left.astype(BF16)
    m_right_b = m_right.astype(BF16)
    m_strict = (c_p < r_p).astype(F32)
    m_lower = (c_p <= r_p).astype(F32)
    eye_p = (c_p == r_p).astype(F32)
    m_diag2 = ((r_p // 2) == (c_p // 2)).astype(F32)
    lvl_masks = []
    size = 2
    while size < CHUNK:
        lvl_masks.append((((r_p // (2 * size)) == (c_p // (2 * size))) & ((r_p // size) != (c_p // size))).astype(F32))
        size *= 2

    def blockdiag(xb):
        return jnp.concatenate([xb * m_left_b, xb * m_right_b], axis=0)

    items = []
    g_ends = []
    for c in range(n_chunks):
        sl = slice(c * CHUNK, (c + 1) * CHUNK)
        lw = logw[sl]
        b = jnp.dot(tril_f, lw, precision=HIGHEST, preferred_element_type=F32)
        eb = jnp.exp(b)
        enb = jnp.exp(-b)
        ebx = jnp.exp(b - lw)
        rh = (r[sl] * eb).astype(BF16)
        kh = (k[sl] * enb).astype(BF16)
        ah = (-kk[sl] * ebx).astype(BF16)
        bh = (kk[sl] * a[sl] * enb).astype(BF16)
        vb = v[sl].astype(BF16)
        g_ends.append(eb[CHUNK - 1:CHUNK, :])
        for j in range(n_pairs):
            ls = slice(j * LANE, (j + 1) * LANE)
            items.append(dict(ah=ah[:, ls], rh=rh[:, ls], bh=bh[:, ls], kh=kh[:, ls], v=vb[:, ls]))

    for it in items:
        ar = jnp.concatenate([it["ah"], it["rh"]], axis=0)
        it["sb"] = _dot_nt(ar, blockdiag(it["bh"]))
        it["sk"] = _dot_nt(ar, blockdiag(it["kh"]))
    for it in items:
        it["a_ab"] = it["sb"][:CHUNK] * m_strict
        it["ak"] = (it["sk"][:CHUNK] * m_strict).astype(BF16)
        it["rbk"] = jnp.concatenate([(it["sb"][CHUNK:] * m_lower).astype(BF16),
                                     (it["sk"][CHUNK:] * m_lower).astype(BF16)], axis=1)
        it["d"] = eye_p + it["a_ab"] * m_diag2
        del it["sb"], it["sk"]
    for mk in lvl_masks:
        for it in items:
            it["db"] = it["d"].astype(BF16)
            it["m"] = _dot((it["a_ab"] * mk).astype(BF16), blockdiag(it["db"]))
        for it in items:
            it["d"] = it["d"] + _dot(it["db"], blockdiag(it["m"].astype(BF16)))
    for it in items:
        it["akv"] = _dot(it["ak"], blockdiag(it["v"]))
        it["tinv"] = it["d"].astype(BF16)
    for it in items:
        it["w"] = _dot(it["tinv"], blockdiag(it["ah"]))
        it["ut"] = _dot(it["tinv"], blockdiag(it["akv"].astype(BF16)))

    states = [state_ref[j] for j in range(n_pairs)]
    y_chunks = []
    for c in range(n_chunks):
        its = items[c * n_pairs:(c + 1) * n_pairs]
        for j, it in enumerate(its):
            lhs = jnp.concatenate([it["w"].astype(BF16), it["rh"]], axis=0)
            it["ws"] = _dot_nt(lhs, blockdiag(states[j].astype(BF16)))
        for j, it in enumerate(its):
            it["ub"] = (it["ws"][:CHUNK] + it["ut"]).astype(BF16)
            uv = jnp.concatenate([it["ub"], it["v"]], axis=0)
            upd = _dot_tn(uv, jnp.concatenate([it["bh"], it["kh"]], axis=0))
            ls = slice(j * LANE, (j + 1) * LANE)
            states[j] = (states[j] + upd[:RWKV_HEAD] * m_left + upd[RWKV_HEAD:] * m_right) * g_ends[c][:, ls]
        y_pairs = [it["ws"][CHUNK:] + _dot(it["rbk"], jnp.concatenate([blockdiag(it["ub"]), blockdiag(it["v"])], axis=0))
                   for it in its]
        y_chunks.append(jnp.concatenate(y_pairs, axis=1))
    for j in range(n_pairs):
        state_ref[j] = states[j]
    y = jnp.concatenate(y_chunks, axis=0) if len(y_chunks) > 1 else y_chunks[0]

    inv_n = 1.0 / RWKV_HEAD
    m = _head_sum(y, ones_blk) * inv_n
    yc = y - m
    var = _head_sum(yc * yc, ones_blk) * inv_n
    y = yc * lax.rsqrt(var + RWKV_GN_EPS) * gnw_ref[...] + gnb_ref[...]
    y = y + bonus * v
    y_ref[0] = (y * g).astype(BF16)


def _rwkv_mixer(p_r, vfirst, prm, has_vres):
    bsz, seq, _ = p_r.shape
    n_tok = min(256, seq)
    tok = lambda b, i: (b, i, 0)
    const = lambda b, i: (0, 0)
    vec = pl.BlockSpec((1, D_RWKV), const)
    in_specs = [pl.BlockSpec((1, n_tok, RW_WIDTH), tok)]
    args = [p_r]
    if has_vres:
        in_specs.append(pl.BlockSpec((1, n_tok, D_RWKV), tok))
        args.append(vfirst)
    in_specs += [pl.BlockSpec((1, RW_WIDTH), const), vec, pl.BlockSpec((LANE, D_RWKV), const), vec,
                 pl.BlockSpec((LANE, D_RWKV), const), pl.BlockSpec((2 * LANE, D_RWKV), const),
                 vec, vec, vec, vec, vec]
    args += [prm["mu"], prm["w0"], prm["w_up"], prm["a0"], prm["a_up"], prm["g_up"],
             prm["k_k"], prm["k_a"], prm["r_k"], prm["gn_w"], prm["gn_b"]]
    out_shape = [jax.ShapeDtypeStruct((bsz, seq, D_RWKV), BF16)]
    out_specs = [pl.BlockSpec((1, n_tok, D_RWKV), tok)]
    if has_vres:
        in_specs += [vec, pl.BlockSpec((D_RWKV, LANE), const), pl.BlockSpec((LANE, D_RWKV), const)]
        args += [prm["v0"], prm["v_down"], prm["v_up"]]
    else:
        out_shape.append(jax.ShapeDtypeStruct((bsz, seq, D_RWKV), F32))
        out_specs.append(pl.BlockSpec((1, n_tok, D_RWKV), tok))
    res = pl.pallas_call(
        functools.partial(_rwkv_kernel, has_vres),
        grid=(bsz, seq // n_tok),
        in_specs=in_specs,
        out_specs=out_specs,
        out_shape=out_shape,
        scratch_shapes=[pltpu.VMEM((RWKV_HEADS // 2, RWKV_HEAD, 2 * RWKV_HEAD), F32),
                        pltpu.VMEM((1, RW_WIDTH), F32)],
        compiler_params=_cparams(2),
        name="rwkv7_vres" if has_vres else "rwkv7",
    )(*args)
    if has_vres:
        return res[0], vfirst
    return res[0], res[1]


def _gla_kernel(p_ref, gkup_ref, gkb_ref, nw_ref, y_ref, state_ref):
    @pl.when(pl.program_id(1) == 0)
    def _():
        state_ref[...] = jnp.zeros_like(state_ref)

    p = p_ref[0].astype(F32)
    n_tok = p.shape[0]
    q = p[:, 0:GL_K]
    k = p[:, GL_K:GL_V]
    v = p[:, GL_V:GL_GK]
    gkd = p[:, GL_GK:GL_G]
    g = p[:, GL_G:GL_WIDTH]
    z = _dot(gkd.astype(BF16), gkup_ref[...]) + gkb_ref[...]
    gk = -_softplus(-z) * (1.0 / GLA_GATE_NORMALIZER)

    lower = _tril_mask(CHUNK, strict=False)
    tril_f = lower.astype(F32)
    n_chunks = n_tok // CHUNK
    items = []
    decs = []
    for c in range(n_chunks):
        sl = slice(c * CHUNK, (c + 1) * CHUNK)
        b = jnp.dot(tril_f, gk[sl], precision=HIGHEST, preferred_element_type=F32)
        b_last = b[CHUNK - 1:CHUNK, :]
        q_e = (q[sl] * jnp.exp(b) * (GLA_DK ** -0.5)).astype(BF16)
        k_e = (k[sl] * jnp.exp(-b)).astype(BF16)
        k_end = (k[sl] * jnp.exp(b_last - b)).astype(BF16)
        decs.append(jnp.exp(b_last))
        vb = v[sl].astype(BF16)
        for h in range(GLA_HEADS):
            ks = slice(h * GLA_DK, (h + 1) * GLA_DK)
            vs = slice(h * GLA_DV, (h + 1) * GLA_DV)
            items.append(dict(q=q_e[:, ks], k=k_e[:, ks], kend=k_end[:, ks], v=vb[:, vs]))
    for it in items:
        it["sc"] = jnp.where(lower, _dot_nt(it["q"], it["k"]), 0.0).astype(BF16)
    for it in items:
        it["intra"] = _dot(it["sc"], it["v"])
        it["kv"] = _dot_tn(it["v"], it["kend"])
    states = [state_ref[h] for h in range(GLA_HEADS)]
    o_chunks = []
    for c in range(n_chunks):
        o_heads = []
        for h in range(GLA_HEADS):
            it = items[c * GLA_HEADS + h]
            ks = slice(h * GLA_DK, (h + 1) * GLA_DK)
            o_h = it["intra"] + _dot_nt(it["q"], states[h].astype(BF16))
            states[h] = states[h] * decs[c][:, ks] + it["kv"]
            o_heads.append(o_h * lax.rsqrt(jnp.mean(o_h * o_h, axis=-1, keepdims=True) + RMS_EPS))
        o_chunks.append(jnp.concatenate(o_heads, axis=1))
    for h in range(GLA_HEADS):
        state_ref[h] = states[h]
    o = jnp.concatenate(o_chunks, axis=0) if len(o_chunks) > 1 else o_chunks[0]
    silu_g = g * _sigmoid(g)
    y_ref[0] = (o * nw_ref[...] * silu_g).astype(BF16)


def _gla_mixer(p_g, prm):
    bsz, seq, _ = p_g.shape
    n_tok = min(256, seq)
    tok = lambda b, i: (b, i, 0)
    const = lambda b, i: (0, 0)
    return pl.pallas_call(
        _gla_kernel,
        grid=(bsz, seq // n_tok),
        in_specs=[pl.BlockSpec((1, n_tok, GL_WIDTH), tok),
                  pl.BlockSpec((LANE, GL_K), const), pl.BlockSpec((1, GL_K), const),
                  pl.BlockSpec((1, D_GLA), const)],
        out_specs=pl.BlockSpec((1, n_tok, D_GLA), tok),
        out_shape=jax.ShapeDtypeStruct((bsz, seq, D_GLA), BF16),
        scratch_shapes=[pltpu.VMEM((GLA_HEADS, GLA_DV, GLA_DK), F32)],
        compiler_params=_cparams(2),
        name="gla",
    )(p_g, prm["gk_up"], prm["gk_b"], prm["norm_w"])


def _outproj_router_kernel(yr_ref, yg_ref, x_ref, wo_ref, lnw_ref, lnb_ref, rw_ref, rb_ref,
                           x1_ref, x1p_ref, idx_ref, gate_ref, rank_ref, cnt_ref, carry_ref):
    @pl.when(pl.program_id(0) == 0)
    def _():
        carry_ref[...] = jnp.zeros_like(carry_ref)

    mix = _dot(yr_ref[...], wo_ref[0:D_RWKV, :]) + _dot(yg_ref[...], wo_ref[D_RWKV:, :])
    x1 = _layer_norm(DEEPNORM_ALPHA * x_ref[...] + mix, lnw_ref[...], lnb_ref[...])
    x1_ref[...] = x1
    _rows_store(x1p_ref, x1)
    tm = x1.shape[0]

    xh, xl = _split_bf16(x1)
    rw = rw_ref[...]
    wh, wl = _split_bf16(rw)
    logits = _dot_nt(wh, xh) + _dot_nt(wh, xl) + _dot_nt(wl, xh) + rb_ref[...]

    e_iota = lax.broadcasted_iota(jnp.int32, (N_EXPERTS, tm), 0)
    work = logits
    vals, idxs = [], []
    member = jnp.zeros((N_EXPERTS, tm), F32)
    for _ in range(TOP_K):
        mx = jnp.max(work, axis=0, keepdims=True)
        ix = jnp.min(jnp.where(work == mx, e_iota, N_EXPERTS), axis=0, keepdims=True)
        sel = e_iota == ix
        work = jnp.where(sel, -jnp.inf, work)
        member = jnp.where(sel, 1.0, member)
        vals.append(mx)
        idxs.append(ix)
    exps = [jnp.exp(vv - vals[0]) for vv in vals]
    inv_den = 1.0 / (exps[0] + exps[1] + exps[2] + exps[3])

    tr = lax.broadcasted_iota(jnp.int32, (tm, tm), 0)
    tc = lax.broadcasted_iota(jnp.int32, (tm, tm), 1)
    before = (tr < tc).astype(BF16)
    cex = _dot(member.astype(BF16), before) + carry_ref[...][:, 0:1]
    for kq in range(TOP_K):
        sel = e_iota == idxs[kq]
        idx_ref[kq:kq + 1, :] = idxs[kq]
        gate_ref[kq:kq + 1, :] = exps[kq] * inv_den
        rank_ref[kq:kq + 1, :] = jnp.sum(jnp.where(sel, cex, 0.0), axis=0, keepdims=True).astype(jnp.int32)
    carry_ref[...] = carry_ref[...] + jnp.sum(member, axis=1, keepdims=True)
    cnt_ref[...] = carry_ref[...].astype(jnp.int32)


def _outproj_router(yr, yg, x, wo, lnw, lnb, rw_t, rb):
    t = x.shape[0]
    tm = min(512, t)
    row = lambda i: (i, 0)
    col = lambda i: (0, i)
    const = lambda i: (0, 0)
    return pl.pallas_call(
        _outproj_router_kernel,
        grid=(t // tm,),
        in_specs=[pl.BlockSpec((tm, D_RWKV), row), pl.BlockSpec((tm, D_GLA), row),
                  pl.BlockSpec((tm, D_MODEL), row), pl.BlockSpec((D_MODEL, D_MODEL), const),
                  pl.BlockSpec((1, D_MODEL), const), pl.BlockSpec((1, D_MODEL), const),
                  pl.BlockSpec((N_EXPERTS, D_MODEL), const), pl.BlockSpec((N_EXPERTS, 1), const)],
        out_specs=[pl.BlockSpec((tm, D_MODEL), row), pl.BlockSpec((tm * ROW_R, LANE), row),
                   pl.BlockSpec((TOP_K, tm), col), pl.BlockSpec((TOP_K, tm), col),
                   pl.BlockSpec((TOP_K, tm), col), pl.BlockSpec((N_EXPERTS, LANE), const)],
        out_shape=[jax.ShapeDtypeStruct((t, D_MODEL), F32), jax.ShapeDtypeStruct((t * ROW_R, LANE), U32),
                   jax.ShapeDtypeStruct((TOP_K, t), jnp.int32), jax.ShapeDtypeStruct((TOP_K, t), F32),
                   jax.ShapeDtypeStruct((TOP_K, t), jnp.int32),
                   jax.ShapeDtypeStruct((N_EXPERTS, LANE), jnp.int32)],
        scratch_shapes=[pltpu.VMEM((N_EXPERTS, LANE), F32)],
        compiler_params=_cparams(1),
        name="outproj_router",
    )(yr, yg, x, wo, lnw, lnb, rw_t, rb)


SEG_BIG = 64
SEG_MID = 8


def _dispatch_kernel(pos_ref, seg_ref, gaps_ref, x_ref, xs_ref, stage_ref, zero_ref, sem, zsem):
    te = x_ref.shape[0] // ROW_R
    i = pl.program_id(0)
    n_tiles = pl.num_programs(0)
    slot = i % 2
    base = i * (TOP_K * te)

    def drain(sl):
        for kq in range(TOP_K):
            pltpu.make_async_copy(x_ref, x_ref, sem.at[sl]).wait()

    @pl.when(pl.program_id(0) == 0)
    def _():
        zero_ref[...] = jnp.zeros_like(zero_ref)

        def zero_copy(d):
            return pltpu.make_async_copy(zero_ref, _row_tile(xs_ref, d), zsem)

        for g in range(N_EXPERTS + 1):
            lo = gaps_ref[2 * g]
            hi = gaps_ref[2 * g + 1]
            lax.fori_loop(lo, hi, lambda d, c: (zero_copy(d).start(), c)[1], 0)
        blk = pl.ds(0, MOE_BLOCK * ROW_R)
        for g in range(N_EXPERTS):
            pltpu.make_async_copy(xs_ref.at[blk, :], xs_ref.at[blk, :], zsem).wait()

    @pl.when(i >= 2)
    def _():
        drain(slot)

    def place(r, carry):
        row = x_ref[pl.ds(pl.multiple_of(r * ROW_R, ROW_R), ROW_R), :]
        for kq in range(TOP_K):
            p = pos_ref[base + kq * te + r]
            stage_ref[slot, pl.ds(pl.multiple_of(p * ROW_R, ROW_R), ROW_R), :] = row
        return carry

    lax.fori_loop(0, te, place, 0, unroll=4)

    for e in range(N_EXPERTS):
        o = (i * N_EXPERTS + e) * 3
        s, n, g = seg_ref[o], seg_ref[o + 1], seg_ref[o + 2]

        def copy(off, rows, s=s, g=g):
            src = stage_ref.at[slot, pl.ds(pl.multiple_of((s + off) * ROW_R, ROW_R), rows * ROW_R), :]
            dst = xs_ref.at[pl.ds(pl.multiple_of((g + off) * ROW_R, ROW_R), rows * ROW_R), :]
            pltpu.make_async_copy(src, dst, sem.at[slot]).start()

        n_big = lax.shift_right_logical(n, 6)
        n_mid = lax.shift_right_logical(n, 3) & 7
        n_one = n & 7
        off_mid = n_big * SEG_BIG
        off_one = off_mid + n_mid * SEG_MID
        lax.fori_loop(0, n_big, lambda q, c, copy=copy: (copy(q * SEG_BIG, SEG_BIG), c)[1], 0)
        lax.fori_loop(0, n_mid, lambda q, c, copy=copy, off_mid=off_mid: (copy(off_mid + q * SEG_MID, SEG_MID), c)[1], 0)
        lax.fori_loop(0, n_one, lambda q, c, copy=copy, off_one=off_one: (copy(off_one + q, 1), c)[1], 0)

    @pl.when(i == n_tiles - 1)
    def _():
        @pl.when(i >= 1)
        def _():
            drain(1 - slot)
        drain(slot)


def _dispatch(x1, pos_tiles, seg, gaps, m_pad, te):
    t = x1.shape[0] // ROW_R
    return pl.pallas_call(
        _dispatch_kernel,
        grid_spec=pltpu.PrefetchScalarGridSpec(
            num_scalar_prefetch=3,
            grid=(t // te,),
            in_specs=[pl.BlockSpec((te * ROW_R, LANE), lambda i, p, s, g: (i, 0))],
            out_specs=pl.BlockSpec(memory_space=pl.ANY),
            scratch_shapes=[pltpu.VMEM((2, TOP_K * te * ROW_R, LANE), U32), pltpu.VMEM((ROW_R, LANE), U32),
                            pltpu.SemaphoreType.DMA((2,)), pltpu.SemaphoreType.DMA(())],
        ),
        out_shape=jax.ShapeDtypeStruct((m_pad * ROW_R, LANE), U32),
        compiler_params=_cparams(1),
        name="moe_dispatch",
    )(pos_tiles, seg, gaps, x1)


def _expert_kernel(be_ref, nv_ref, xs_ref, w1_ref, b1_ref, w2_ref, b2_ref, ys_ref, act_ref, w1b_ref, w2b_ref):
    j = pl.program_id(0)
    n_valid = nv_ref[j]

    @pl.when((j == 0) | (be_ref[j] != be_ref[jnp.maximum(j - 1, 0)]))
    def _():
        rows = 128
        for c in range(D_MODEL // rows):
            rs = slice(c * rows, (c + 1) * rows)
            w1b_ref[rs, :] = w1_ref[0, 0, rs, :].astype(BF16)
            w2b_ref[rs, :] = w2_ref[0, 0, rs, :].astype(BF16)

    @pl.when(n_valid == 0)
    def _():
        ys_ref[...] = jnp.zeros_like(ys_ref)

    @pl.when(n_valid > 0)
    def _():
        xb = _rows_load(xs_ref, MOE_BLOCK).astype(BF16)
        n_chunk = 256
        for c in range(D_FF // n_chunk):
            cs = slice(c * n_chunk, (c + 1) * n_chunk)
            us = slice(D_FF + c * n_chunk, D_FF + (c + 1) * n_chunk)
            gt = _dot(xb, w1b_ref[:, cs]) + b1_ref[0, 0, :, cs]
            up = _dot(xb, w1b_ref[:, us]) + b1_ref[0, 0, :, us]
            gt = jnp.minimum(gt, SWIGLU_LIMIT)
            up = jnp.clip(up, -SWIGLU_LIMIT, SWIGLU_LIMIT)
            act_ref[:, cs] = ((up + 1.0) * gt * _sigmoid(SWIGLU_ALPHA * gt)).astype(BF16)
        _rows_store(ys_ref, _dot(act_ref[...], w2b_ref[...]) + b2_ref[0, 0])


def _experts(xs, block_e, n_valid, layer, w1, b1, w2, b2):
    m_pad = xs.shape[0] // ROW_R
    n_blocks = m_pad // MOE_BLOCK
    wmap = lambda j, be, nv: (layer, be[j], 0, 0)
    return pl.pallas_call(
        _expert_kernel,
        grid_spec=pltpu.PrefetchScalarGridSpec(
            num_scalar_prefetch=2,
            grid=(n_blocks,),
            in_specs=[pl.BlockSpec((MOE_BLOCK * ROW_R, LANE), lambda j, be, nv: (j, 0)),
                      pl.BlockSpec((1, 1, D_MODEL, 2 * D_FF), wmap),
                      pl.BlockSpec((1, 1, 1, 2 * D_FF), wmap),
                      pl.BlockSpec((1, 1, D_FF, D_MODEL), wmap),
                      pl.BlockSpec((1, 1, 1, D_MODEL), wmap)],
            out_specs=pl.BlockSpec((MOE_BLOCK * ROW_R, LANE), lambda j, be, nv: (j, 0)),
            scratch_shapes=[pltpu.VMEM((MOE_BLOCK, D_FF), BF16),
                            pltpu.VMEM((D_MODEL, 2 * D_FF), BF16), pltpu.VMEM((D_FF, D_MODEL), BF16)],
        ),
        out_shape=jax.ShapeDtypeStruct((m_pad * ROW_R, LANE), U32),
        compiler_params=_cparams(1),
        name="moe_experts",
    )(block_e, n_valid, xs, w1, b1, w2, b2)


def _combine_kernel(with_inproj, dest_ref, ys_ref, x1_ref, gate_ref, lnw_ref, lnb_ref, *refs):
    if with_inproj:
        wr_ref, wg_ref, out_ref, pr_ref, pg_ref, buf_ref, sem = refs
    else:
        out_ref, buf_ref, sem = refs
    te = x1_ref.shape[0]
    i = pl.program_id(0)
    n_tiles = pl.num_programs(0)

    def gather(tile, slot):
        base = tile * (TOP_K * te)

        def issue(r, carry):
            for kq in range(TOP_K):
                d = dest_ref[base + kq * te + r]
                pltpu.make_async_copy(_row_tile(ys_ref, d), _row_tile(buf_ref.at[slot, kq], r),
                                      sem.at[slot]).start(priority=kq % 2)
            return carry

        lax.fori_loop(0, te, issue, 0, unroll=8)

    @pl.when(i == 0)
    def _():
        gather(0, 0)

    for slot in range(2):
        @pl.when((i + 1 < n_tiles) & ((i + 1) % 2 == slot))
        def _():
            gather(i + 1, slot)

    for slot in range(2):
        @pl.when(i % 2 == slot)
        def _():
            for kq in range(TOP_K):
                pltpu.make_async_copy(buf_ref.at[slot, kq], buf_ref.at[slot, kq], sem.at[slot]).wait()
            gates = gate_ref[...]
            z = DEEPNORM_ALPHA * x1_ref[...]
            for kq in range(TOP_K):
                z = z + _rows_load(buf_ref.at[slot, kq], te) * gates[:, kq:kq + 1]
            xn = _layer_norm(z, lnw_ref[...], lnb_ref[...])
            out_ref[...] = xn
            if with_inproj:
                xb = xn.astype(BF16)
                pr_ref[...] = _dot(xb, wr_ref[...]).astype(BF16)
                pg_ref[...] = _dot(xb, wg_ref[...]).astype(BF16)


def _combine(ys, dest_tiles, x1, gates_t, lnw, lnb, te, next_inproj=None):
    t = x1.shape[0]
    row = lambda i, d: (i, 0)
    const = lambda i, d: (0, 0)
    in_specs = [pl.BlockSpec(memory_space=pl.ANY),
                pl.BlockSpec((te, D_MODEL), row),
                pl.BlockSpec((te, TOP_K), row),
                pl.BlockSpec((1, D_MODEL), const),
                pl.BlockSpec((1, D_MODEL), const)]
    out_specs = [pl.BlockSpec((te, D_MODEL), row)]
    out_shape = [jax.ShapeDtypeStruct((t, D_MODEL), F32)]
    args = [dest_tiles, ys, x1, gates_t, lnw, lnb]
    if next_inproj is not None:
        in_specs += [pl.BlockSpec((D_MODEL, RW_WIDTH), const), pl.BlockSpec((D_MODEL, GL_WIDTH), const)]
        out_specs += [pl.BlockSpec((te, RW_WIDTH), row), pl.BlockSpec((te, GL_WIDTH), row)]
        out_shape += [jax.ShapeDtypeStruct((t, RW_WIDTH), BF16), jax.ShapeDtypeStruct((t, GL_WIDTH), BF16)]
        args += list(next_inproj)
    return pl.pallas_call(
        functools.partial(_combine_kernel, next_inproj is not None),
        grid_spec=pltpu.PrefetchScalarGridSpec(
            num_scalar_prefetch=1,
            grid=(t // te,),
            in_specs=in_specs,
            out_specs=out_specs,
            scratch_shapes=[pltpu.VMEM((2, TOP_K, te * ROW_R, LANE), U32), pltpu.SemaphoreType.DMA((2,))],
        ),
        out_shape=out_shape,
        compiler_params=_cparams(1),
        name="moe_combine_inproj" if next_inproj is not None else "moe_combine",
    )(*args)


def _moe(x1, x1p, idx, gate, rank, counts, layer, w1, b1, w2, b2, lnw, lnb, next_inproj=None):
    t = x1.shape[0]
    te = min(256, t)
    assert (t * TOP_K) % MOE_BLOCK == 0
    n_blocks = (t * TOP_K) // MOE_BLOCK + N_EXPERTS
    m_pad = n_blocks * MOE_BLOCK
    padded = ((counts + MOE_BLOCK - 1) // MOE_BLOCK) * MOE_BLOCK
    pend = jnp.cumsum(padded)
    pstart = pend - padded
    dest = rank
    for e in range(N_EXPERTS):
        dest = dest + jnp.where(idx == e, pstart[e], 0)
    dest_tiles = dest.reshape(TOP_K, t // te, te).transpose(1, 0, 2).reshape(-1)
    blk_start = jnp.arange(n_blocks, dtype=jnp.int32) * MOE_BLOCK
    block_e = jnp.minimum(jnp.sum(pend[None, :] <= blk_start[:, None], axis=1), N_EXPERTS - 1).astype(jnp.int32)
    is_e = block_e[:, None] == jnp.arange(N_EXPERTS, dtype=jnp.int32)[None, :]
    blk_end = jnp.sum(jnp.where(is_e, (pstart + counts)[None, :], 0), axis=1)
    n_valid = jnp.clip(blk_end - blk_start, 0, MOE_BLOCK).astype(jnp.int32)

    gap_lo = jnp.concatenate([pstart + counts, pend[-1:]])
    gap_hi = jnp.concatenate([pend, jnp.full((1,), m_pad, pend.dtype)])
    gaps = jnp.stack([gap_lo, gap_hi], axis=1).reshape(-1).astype(jnp.int32)

    td = min(512, t)
    n_td = t // td
    idx_t = idx.reshape(TOP_K, n_td, td)
    e_ids = jnp.arange(N_EXPERTS, dtype=jnp.int32)
    tcnt = jnp.sum((idx_t[..., None] == e_ids).astype(jnp.int32), axis=(0, 2))
    cum = jnp.cumsum(tcnt, axis=0) - tcnt
    toff = jnp.cumsum(tcnt, axis=1) - tcnt
    seg = jnp.stack([toff, tcnt, pstart[None, :] + cum], axis=-1).reshape(-1).astype(jnp.int32)
    delta = toff - cum
    pos = rank.reshape(TOP_K, n_td, td)
    for e in range(N_EXPERTS):
        pos = pos + jnp.where(idx_t == e, delta[None, :, e, None], 0)
    pos_tiles = pos.transpose(1, 0, 2).reshape(-1).astype(jnp.int32)

    xs = _dispatch(x1p, pos_tiles, seg, gaps, m_pad, td)
    ys = _experts(xs, block_e, n_valid, layer, w1, b1, w2, b2)
    return _combine(ys, dest_tiles, x1, gate.T, lnw, lnb, te, next_inproj)


def _pad_cols(w, width):
    return jnp.pad(w, ((0, 0), (0, width - w.shape[1])))


def _pad_rows(w, height):
    return jnp.pad(w, ((0, height - w.shape[0]), (0, 0)))


def _pack_inproj(w_in_l, mu_l):
    d3 = 3 * D_RWKV
    o_ad = d3 + W_LORA
    o_gd = o_ad + A_LORA
    o_gla = o_gd + G_LORA
    wr = jnp.concatenate([w_in_l[:, :d3],
                          _pad_cols(w_in_l[:, d3:o_ad], LANE),
                          _pad_cols(w_in_l[:, o_ad:o_gd], LANE),
                          _pad_cols(w_in_l[:, o_gd:o_gla], 2 * LANE)], axis=1)
    mu = mu_l[None, :]
    mu_p = jnp.concatenate([mu[:, :d3], _pad_cols(mu[:, d3:o_ad], LANE), _pad_cols(mu[:, o_ad:o_gd], LANE),
                            _pad_cols(mu[:, o_gd:o_gla], 2 * LANE)], axis=1)
    g = w_in_l[:, o_gla:]
    wg = jnp.concatenate([g[:, :GL_GK], _pad_cols(g[:, GL_GK:GL_GK + GLA_GATE_RANK], LANE),
                          g[:, GL_GK + GLA_GATE_RANK:]], axis=1)
    return wr.astype(BF16), wg.astype(BF16), mu_p


def kernel(x, ln_in_w, ln_in_b, w_in, rwkv_mu, rwkv_w0, rwkv_w_up, rwkv_a0, rwkv_a_up, rwkv_g_up, rwkv_k_k, rwkv_k_a, rwkv_r_k, rwkv_gn_w, rwkv_gn_b, rwkv_v0, rwkv_v_down, rwkv_v_up, gla_gk_up, gla_gk_b, gla_norm_w, w_out, ln1_w, ln1_b, router_w, router_b, exp_w1, exp_b1, exp_w2, exp_b2, ln2_w, ln2_b):
    bsz, seq, d = x.shape
    t = bsz * seq
    xf = x.reshape(t, d)
    vfirst = None
    packed = [_pack_inproj(w_in[l], rwkv_mu[l]) for l in range(DEPTH)]
    xf, p_r, p_g = _inproj(xf, ln_in_w[None, :], ln_in_b[None, :], packed[0][0], packed[0][1])
    for l in range(DEPTH):
        mu_p = packed[l][2]
        rprm = {
            "mu": mu_p, "w0": rwkv_w0[l][None, :], "w_up": _pad_rows(rwkv_w_up[l], LANE).astype(BF16),
            "a0": rwkv_a0[l][None, :], "a_up": _pad_rows(rwkv_a_up[l], LANE).astype(BF16),
            "g_up": _pad_rows(rwkv_g_up[l], 2 * LANE).astype(BF16),
            "k_k": rwkv_k_k[l][None, :], "k_a": rwkv_k_a[l][None, :], "r_k": rwkv_r_k[l][None, :],
            "gn_w": rwkv_gn_w[l][None, :], "gn_b": rwkv_gn_b[l][None, :],
        }
        if l > 0:
            rprm["v0"] = rwkv_v0[l - 1][None, :]
            rprm["v_down"] = _pad_cols(rwkv_v_down[l - 1], LANE).astype(BF16)
            rprm["v_up"] = _pad_rows(rwkv_v_up[l - 1], LANE).astype(BF16)
        y_r, vfirst = _rwkv_mixer(p_r.reshape(bsz, seq, RW_WIDTH), vfirst, rprm, has_vres=l > 0)
        gprm = {"gk_up": _pad_rows(gla_gk_up[l], LANE).astype(BF16), "gk_b": gla_gk_b[l][None, :],
                "norm_w": jnp.tile(gla_norm_w[l], GLA_HEADS)[None, :]}
        y_g = _gla_mixer(p_g.reshape(bsz, seq, GL_WIDTH), gprm)
        x1, x1p, idx, gate, rank, cnt = _outproj_router(
            y_r.reshape(t, D_RWKV), y_g.reshape(t, D_GLA), xf, w_out[l].astype(BF16),
            ln1_w[l][None, :], ln1_b[l][None, :], router_w[l].T, router_b[l][:, None])
        nxt = packed[l + 1][:2] if l + 1 < DEPTH else None
        res = _moe(x1, x1p, idx, gate, rank, cnt[:, 0], l, exp_w1, exp_b1[:, :, None, :],
                   exp_w2, exp_b2[:, :, None, :], ln2_w[l][None, :], ln2_b[l][None, :], nxt)
        if nxt is not None:
            xf, p_r, p_g = res
        else:
            xf = res[0]
    return xf.reshape(bsz, seq, d)
```

```python
import functools

import jax
import jax.numpy as jnp
from jax import lax
from jax.experimental import pallas as pl
from jax.experimental.pallas import tpu as pltpu

F32 = jnp.float32
BF16 = jnp.bfloat16
HIGHEST = lax.Precision.HIGHEST

D_MODEL = 1024
DEPTH = 2
CHUNK = 64
RWKV_HEAD = 64
D_RWKV = 512
RWKV_HEADS = 8
W_LORA = 64
A_LORA = 64
V_LORA = 32
G_LORA = 160
RWKV_GN_EPS = 64e-5
D_GLA = 512
GLA_HEADS = 4
GLA_DV = 128
GLA_DK = 64
GLA_GATE_RANK = 16
GLA_GATE_NORMALIZER = 16.0
RMS_EPS = 1e-6
N_EXPERTS = 32
TOP_K = 4
D_FF = 1024
SWIGLU_LIMIT = 7.0
SWIGLU_ALPHA = 1.702
MOE_BLOCK = 512
DEEPNORM_ALPHA = (2 * DEPTH) ** 0.25
LN_EPS = 1e-5

LANE = 128
SUBLANE = 8
ROW_R = D_MODEL // LANE
assert ROW_R == SUBLANE
RW_WD = 3 * D_RWKV
RW_AD = RW_WD + LANE
RW_GD = RW_AD + LANE
RW_WIDTH = RW_GD + 2 * LANE
GL_K = GLA_HEADS * GLA_DK
GL_V = 2 * GL_K
GL_GK = GL_V + D_GLA
GL_G = GL_GK + LANE
GL_WIDTH = GL_G + D_GLA

VMEM_LIMIT = 56 * 1024 * 1024


def _cparams(n_axes):
    return pltpu.CompilerParams(dimension_semantics=("arbitrary",) * n_axes,
                                vmem_limit_bytes=VMEM_LIMIT)


def _dot(a, b):
    return jnp.dot(a, b, preferred_element_type=F32)


def _dot_nt(a, b):
    return lax.dot_general(a, b, (((1,), (1,)), ((), ())), preferred_element_type=F32)


def _dot_tn(a, b):
    return lax.dot_general(a, b, (((0,), (0,)), ((), ())), preferred_element_type=F32)


def _split_bf16(x):
    hi = x.astype(BF16)
    lo = (x - hi.astype(F32)).astype(BF16)
    return hi, lo


def _layer_norm(z, w, b):
    mu = jnp.mean(z, axis=-1, keepdims=True)
    zc = z - mu
    var = jnp.mean(zc * zc, axis=-1, keepdims=True)
    return zc * lax.rsqrt(var + LN_EPS) * w + b


def _sigmoid(x):
    return 1.0 / (1.0 + jnp.exp(-x))


def _softplus(x):
    return jnp.maximum(x, 0.0) + jnp.log(1.0 + jnp.exp(-jnp.abs(x)))


def _rows_load(ref, n):
    return jnp.concatenate([ref[pl.ds(s, n, stride=ROW_R), :] for s in range(ROW_R)], axis=1)


def _rows_store(ref, val):
    for s in range(ROW_R):
        ref[pl.ds(s, val.shape[0], stride=ROW_R), :] = val[:, s * LANE:(s + 1) * LANE]


def _row_tile(ref, row):
    return ref.at[pl.ds(pl.multiple_of(row * ROW_R, ROW_R), ROW_R), :]


def _tril_mask(n, strict):
    r = lax.broadcasted_iota(jnp.int32, (n, n), 0)
    c = lax.broadcasted_iota(jnp.int32, (n, n), 1)
    return (c < r) if strict else (c <= r)


def _inproj_kernel(x_ref, lnw_ref, lnb_ref, wr_ref, wg_ref, x0_ref, pr_ref, pg_ref):
    x = _layer_norm(x_ref[...], lnw_ref[...], lnb_ref[...])
    x0_ref[...] = x
    xb = x.astype(BF16)
    pr_ref[...] = _dot(xb, wr_ref[...]).astype(BF16)
    pg_ref[...] = _dot(xb, wg_ref[...]).astype(BF16)


def _inproj(x, lnw, lnb, wr, wg):
    t = x.shape[0]
    tm = min(512, t)
    row = lambda i: (i, 0)
    const = lambda i: (0, 0)
    return pl.pallas_call(
        _inproj_kernel,
        grid=(t // tm,),
        in_specs=[pl.BlockSpec((tm, D_MODEL), row),
                  pl.BlockSpec((1, D_MODEL), const), pl.BlockSpec((1, D_MODEL), const),
                  pl.BlockSpec((D_MODEL, RW_WIDTH), const), pl.BlockSpec((D_MODEL, GL_WIDTH), const)],
        out_specs=[pl.BlockSpec((tm, D_MODEL), row), pl.BlockSpec((tm, RW_WIDTH), row),
                   pl.BlockSpec((tm, GL_WIDTH), row)],
        out_shape=[jax.ShapeDtypeStruct((t, D_MODEL), F32), jax.ShapeDtypeStruct((t, RW_WIDTH), BF16),
                   jax.ShapeDtypeStruct((t, GL_WIDTH), BF16)],
        compiler_params=_cparams(1),
        name="inproj_ln",
    )(x, lnw, lnb, wr, wg)


def _head_sum(x, ones_pair):
    n, tiles = x.shape[0], x.shape[1] // LANE
    st = jnp.concatenate([x[:, j * LANE:(j + 1) * LANE] for j in range(tiles)], axis=0)
    hi, lo = _split_bf16(st)
    s = _dot(jnp.concatenate([hi, lo], axis=0), ones_pair)
    s = s[:tiles * n] + s[tiles * n:]
    return jnp.concatenate([s[j * n:(j + 1) * n] for j in range(tiles)], axis=1)


def _rwkv_kernel(has_vres, *refs):
    if has_vres:
        (p_ref, vfirst_ref, mu_ref, w0_ref, wup_ref, a0_ref, aup_ref, gup_ref, kk_ref, ka_ref, rk_ref,
         gnw_ref, gnb_ref, v0_ref, vdown_ref, vup_ref, y_ref, state_ref, prev_ref) = refs
    else:
        (p_ref, mu_ref, w0_ref, wup_ref, a0_ref, aup_ref, gup_ref, kk_ref, ka_ref, rk_ref,
         gnw_ref, gnb_ref, y_ref, vfirst_out_ref, state_ref, prev_ref) = refs

    @pl.when(pl.program_id(1) == 0)
    def _():
        state_ref[...] = jnp.zeros_like(state_ref)
        prev_ref[...] = jnp.zeros_like(prev_ref)

    p = p_ref[0].astype(F32)
    n_tok = p.shape[0]
    row = lax.broadcasted_iota(jnp.int32, (n_tok, 1), 0)
    prev = jnp.where(row == 0, prev_ref[...], pltpu.roll(p, 1, axis=0))
    prev_ref[...] = p[n_tok - 1:n_tok, :]
    xs = p + (prev - p) * mu_ref[...]
    r = xs[:, 0:D_RWKV]
    k = xs[:, D_RWKV:2 * D_RWKV]
    v = xs[:, 2 * D_RWKV:3 * D_RWKV]
    wd = xs[:, RW_WD:RW_AD]
    ad = xs[:, RW_AD:RW_GD]
    gd = xs[:, RW_GD:RW_WIDTH]

    w = w0_ref[...] + _dot(jnp.tanh(wd).astype(BF16), wup_ref[...])
    w = -_softplus(-w) - 0.5
    logw = -jnp.exp(w)
    a = _sigmoid(a0_ref[...] + _dot(ad.astype(BF16), aup_ref[...]))
    g = _dot(_sigmoid(gd).astype(BF16), gup_ref[...])
    if has_vres:
        vmix = _dot(_dot(v.astype(BF16), vdown_ref[...]).astype(BF16), vup_ref[...])
        v = v + (vfirst_ref[0] - v) * _sigmoid(v0_ref[...] + vmix)
    else:
        vfirst_out_ref[0] = v

    hr = lax.broadcasted_iota(jnp.int32, (LANE, LANE), 0) // RWKV_HEAD
    hc = lax.broadcasted_iota(jnp.int32, (LANE, LANE), 1) // RWKV_HEAD
    ones_blk = (hr == hc).astype(BF16)

    kk = k * kk_ref[...]
    kk = kk * lax.rsqrt(jnp.maximum(_head_sum(kk * kk, ones_blk), 1e-24))
    k = k * (1.0 + (a - 1.0) * ka_ref[...])
    bonus = _head_sum(r * k * rk_ref[...], ones_blk)

    n_chunks = n_tok // CHUNK
    n_pairs = RWKV_HEADS // 2
    assert 2 * RWKV_HEAD == LANE and CHUNK == RWKV_HEAD
    rr = lax.broadcasted_iota(jnp.int32, (CHUNK, CHUNK), 0)
    cc = lax.broadcasted_iota(jnp.int32, (CHUNK, CHUNK), 1)
    tril_f = (cc <= rr).astype(F32)
    r_p = lax.broadcasted_iota(jnp.int32, (CHUNK, LANE), 0)
    l_p = lax.broadcasted_iota(jnp.int32, (CHUNK, LANE), 1)
    c_p = jnp.where(l_p >= RWKV_HEAD, l_p - RWKV_HEAD, l_p)
    m_left = (l_p < RWKV_HEAD).astype(F32)
    m_right = (l_p >= RWKV_HEAD).astype(F32)
    m_left_b = m_left.astype(BF16)
    m_right_b = m_right.astype(BF16)
    m_strict = (c_p < r_p).astype(F32)
    m_lower = (c_p <= r_p).astype(F32)
    eye_p = (c_p == r_p).astype(F32)
    m_diag2 = ((r_p // 2) == (c_p // 2)).astype(F32)
    lvl_masks = []
    size = 2
    while size < CHUNK:
        lvl_masks.append((((r_p // (2 * size)) == (c_p // (2 * size))) & ((r_p // size) != (c_p // size))).astype(F32))
        size *= 2

    def blockdiag(xb):
        return jnp.concatenate([xb * m_left_b, xb * m_right_b], axis=0)

    items = []
    g_ends = []
    for c in range(n_chunks):
        sl = slice(c * CHUNK, (c + 1) * CHUNK)
        lw = logw[sl]
        b = jnp.dot(tril_f, lw, precision=HIGHEST, preferred_element_type=F32)
        eb = jnp.exp(b)
        enb = jnp.exp(-b)
        ebx = jnp.exp(b - lw)
        rh = (r[sl] * eb).astype(BF16)
        kh = (k[sl] * enb).astype(BF16)
        ah = (-kk[sl] * ebx).astype(BF16)
        bh = (kk[sl] * a[sl] * enb).astype(BF16)
        vb = v[sl].astype(BF16)
        g_ends.append(eb[CHUNK - 1:CHUNK, :])
        for j in range(n_pairs):
            ls = slice(j * LANE, (j + 1) * LANE)
            items.append(dict(ah=ah[:, ls], rh=rh[:, ls], bh=bh[:, ls], kh=kh[:, ls], v=vb[:, ls]))

    for it in items:
        ar = jnp.concatenate([it["ah"], it["rh"]], axis=0)
        it["sb"] = _dot_nt(ar, blockdiag(it["bh"]))
        it["sk"] = _dot_nt(ar, blockdiag(it["kh"]))
    for it in items:
        it["a_ab"] = it["sb"][:CHUNK] * m_strict
        it["ak"] = (it["sk"][:CHUNK] * m_strict).astype(BF16)
        it["rbk"] = jnp.concatenate([(it["sb"][CHUNK:] * m_lower).astype(BF16),
                                     (it["sk"][CHUNK:] * m_lower).astype(BF16)], axis=1)
        it["d"] = eye_p + it["a_ab"] * m_diag2
        del it["sb"], it["sk"]
    for mk in lvl_masks:
        for it in items:
            it["db"] = it["d"].astype(BF16)
            it["m"] = _dot((it["a_ab"] * mk).astype(BF16), blockdiag(it["db"]))
        for it in items:
            it["d"] = it["d"] + _dot(it["db"], blockdiag(it["m"].astype(BF16)))
    for it in items:
        it["akv"] = _dot(it["ak"], blockdiag(it["v"]))
        it["tinv"] = it["d"].astype(BF16)
    for it in items:
        it["w"] = _dot(it["tinv"], blockdiag(it["ah"]))
        it["ut"] = _dot(it["tinv"], blockdiag(it["akv"].astype(BF16)))

    states = [state_ref[j] for j in range(n_pairs)]
    y_chunks = []
    for c in range(n_chunks):
        its = items[c * n_pairs:(c + 1) * n_pairs]
        for j, it in enumerate(its):
            lhs = jnp.concatenate([it["w"].astype(BF16), it["rh"]], axis=0)
            it["ws"] = _dot_nt(lhs, blockdiag(states[j].astype(BF16)))
        for j, it in enumerate(its):
            it["ub"] = (it["ws"][:CHUNK] + it["ut"]).astype(BF16)
            uv = jnp.concatenate([it["ub"], it["v"]], axis=0)
            upd = _dot_tn(uv, jnp.concatenate([it["bh"], it["kh"]], axis=0))
            ls = slice(j * LANE, (j + 1) * LANE)
            states[j] = (states[j] + upd[:RWKV_HEAD] * m_left + upd[RWKV_HEAD:] * m_right) * g_ends[c][:, ls]
        y_pairs = [it["ws"][CHUNK:] + _dot(it["rbk"], jnp.concatenate([blockdiag(it["ub"]), blockdiag(it["v"])], axis=0))
                   for it in its]
        y_chunks.append(jnp.concatenate(y_pairs, axis=1))
    for j in range(n_pairs):
        state_ref[j] = states[j]
    y = jnp.concatenate(y_chunks, axis=0) if len(y_chunks) > 1 else y_chunks[0]

    inv_n = 1.0 / RWKV_HEAD
    m = _head_sum(y, ones_blk) * inv_n
    yc = y - m
    var = _head_sum(yc * yc, ones_blk) * inv_n
    y = yc * lax.rsqrt(var + RWKV_GN_EPS) * gnw_ref[...] + gnb_ref[...]
    y = y + bonus * v
    y_ref[0] = (y * g).astype(BF16)


def _rwkv_mixer(p_r, vfirst, prm, has_vres):
    bsz, seq, _ = p_r.shape
    n_tok = min(256, seq)
    tok = lambda b, i: (b, i, 0)
    const = lambda b, i: (0, 0)
    vec = pl.BlockSpec((1, D_RWKV), const)
    in_specs = [pl.BlockSpec((1, n_tok, RW_WIDTH), tok)]
    args = [p_r]
    if has_vres:
        in_specs.append(pl.BlockSpec((1, n_tok, D_RWKV), tok))
        args.append(vfirst)
    in_specs += [pl.BlockSpec((1, RW_WIDTH), const), vec, pl.BlockSpec((LANE, D_RWKV), const), vec,
                 pl.BlockSpec((LANE, D_RWKV), const), pl.BlockSpec((2 * LANE, D_RWKV), const),
                 vec, vec, vec, vec, vec]
    args += [prm["mu"], prm["w0"], prm["w_up"], prm["a0"], prm["a_up"], prm["g_up"],
             prm["k_k"], prm["k_a"], prm["r_k"], prm["gn_w"], prm["gn_b"]]
    out_shape = [jax.ShapeDtypeStruct((bsz, seq, D_RWKV), BF16)]
    out_specs = [pl.BlockSpec((1, n_tok, D_RWKV), tok)]
    if has_vres:
        in_specs += [vec, pl.BlockSpec((D_RWKV, LANE), const), pl.BlockSpec((LANE, D_RWKV), const)]
        args += [prm["v0"], prm["v_down"], prm["v_up"]]
    else:
        out_shape.append(jax.ShapeDtypeStruct((bsz, seq, D_RWKV), F32))
        out_specs.append(pl.BlockSpec((1, n_tok, D_RWKV), tok))
    res = pl.pallas_call(
        functools.partial(_rwkv_kernel, has_vres),
        grid=(bsz, seq // n_tok),
        in_specs=in_specs,
        out_specs=out_specs,
        out_shape=out_shape,
        scratch_shapes=[pltpu.VMEM((RWKV_HEADS // 2, RWKV_HEAD, 2 * RWKV_HEAD), F32),
                        pltpu.VMEM((1, RW_WIDTH), F32)],
        compiler_params=_cparams(2),
        name="rwkv7_vres" if has_vres else "rwkv7",
    )(*args)
    if has_vres:
        return res[0], vfirst
    return res[0], res[1]


def _gla_kernel(p_ref, gkup_ref, gkb_ref, nw_ref, y_ref, state_ref):
    @pl.when(pl.program_id(1) == 0)
    def _():
        state_ref[...] = jnp.zeros_like(state_ref)

    p = p_ref[0].astype(F32)
    n_tok = p.shape[0]
    q = p[:, 0:GL_K]
    k = p[:, GL_K:GL_V]
    v = p[:, GL_V:GL_GK]
    gkd = p[:, GL_GK:GL_G]
    g = p[:, GL_G:GL_WIDTH]
    z = _dot(gkd.astype(BF16), gkup_ref[...]) + gkb_ref[...]
    gk = -_softplus(-z) * (1.0 / GLA_GATE_NORMALIZER)

    lower = _tril_mask(CHUNK, strict=False)
    tril_f = lower.astype(F32)
    n_chunks = n_tok // CHUNK
    items = []
    decs = []
    for c in range(n_chunks):
        sl = slice(c * CHUNK, (c + 1) * CHUNK)
        b = jnp.dot(tril_f, gk[sl], precision=HIGHEST, preferred_element_type=F32)
        b_last = b[CHUNK - 1:CHUNK, :]
        q_e = (q[sl] * jnp.exp(b) * (GLA_DK ** -0.5)).astype(BF16)
        k_e = (k[sl] * jnp.exp(-b)).astype(BF16)
        k_end = (k[sl] * jnp.exp(b_last - b)).astype(BF16)
        decs.append(jnp.exp(b_last))
        vb = v[sl].astype(BF16)
        for h in range(GLA_HEADS):
            ks = slice(h * GLA_DK, (h + 1) * GLA_DK)
            vs = slice(h * GLA_DV, (h + 1) * GLA_DV)
            items.append(dict(q=q_e[:, ks], k=k_e[:, ks], kend=k_end[:, ks], v=vb[:, vs]))
    for it in items:
        it["sc"] = jnp.where(lower, _dot_nt(it["q"], it["k"]), 0.0).astype(BF16)
    for it in items:
        it["intra"] = _dot(it["sc"], it["v"])
        it["kv"] = _dot_tn(it["v"], it["kend"])
    states = [state_ref[h] for h in range(GLA_HEADS)]
    o_chunks = []
    for c in range(n_chunks):
        o_heads = []
        for h in range(GLA_HEADS):
            it = items[c * GLA_HEADS + h]
            ks = slice(h * GLA_DK, (h + 1) * GLA_DK)
            o_h = it["intra"] + _dot_nt(it["q"], states[h].astype(BF16))
            states[h] = states[h] * decs[c][:, ks] + it["kv"]
            o_heads.append(o_h * lax.rsqrt(jnp.mean(o_h * o_h, axis=-1, keepdims=True) + RMS_EPS))
        o_chunks.append(jnp.concatenate(o_heads, axis=1))
    for h in range(GLA_HEADS):
        state_ref[h] = states[h]
    o = jnp.concatenate(o_chunks, axis=0) if len(o_chunks) > 1 else o_chunks[0]
    silu_g = g * _sigmoid(g)
    y_ref[0] = (o * nw_ref[...] * silu_g).astype(BF16)


def _gla_mixer(p_g, prm):
    bsz, seq, _ = p_g.shape
    n_tok = min(256, seq)
    tok = lambda b, i: (b, i, 0)
    const = lambda b, i: (0, 0)
    return pl.pallas_call(
        _gla_kernel,
        grid=(bsz, seq // n_tok),
        in_specs=[pl.BlockSpec((1, n_tok, GL_WIDTH), tok),
                  pl.BlockSpec((LANE, GL_K), const), pl.BlockSpec((1, GL_K), const),
                  pl.BlockSpec((1, D_GLA), const)],
        out_specs=pl.BlockSpec((1, n_tok, D_GLA), tok),
        out_shape=jax.ShapeDtypeStruct((bsz, seq, D_GLA), BF16),
        scratch_shapes=[pltpu.VMEM((GLA_HEADS, GLA_DV, GLA_DK), F32)],
        compiler_params=_cparams(2),
        name="gla",
    )(p_g, prm["gk_up"], prm["gk_b"], prm["norm_w"])


def _outproj_router_kernel(yr_ref, yg_ref, x_ref, wo_ref, lnw_ref, lnb_ref, rw_ref, rb_ref,
                           x1_ref, x1p_ref, idx_ref, gate_ref, rank_ref, cnt_ref, carry_ref):
    @pl.when(pl.program_id(0) == 0)
    def _():
        carry_ref[...] = jnp.zeros_like(carry_ref)

    mix = _dot(yr_ref[...], wo_ref[0:D_RWKV, :]) + _dot(yg_ref[...], wo_ref[D_RWKV:, :])
    x1 = _layer_norm(DEEPNORM_ALPHA * x_ref[...] + mix, lnw_ref[...], lnb_ref[...])
    x1_ref[...] = x1
    _rows_store(x1p_ref, x1)
    tm = x1.shape[0]

    xh, xl = _split_bf16(x1)
    rw = rw_ref[...]
    wh, wl = _split_bf16(rw)
    logits = _dot_nt(wh, xh) + _dot_nt(wh, xl) + _dot_nt(wl, xh) + rb_ref[...]

    e_iota = lax.broadcasted_iota(jnp.int32, (N_EXPERTS, tm), 0)
    work = logits
    vals, idxs = [], []
    member = jnp.zeros((N_EXPERTS, tm), F32)
    for _ in range(TOP_K):
        mx = jnp.max(work, axis=0, keepdims=True)
        ix = jnp.min(jnp.where(work == mx, e_iota, N_EXPERTS), axis=0, keepdims=True)
        sel = e_iota == ix
        work = jnp.where(sel, -jnp.inf, work)
        member = jnp.where(sel, 1.0, member)
        vals.append(mx)
        idxs.append(ix)
    exps = [jnp.exp(vv - vals[0]) for vv in vals]
    inv_den = 1.0 / (exps[0] + exps[1] + exps[2] + exps[3])

    tr = lax.broadcasted_iota(jnp.int32, (tm, tm), 0)
    tc = lax.broadcasted_iota(jnp.int32, (tm, tm), 1)
    before = (tr < tc).astype(BF16)
    cex = _dot(member.astype(BF16), before) + carry_ref[...][:, 0:1]
    for kq in range(TOP_K):
        sel = e_iota == idxs[kq]
        idx_ref[kq:kq + 1, :] = idxs[kq]
        gate_ref[kq:kq + 1, :] = exps[kq] * inv_den
        rank_ref[kq:kq + 1, :] = jnp.sum(jnp.where(sel, cex, 0.0), axis=0, keepdims=True).astype(jnp.int32)
    carry_ref[...] = carry_ref[...] + jnp.sum(member, axis=1, keepdims=True)
    cnt_ref[...] = carry_ref[...].astype(jnp.int32)


def _outproj_router(yr, yg, x, wo, lnw, lnb, rw_t, rb):
    t = x.shape[0]
    tm = min(512, t)
    row = lambda i: (i, 0)
    col = lambda i: (0, i)
    const = lambda i: (0, 0)
    return pl.pallas_call(
        _outproj_router_kernel,
        grid=(t // tm,),
        in_specs=[pl.BlockSpec((tm, D_RWKV), row), pl.BlockSpec((tm, D_GLA), row),
                  pl.BlockSpec((tm, D_MODEL), row), pl.BlockSpec((D_MODEL, D_MODEL), const),
                  pl.BlockSpec((1, D_MODEL), const), pl.BlockSpec((1, D_MODEL), const),
                  pl.BlockSpec((N_EXPERTS, D_MODEL), const), pl.BlockSpec((N_EXPERTS, 1), const)],
        out_specs=[pl.BlockSpec((tm, D_MODEL), row), pl.BlockSpec((tm * ROW_R, LANE), row),
                   pl.BlockSpec((TOP_K, tm), col), pl.BlockSpec((TOP_K, tm), col),
                   pl.BlockSpec((TOP_K, tm), col), pl.BlockSpec((N_EXPERTS, LANE), const)],
        out_shape=[jax.ShapeDtypeStruct((t, D_MODEL), F32), jax.ShapeDtypeStruct((t * ROW_R, LANE), F32),
                   jax.ShapeDtypeStruct((TOP_K, t), jnp.int32), jax.ShapeDtypeStruct((TOP_K, t), F32),
                   jax.ShapeDtypeStruct((TOP_K, t), jnp.int32),
                   jax.ShapeDtypeStruct((N_EXPERTS, LANE), jnp.int32)],
        scratch_shapes=[pltpu.VMEM((N_EXPERTS, LANE), F32)],
        compiler_params=_cparams(1),
        name="outproj_router",
    )(yr, yg, x, wo, lnw, lnb, rw_t, rb)


SEG_BIG = 64
SEG_MID = 8


def _issue_run(n, start_copy):
    n_big = lax.shift_right_logical(n, 6)
    n_mid = lax.shift_right_logical(n, 3) & 7
    n_one = n & 7
    off_mid = n_big * SEG_BIG
    off_one = off_mid + n_mid * SEG_MID
    lax.fori_loop(0, n_big, lambda q, c: (start_copy(q * SEG_BIG, SEG_BIG), c)[1], 0)
    lax.fori_loop(0, n_mid, lambda q, c: (start_copy(off_mid + q * SEG_MID, SEG_MID), c)[1], 0)
    lax.fori_loop(0, n_one, lambda q, c: (start_copy(off_one + q, 1), c)[1], 0)


def _dispatch_kernel(pos_ref, seg_ref, gaps_ref, x_ref, xs_ref, stage_ref, zero_ref, sem, zsem):
    te = x_ref.shape[0] // ROW_R
    i = pl.program_id(0)
    n_tiles = pl.num_programs(0)
    slot = i % 2
    base = i * (TOP_K * te)

    def drain(sl):
        for kq in range(TOP_K):
            pltpu.make_async_copy(x_ref, x_ref, sem.at[sl]).wait()

    @pl.when(pl.program_id(0) == 0)
    def _():
        zero_ref[...] = jnp.zeros_like(zero_ref)

        for g in range(N_EXPERTS + 1):
            lo = gaps_ref[2 * g]

            def zero_copy(off, rows, lo=lo):
                dst = xs_ref.at[pl.ds(pl.multiple_of((lo + off) * ROW_R, ROW_R), rows * ROW_R), :]
                pltpu.make_async_copy(zero_ref.at[pl.ds(0, rows * ROW_R), :], dst, zsem).start()

            _issue_run(gaps_ref[2 * g + 1] - lo, zero_copy)
        blk = pl.ds(0, MOE_BLOCK * ROW_R)
        for g in range(N_EXPERTS):
            pltpu.make_async_copy(xs_ref.at[blk, :], xs_ref.at[blk, :], zsem).wait()

    @pl.when(i >= 2)
    def _():
        drain(slot)

    def place(r, carry):
        row = x_ref[pl.ds(pl.multiple_of(r * ROW_R, ROW_R), ROW_R), :]
        for kq in range(TOP_K):
            p = pos_ref[base + kq * te + r]
            stage_ref[slot, pl.ds(pl.multiple_of(p * ROW_R, ROW_R), ROW_R), :] = row
        return carry

    lax.fori_loop(0, te, place, 0, unroll=4)

    for e in range(N_EXPERTS):
        o = (i * N_EXPERTS + e) * 3
        s, n, g = seg_ref[o], seg_ref[o + 1], seg_ref[o + 2]

        def copy(off, rows, s=s, g=g):
            src = stage_ref.at[slot, pl.ds(pl.multiple_of((s + off) * ROW_R, ROW_R), rows * ROW_R), :]
            dst = xs_ref.at[pl.ds(pl.multiple_of((g + off) * ROW_R, ROW_R), rows * ROW_R), :]
            pltpu.make_async_copy(src, dst, sem.at[slot]).start()

        _issue_run(n, copy)

    @pl.when(i == n_tiles - 1)
    def _():
        @pl.when(i >= 1)
        def _():
            drain(1 - slot)
        drain(slot)


def _dispatch(x1, pos_tiles, seg, gaps, m_pad, te):
    t = x1.shape[0] // ROW_R
    return pl.pallas_call(
        _dispatch_kernel,
        grid_spec=pltpu.PrefetchScalarGridSpec(
            num_scalar_prefetch=3,
            grid=(t // te,),
            in_specs=[pl.BlockSpec((te * ROW_R, LANE), lambda i, p, s, g: (i, 0))],
            out_specs=pl.BlockSpec(memory_space=pl.ANY),
            scratch_shapes=[pltpu.VMEM((2, TOP_K * te * ROW_R, LANE), F32), pltpu.VMEM((SEG_BIG * ROW_R, LANE), F32),
                            pltpu.SemaphoreType.DMA((2,)), pltpu.SemaphoreType.DMA(())],
        ),
        out_shape=jax.ShapeDtypeStruct((m_pad * ROW_R, LANE), F32),
        compiler_params=_cparams(1),
        name="moe_dispatch",
    )(pos_tiles, seg, gaps, x1)


def _expert_kernel(be_ref, nv_ref, xs_ref, w1_ref, b1_ref, w2_ref, b2_ref, ys_ref, act_ref, w1b_ref, w2b_ref):
    j = pl.program_id(0)
    n_valid = nv_ref[j]

    @pl.when((j == 0) | (be_ref[j] != be_ref[jnp.maximum(j - 1, 0)]))
    def _():
        rows = 128
        for c in range(D_MODEL // rows):
            rs = slice(c * rows, (c + 1) * rows)
            w1b_ref[rs, :] = w1_ref[0, 0, rs, :].astype(BF16)
            w2b_ref[rs, :] = w2_ref[0, 0, rs, :].astype(BF16)

    @pl.when(n_valid == 0)
    def _():
        ys_ref[...] = jnp.zeros_like(ys_ref)

    @pl.when(n_valid > 0)
    def _():
        xb = _rows_load(xs_ref, MOE_BLOCK).astype(BF16)
        n_chunk = 256
        for c in range(D_FF // n_chunk):
            cs = slice(c * n_chunk, (c + 1) * n_chunk)
            us = slice(D_FF + c * n_chunk, D_FF + (c + 1) * n_chunk)
            gt = _dot(xb, w1b_ref[:, cs]) + b1_ref[0, 0, :, cs]
            up = _dot(xb, w1b_ref[:, us]) + b1_ref[0, 0, :, us]
            gt = jnp.minimum(gt, SWIGLU_LIMIT)
            up = jnp.clip(up, -SWIGLU_LIMIT, SWIGLU_LIMIT)
            act_ref[:, cs] = ((up + 1.0) * gt * _sigmoid(SWIGLU_ALPHA * gt)).astype(BF16)
        _rows_store(ys_ref, _dot(act_ref[...], w2b_ref[...]) + b2_ref[0, 0])


def _experts(xs, block_e, n_valid, layer, w1, b1, w2, b2):
    m_pad = xs.shape[0] // ROW_R
    n_blocks = m_pad // MOE_BLOCK
    wmap = lambda j, be, nv: (layer, be[j], 0, 0)
    return pl.pallas_call(
        _expert_kernel,
        grid_spec=pltpu.PrefetchScalarGridSpec(
            num_scalar_prefetch=2,
            grid=(n_blocks,),
            in_specs=[pl.BlockSpec((MOE_BLOCK * ROW_R, LANE), lambda j, be, nv: (j, 0)),
                      pl.BlockSpec((1, 1, D_MODEL, 2 * D_FF), wmap),
                      pl.BlockSpec((1, 1, 1, 2 * D_FF), wmap),
                      pl.BlockSpec((1, 1, D_FF, D_MODEL), wmap),
                      pl.BlockSpec((1, 1, 1, D_MODEL), wmap)],
            out_specs=pl.BlockSpec((MOE_BLOCK * ROW_R, LANE), lambda j, be, nv: (j, 0)),
            scratch_shapes=[pltpu.VMEM((MOE_BLOCK, D_FF), BF16),
                            pltpu.VMEM((D_MODEL, 2 * D_FF), BF16), pltpu.VMEM((D_FF, D_MODEL), BF16)],
        ),
        out_shape=jax.ShapeDtypeStruct((m_pad * ROW_R, LANE), F32),
        compiler_params=_cparams(1),
        name="moe_experts",
    )(block_e, n_valid, xs, w1, b1, w2, b2)


def _combine_kernel(with_inproj, dest_ref, ys_ref, x1_ref, gate_ref, lnw_ref, lnb_ref, *refs):
    if with_inproj:
        wr_ref, wg_ref, out_ref, pr_ref, pg_ref, buf_ref, sem = refs
    else:
        out_ref, buf_ref, sem = refs
    te = x1_ref.shape[0]
    i = pl.program_id(0)
    n_tiles = pl.num_programs(0)

    def gather(tile, slot):
        base = tile * (TOP_K * te)

        def issue(r, carry):
            for kq in range(TOP_K):
                d = dest_ref[base + kq * te + r]
                pltpu.make_async_copy(_row_tile(ys_ref, d), _row_tile(buf_ref.at[slot, kq], r),
                                      sem.at[slot]).start(priority=kq % 2)
            return carry

        lax.fori_loop(0, te, issue, 0, unroll=8)

    @pl.when(i == 0)
    def _():
        gather(0, 0)

    for slot in range(2):
        @pl.when((i + 1 < n_tiles) & ((i + 1) % 2 == slot))
        def _():
            gather(i + 1, slot)

    for slot in range(2):
        @pl.when(i % 2 == slot)
        def _():
            for kq in range(TOP_K):
                pltpu.make_async_copy(buf_ref.at[slot, kq], buf_ref.at[slot, kq], sem.at[slot]).wait()
            gates = gate_ref[...]
            z = DEEPNORM_ALPHA * x1_ref[...]
            for kq in range(TOP_K):
                z = z + _rows_load(buf_ref.at[slot, kq], te) * gates[:, kq:kq + 1]
            xn = _layer_norm(z, lnw_ref[...], lnb_ref[...])
            out_ref[...] = xn
            if with_inproj:
                xb = xn.astype(BF16)
                pr_ref[...] = _dot(xb, wr_ref[...]).astype(BF16)
                pg_ref[...] = _dot(xb, wg_ref[...]).astype(BF16)


def _combine(ys, dest_tiles, x1, gates_t, lnw, lnb, te, next_inproj=None):
    t = x1.shape[0]
    row = lambda i, d: (i, 0)
    const = lambda i, d: (0, 0)
    in_specs = [pl.BlockSpec(memory_space=pl.ANY),
                pl.BlockSpec((te, D_MODEL), row),
                pl.BlockSpec((te, TOP_K), row),
                pl.BlockSpec((1, D_MODEL), const),
                pl.BlockSpec((1, D_MODEL), const)]
    out_specs = [pl.BlockSpec((te, D_MODEL), row)]
    out_shape = [jax.ShapeDtypeStruct((t, D_MODEL), F32)]
    args = [dest_tiles, ys, x1, gates_t, lnw, lnb]
    if next_inproj is not None:
        in_specs += [pl.BlockSpec((D_MODEL, RW_WIDTH), const), pl.BlockSpec((D_MODEL, GL_WIDTH), const)]
        out_specs += [pl.BlockSpec((te, RW_WIDTH), row), pl.BlockSpec((te, GL_WIDTH), row)]
        out_shape += [jax.ShapeDtypeStruct((t, RW_WIDTH), BF16), jax.ShapeDtypeStruct((t, GL_WIDTH), BF16)]
        args += list(next_inproj)
    return pl.pallas_call(
        functools.partial(_combine_kernel, next_inproj is not None),
        grid_spec=pltpu.PrefetchScalarGridSpec(
            num_scalar_prefetch=1,
            grid=(t // te,),
            in_specs=in_specs,
            out_specs=out_specs,
            scratch_shapes=[pltpu.VMEM((2, TOP_K, te * ROW_R, LANE), F32), pltpu.SemaphoreType.DMA((2,))],
        ),
        out_shape=out_shape,
        compiler_params=_cparams(1),
        name="moe_combine_inproj" if next_inproj is not None else "moe_combine",
    )(*args)


def _moe(x1, x1p, idx, gate, rank, counts, layer, w1, b1, w2, b2, lnw, lnb, next_inproj=None):
    t = x1.shape[0]
    te = min(256, t)
    assert (t * TOP_K) % MOE_BLOCK == 0
    n_blocks = (t * TOP_K) // MOE_BLOCK + N_EXPERTS
    m_pad = n_blocks * MOE_BLOCK
    padded = ((counts + MOE_BLOCK - 1) // MOE_BLOCK) * MOE_BLOCK
    pend = jnp.cumsum(padded)
    pstart = pend - padded
    dest = rank
    for e in range(N_EXPERTS):
        dest = dest + jnp.where(idx == e, pstart[e], 0)
    dest_tiles = dest.reshape(TOP_K, t // te, te).transpose(1, 0, 2).reshape(-1)
    blk_start = jnp.arange(n_blocks, dtype=jnp.int32) * MOE_BLOCK
    block_e = jnp.minimum(jnp.sum(pend[None, :] <= blk_start[:, None], axis=1), N_EXPERTS - 1).astype(jnp.int32)
    is_e = block_e[:, None] == jnp.arange(N_EXPERTS, dtype=jnp.int32)[None, :]
    blk_end = jnp.sum(jnp.where(is_e, (pstart + counts)[None, :], 0), axis=1)
    n_valid = jnp.clip(blk_end - blk_start, 0, MOE_BLOCK).astype(jnp.int32)

    gap_lo = jnp.concatenate([pstart + counts, pend[-1:]])
    gap_hi = jnp.concatenate([pend, jnp.full((1,), m_pad, pend.dtype)])
    gaps = jnp.stack([gap_lo, gap_hi], axis=1).reshape(-1).astype(jnp.int32)

    td = min(512, t)
    n_td = t // td
    idx_t = idx.reshape(TOP_K, n_td, td)
    e_ids = jnp.arange(N_EXPERTS, dtype=jnp.int32)
    tcnt = jnp.sum((idx_t[..., None] == e_ids).astype(jnp.int32), axis=(0, 2))
    cum = jnp.cumsum(tcnt, axis=0) - tcnt
    toff = jnp.cumsum(tcnt, axis=1) - tcnt
    seg = jnp.stack([toff, tcnt, pstart[None, :] + cum], axis=-1).reshape(-1).astype(jnp.int32)
    delta = toff - cum
    pos = rank.reshape(TOP_K, n_td, td)
    for e in range(N_EXPERTS):
        pos = pos + jnp.where(idx_t == e, delta[None, :, e, None], 0)
    pos_tiles = pos.transpose(1, 0, 2).reshape(-1).astype(jnp.int32)

    xs = _dispatch(x1p, pos_tiles, seg, gaps, m_pad, td)
    ys = _experts(xs, block_e, n_valid, layer, w1, b1, w2, b2)
    return _combine(ys, dest_tiles, x1, gate.T, lnw, lnb, te, next_inproj)


def _pad_cols(w, width):
    return jnp.pad(w, ((0, 0), (0, width - w.shape[1])))


def _pad_rows(w, height):
    return jnp.pad(w, ((0, height - w.shape[0]), (0, 0)))


def _pack_inproj(w_in_l, mu_l):
    d3 = 3 * D_RWKV
    o_ad = d3 + W_LORA
    o_gd = o_ad + A_LORA
    o_gla = o_gd + G_LORA
    wr = jnp.concatenate([w_in_l[:, :d3],
                          _pad_cols(w_in_l[:, d3:o_ad], LANE),
                          _pad_cols(w_in_l[:, o_ad:o_gd], LANE),
                          _pad_cols(w_in_l[:, o_gd:o_gla], 2 * LANE)], axis=1)
    mu = mu_l[None, :]
    mu_p = jnp.concatenate([mu[:, :d3], _pad_cols(mu[:, d3:o_ad], LANE), _pad_cols(mu[:, o_ad:o_gd], LANE),
                            _pad_cols(mu[:, o_gd:o_gla], 2 * LANE)], axis=1)
    g = w_in_l[:, o_gla:]
    wg = jnp.concatenate([g[:, :GL_GK], _pad_cols(g[:, GL_GK:GL_GK + GLA_GATE_RANK], LANE),
                          g[:, GL_GK + GLA_GATE_RANK:]], axis=1)
    return wr.astype(BF16), wg.astype(BF16), mu_p


def kernel(x, ln_in_w, ln_in_b, w_in, rwkv_mu, rwkv_w0, rwkv_w_up, rwkv_a0, rwkv_a_up, rwkv_g_up, rwkv_k_k, rwkv_k_a, rwkv_r_k, rwkv_gn_w, rwkv_gn_b, rwkv_v0, rwkv_v_down, rwkv_v_up, gla_gk_up, gla_gk_b, gla_norm_w, w_out, ln1_w, ln1_b, router_w, router_b, exp_w1, exp_b1, exp_w2, exp_b2, ln2_w, ln2_b):
    bsz, seq, d = x.shape
    t = bsz * seq
    xf = x.reshape(t, d)
    vfirst = None
    packed = [_pack_inproj(w_in[l], rwkv_mu[l]) for l in range(DEPTH)]
    xf, p_r, p_g = _inproj(xf, ln_in_w[None, :], ln_in_b[None, :], packed[0][0], packed[0][1])
    for l in range(DEPTH):
        mu_p = packed[l][2]
        rprm = {
            "mu": mu_p, "w0": rwkv_w0[l][None, :], "w_up": _pad_rows(rwkv_w_up[l], LANE).astype(BF16),
            "a0": rwkv_a0[l][None, :], "a_up": _pad_rows(rwkv_a_up[l], LANE).astype(BF16),
            "g_up": _pad_rows(rwkv_g_up[l], 2 * LANE).astype(BF16),
            "k_k": rwkv_k_k[l][None, :], "k_a": rwkv_k_a[l][None, :], "r_k": rwkv_r_k[l][None, :],
            "gn_w": rwkv_gn_w[l][None, :], "gn_b": rwkv_gn_b[l][None, :],
        }
        if l > 0:
            rprm["v0"] = rwkv_v0[l - 1][None, :]
            rprm["v_down"] = _pad_cols(rwkv_v_down[l - 1], LANE).astype(BF16)
            rprm["v_up"] = _pad_rows(rwkv_v_up[l - 1], LANE).astype(BF16)
        y_r, vfirst = _rwkv_mixer(p_r.reshape(bsz, seq, RW_WIDTH), vfirst, rprm, has_vres=l > 0)
        gprm = {"gk_up": _pad_rows(gla_gk_up[l], LANE).astype(BF16), "gk_b": gla_gk_b[l][None, :],
                "norm_w": jnp.tile(gla_norm_w[l], GLA_HEADS)[None, :]}
        y_g = _gla_mixer(p_g.reshape(bsz, seq, GL_WIDTH), gprm)
        x1, x1p, idx, gate, rank, cnt = _outproj_router(
            y_r.reshape(t, D_RWKV), y_g.reshape(t, D_GLA), xf, w_out[l].astype(BF16),
            ln1_w[l][None, :], ln1_b[l][None, :], router_w[l].T, router_b[l][:, None])
        nxt = packed[l + 1][:2] if l + 1 < DEPTH else None
        res = _moe(x1, x1p, idx, gate, rank, cnt[:, 0], l, exp_w1, exp_b1[:, :, None, :],
                   exp_w2, exp_b2[:, :, None, :], ln2_w[l][None, :], ln2_b[l][None, :], nxt)
        if nxt is not None:
            xf, p_r, p_g = res
        else:
            xf = res[0]
    return xf.reshape(bsz, seq, d)
```

```python
import functools

import jax
import jax.numpy as jnp
from jax import lax
from jax.experimental import pallas as pl
from jax.experimental.pallas import tpu as pltpu

F32 = jnp.float32
BF16 = jnp.bfloat16
HIGHEST = lax.Precision.HIGHEST

D_MODEL = 1024
DEPTH = 2
CHUNK = 64
RWKV_HEAD = 64
D_RWKV = 512
RWKV_HEADS = 8
W_LORA = 64
A_LORA = 64
V_LORA = 32
G_LORA = 160
RWKV_GN_EPS = 64e-5
D_GLA = 512
GLA_HEADS = 4
GLA_DV = 128
GLA_DK = 64
GLA_GATE_RANK = 16
GLA_GATE_NORMALIZER = 16.0
RMS_EPS = 1e-6
N_EXPERTS = 32
TOP_K = 4
D_FF = 1024
SWIGLU_LIMIT = 7.0
SWIGLU_ALPHA = 1.702
MOE_BLOCK = 512
DEEPNORM_ALPHA = (2 * DEPTH) ** 0.25
LN_EPS = 1e-5

LANE = 128
SUBLANE = 8
ROW_R = D_MODEL // LANE
assert ROW_R == SUBLANE
RW_WD = 3 * D_RWKV
RW_AD = RW_WD + LANE
RW_GD = RW_AD + LANE
RW_WIDTH = RW_GD + 2 * LANE
GL_K = GLA_HEADS * GLA_DK
GL_V = 2 * GL_K
GL_GK = GL_V + D_GLA
GL_G = GL_GK + LANE
GL_WIDTH = GL_G + D_GLA

VMEM_LIMIT = 56 * 1024 * 1024


def _cparams(n_axes):
    return pltpu.CompilerParams(dimension_semantics=("arbitrary",) * n_axes,
                                vmem_limit_bytes=VMEM_LIMIT)


def _dot(a, b):
    return jnp.dot(a, b, preferred_element_type=F32)


def _dot_nt(a, b):
    return lax.dot_general(a, b, (((1,), (1,)), ((), ())), preferred_element_type=F32)


def _dot_tn(a, b):
    return lax.dot_general(a, b, (((0,), (0,)), ((), ())), preferred_element_type=F32)


def _split_bf16(x):
    hi = x.astype(BF16)
    lo = (x - hi.astype(F32)).astype(BF16)
    return hi, lo


def _layer_norm(z, w, b):
    mu = jnp.mean(z, axis=-1, keepdims=True)
    zc = z - mu
    var = jnp.mean(zc * zc, axis=-1, keepdims=True)
    return zc * lax.rsqrt(var + LN_EPS) * w + b


def _sigmoid(x):
    return 1.0 / (1.0 + jnp.exp(-x))


def _softplus(x):
    return jnp.maximum(x, 0.0) + jnp.log(1.0 + jnp.exp(-jnp.abs(x)))


def _rows_load(ref, n):
    return jnp.concatenate([ref[pl.ds(s, n, stride=ROW_R), :] for s in range(ROW_R)], axis=1)


def _rows_store(ref, val):
    for s in range(ROW_R):
        ref[pl.ds(s, val.shape[0], stride=ROW_R), :] = val[:, s * LANE:(s + 1) * LANE]


def _row_tile(ref, row):
    return ref.at[pl.ds(pl.multiple_of(row * ROW_R, ROW_R), ROW_R), :]


def _tril_mask(n, strict):
    r = lax.broadcasted_iota(jnp.int32, (n, n), 0)
    c = lax.broadcasted_iota(jnp.int32, (n, n), 1)
    return (c < r) if strict else (c <= r)


def _inproj_kernel(x_ref, lnw_ref, lnb_ref, wr_ref, wg_ref, x0_ref, pr_ref, pg_ref):
    x = _layer_norm(x_ref[...], lnw_ref[...], lnb_ref[...])
    x0_ref[...] = x
    xb = x.astype(BF16)
    pr_ref[...] = _dot(xb, wr_ref[...]).astype(BF16)
    pg_ref[...] = _dot(xb, wg_ref[...]).astype(BF16)


def _inproj(x, lnw, lnb, wr, wg):
    t = x.shape[0]
    tm = min(512, t)
    row = lambda i: (i, 0)
    const = lambda i: (0, 0)
    return pl.pallas_call(
        _inproj_kernel,
        grid=(t // tm,),
        in_specs=[pl.BlockSpec((tm, D_MODEL), row),
                  pl.BlockSpec((1, D_MODEL), const), pl.BlockSpec((1, D_MODEL), const),
                  pl.BlockSpec((D_MODEL, RW_WIDTH), const), pl.BlockSpec((D_MODEL, GL_WIDTH), const)],
        out_specs=[pl.BlockSpec((tm, D_MODEL), row), pl.BlockSpec((tm, RW_WIDTH), row),
                   pl.BlockSpec((tm, GL_WIDTH), row)],
        out_shape=[jax.ShapeDtypeStruct((t, D_MODEL), F32), jax.ShapeDtypeStruct((t, RW_WIDTH), BF16),
                   jax.ShapeDtypeStruct((t, GL_WIDTH), BF16)],
        compiler_params=_cparams(1),
        name="inproj_ln",
    )(x, lnw, lnb, wr, wg)


def _head_sum(x, ones_pair):
    n, tiles = x.shape[0], x.shape[1] // LANE
    st = jnp.concatenate([x[:, j * LANE:(j + 1) * LANE] for j in range(tiles)], axis=0)
    hi, lo = _split_bf16(st)
    s = _dot(jnp.concatenate([hi, lo], axis=0), ones_pair)
    s = s[:tiles * n] + s[tiles * n:]
    return jnp.concatenate([s[j * n:(j + 1) * n] for j in range(tiles)], axis=1)


def _rwkv_kernel(has_vres, *refs):
    if has_vres:
        (p_ref, vfirst_ref, mu_ref, w0_ref, wup_ref, a0_ref, aup_ref, gup_ref, kk_ref, ka_ref, rk_ref,
         gnw_ref, gnb_ref, v0_ref, vdown_ref, vup_ref, y_ref, state_ref, prev_ref) = refs
    else:
        (p_ref, mu_ref, w0_ref, wup_ref, a0_ref, aup_ref, gup_ref, kk_ref, ka_ref, rk_ref,
         gnw_ref, gnb_ref, y_ref, vfirst_out_ref, state_ref, prev_ref) = refs

    @pl.when(pl.program_id(1) == 0)
    def _():
        state_ref[...] = jnp.zeros_like(state_ref)
        prev_ref[...] = jnp.zeros_like(prev_ref)

    p = p_ref[0].astype(F32)
    n_tok = p.shape[0]
    row = lax.broadcasted_iota(jnp.int32, (n_tok, 1), 0)
    prev = jnp.where(row == 0, prev_ref[...], pltpu.roll(p, 1, axis=0))
    prev_ref[...] = p[n_tok - 1:n_tok, :]
    xs = p + (prev - p) * mu_ref[...]
    r = xs[:, 0:D_RWKV]
    k = xs[:, D_RWKV:2 * D_RWKV]
    v = xs[:, 2 * D_RWKV:3 * D_RWKV]
    wd = xs[:, RW_WD:RW_AD]
    ad = xs[:, RW_AD:RW_GD]
    gd = xs[:, RW_GD:RW_WIDTH]

    w = w0_ref[...] + _dot(jnp.tanh(wd).astype(BF16), wup_ref[...])
    w = -_softplus(-w) - 0.5
    logw = -jnp.exp(w)
    a = _sigmoid(a0_ref[...] + _dot(ad.astype(BF16), aup_ref[...]))
    g = _dot(_sigmoid(gd).astype(BF16), gup_ref[...])
    if has_vres:
        vmix = _dot(_dot(v.astype(BF16), vdown_ref[...]).astype(BF16), vup_ref[...])
        v = v + (vfirst_ref[0] - v) * _sigmoid(v0_ref[...] + vmix)
    else:
        vfirst_out_ref[0] = v

    hr = lax.broadcasted_iota(jnp.int32, (LANE, LANE), 0) // RWKV_HEAD
    hc = lax.broadcasted_iota(jnp.int32, (LANE, LANE), 1) // RWKV_HEAD
    ones_blk = (hr == hc).astype(BF16)

    kk = k * kk_ref[...]
    kk = kk * lax.rsqrt(jnp.maximum(_head_sum(kk * kk, ones_blk), 1e-24))
    k = k * (1.0 + (a - 1.0) * ka_ref[...])
    bonus = _head_sum(r * k * rk_ref[...], ones_blk)

    n_chunks = n_tok // CHUNK
    n_pairs = RWKV_HEADS // 2
    assert 2 * RWKV_HEAD == LANE and CHUNK == RWKV_HEAD
    rr = lax.broadcasted_iota(jnp.int32, (CHUNK, CHUNK), 0)
    cc = lax.broadcasted_iota(jnp.int32, (CHUNK, CHUNK), 1)
    tril_f = (cc <= rr).astype(F32)
    r_p = lax.broadcasted_iota(jnp.int32, (CHUNK, LANE), 0)
    l_p = lax.broadcasted_iota(jnp.int32, (CHUNK, LANE), 1)
    c_p = jnp.where(l_p >= RWKV_HEAD, l_p - RWKV_HEAD, l_p)
    m_left = (l_p < RWKV_HEAD).astype(F32)
    m_right = (l_p >= RWKV_HEAD).astype(F32)
    m_left_b = m_left.astype(BF16)
    m_right_b = m_right.astype(BF16)
    m_strict = (c_p < r_p).astype(F32)
    m_lower = (c_p <= r_p).astype(F32)
    eye_p = (c_p == r_p).astype(F32)
    m_diag2 = ((r_p // 2) == (c_p // 2)).astype(F32)
    lvl_masks = []
    size = 2
    while size < CHUNK:
        lvl_masks.append((((r_p // (2 * size)) == (c_p // (2 * size))) & ((r_p // size) != (c_p // size))).astype(F32))
        size *= 2

    def blockdiag(xb):
        return jnp.concatenate([xb * m_left_b, xb * m_right_b], axis=0)

    items = []
    g_ends = []
    for c in range(n_chunks):
        sl = slice(c * CHUNK, (c + 1) * CHUNK)
        lw = logw[sl]
        b = jnp.dot(tril_f, lw, precision=HIGHEST, preferred_element_type=F32)
        eb = jnp.exp(b)
        enb = jnp.exp(-b)
        ebx = jnp.exp(b - lw)
        rh = (r[sl] * eb).astype(BF16)
        kh = (k[sl] * enb).astype(BF16)
        ah = (-kk[sl] * ebx).astype(BF16)
        bh = (kk[sl] * a[sl] * enb).astype(BF16)
        vb = v[sl].astype(BF16)
        g_ends.append(eb[CHUNK - 1:CHUNK, :])
        for j in range(n_pairs):
            ls = slice(j * LANE, (j + 1) * LANE)
            items.append(dict(ah=ah[:, ls], rh=rh[:, ls], bh=bh[:, ls], kh=kh[:, ls], v=vb[:, ls]))

    for it in items:
        ar = jnp.concatenate([it["ah"], it["rh"]], axis=0)
        it["sb"] = _dot_nt(ar, blockdiag(it["bh"]))
        it["sk"] = _dot_nt(ar, blockdiag(it["kh"]))
    for it in items:
        it["a_ab"] = it["sb"][:CHUNK] * m_strict
        it["ak"] = (it["sk"][:CHUNK] * m_strict).astype(BF16)
        it["rbk"] = jnp.concatenate([(it["sb"][CHUNK:] * m_lower).astype(BF16),
                                     (it["sk"][CHUNK:] * m_lower).astype(BF16)], axis=1)
        it["d"] = eye_p + it["a_ab"] * m_diag2
        del it["sb"], it["sk"]
    for mk in lvl_masks:
        for it in items:
            it["db"] = it["d"].astype(BF16)
            it["m"] = _dot((it["a_ab"] * mk).astype(BF16), blockdiag(it["db"]))
        for it in items:
            it["d"] = it["d"] + _dot(it["db"], blockdiag(it["m"].astype(BF16)))
    for it in items:
        it["akv"] = _dot(it["ak"], blockdiag(it["v"]))
        it["tinv"] = it["d"].astype(BF16)
    for it in items:
        it["w"] = _dot(it["tinv"], blockdiag(it["ah"]))
        it["ut"] = _dot(it["tinv"], blockdiag(it["akv"].astype(BF16)))

    states = [state_ref[j] for j in range(n_pairs)]
    y_chunks = []
    for c in range(n_chunks):
        its = items[c * n_pairs:(c + 1) * n_pairs]
        for j, it in enumerate(its):
            lhs = jnp.concatenate([it["w"].astype(BF16), it["rh"]], axis=0)
            it["ws"] = _dot_nt(lhs, blockdiag(states[j].astype(BF16)))
        for j, it in enumerate(its):
            it["ub"] = (it["ws"][:CHUNK] + it["ut"]).astype(BF16)
            uv = jnp.concatenate([it["ub"], it["v"]], axis=0)
            upd = _dot_tn(uv, jnp.concatenate([it["bh"], it["kh"]], axis=0))
            ls = slice(j * LANE, (j + 1) * LANE)
            states[j] = (states[j] + upd[:RWKV_HEAD] * m_left + upd[RWKV_HEAD:] * m_right) * g_ends[c][:, ls]
        y_pairs = [it["ws"][CHUNK:] + _dot(it["rbk"], jnp.concatenate([blockdiag(it["ub"]), blockdiag(it["v"])], axis=0))
                   for it in its]
        y_chunks.append(jnp.concatenate(y_pairs, axis=1))
    for j in range(n_pairs):
        state_ref[j] = states[j]
    y = jnp.concatenate(y_chunks, axis=0) if len(y_chunks) > 1 else y_chunks[0]

    inv_n = 1.0 / RWKV_HEAD
    m = _head_sum(y, ones_blk) * inv_n
    yc = y - m
    var = _head_sum(yc * yc, ones_blk) * inv_n
    y = yc * lax.rsqrt(var + RWKV_GN_EPS) * gnw_ref[...] + gnb_ref[...]
    y = y + bonus * v
    y_ref[0] = (y * g).astype(BF16)


def _rwkv_mixer(p_r, vfirst, prm, has_vres):
    bsz, seq, _ = p_r.shape
    n_tok = min(512, seq)
    tok = lambda b, i: (b, i, 0)
    const = lambda b, i: (0, 0)
    vec = pl.BlockSpec((1, D_RWKV), const)
    in_specs = [pl.BlockSpec((1, n_tok, RW_WIDTH), tok)]
    args = [p_r]
    if has_vres:
        in_specs.append(pl.BlockSpec((1, n_tok, D_RWKV), tok))
        args.append(vfirst)
    in_specs += [pl.BlockSpec((1, RW_WIDTH), const), vec, pl.BlockSpec((LANE, D_RWKV), const), vec,
                 pl.BlockSpec((LANE, D_RWKV), const), pl.BlockSpec((2 * LANE, D_RWKV), const),
                 vec, vec, vec, vec, vec]
    args += [prm["mu"], prm["w0"], prm["w_up"], prm["a0"], prm["a_up"], prm["g_up"],
             prm["k_k"], prm["k_a"], prm["r_k"], prm["gn_w"], prm["gn_b"]]
    out_shape = [jax.ShapeDtypeStruct((bsz, seq, D_RWKV), BF16)]
    out_specs = [pl.BlockSpec((1, n_tok, D_RWKV), tok)]
    if has_vres:
        in_specs += [vec, pl.BlockSpec((D_RWKV, LANE), const), pl.BlockSpec((LANE, D_RWKV), const)]
        args += [prm["v0"], prm["v_down"], prm["v_up"]]
    else:
        out_shape.append(jax.ShapeDtypeStruct((bsz, seq, D_RWKV), F32))
        out_specs.append(pl.BlockSpec((1, n_tok, D_RWKV), tok))
    res = pl.pallas_call(
        functools.partial(_rwkv_kernel, has_vres),
        grid=(bsz, seq // n_tok),
        in_specs=in_specs,
        out_specs=out_specs,
        out_shape=out_shape,
        scratch_shapes=[pltpu.VMEM((RWKV_HEADS // 2, RWKV_HEAD, 2 * RWKV_HEAD), F32),
                        pltpu.VMEM((1, RW_WIDTH), F32)],
        compiler_params=_cparams(2),
        name="rwkv7_vres" if has_vres else "rwkv7",
    )(*args)
    if has_vres:
        return res[0], vfirst
    return res[0], res[1]


def _gla_kernel(p_ref, gkup_ref, gkb_ref, nw_ref, y_ref, state_ref):
    @pl.when(pl.program_id(1) == 0)
    def _():
        state_ref[...] = jnp.zeros_like(state_ref)

    p = p_ref[0].astype(F32)
    n_tok = p.shape[0]
    q = p[:, 0:GL_K]
    k = p[:, GL_K:GL_V]
    v = p[:, GL_V:GL_GK]
    gkd = p[:, GL_GK:GL_G]
    g = p[:, GL_G:GL_WIDTH]
    z = _dot(gkd.astype(BF16), gkup_ref[...]) + gkb_ref[...]
    gk = -_softplus(-z) * (1.0 / GLA_GATE_NORMALIZER)

    lower = _tril_mask(CHUNK, strict=False)
    tril_f = lower.astype(F32)
    n_chunks = n_tok // CHUNK
    items = []
    decs = []
    for c in range(n_chunks):
        sl = slice(c * CHUNK, (c + 1) * CHUNK)
        b = jnp.dot(tril_f, gk[sl], precision=HIGHEST, preferred_element_type=F32)
        b_last = b[CHUNK - 1:CHUNK, :]
        q_e = (q[sl] * jnp.exp(b) * (GLA_DK ** -0.5)).astype(BF16)
        k_e = (k[sl] * jnp.exp(-b)).astype(BF16)
        k_end = (k[sl] * jnp.exp(b_last - b)).astype(BF16)
        decs.append(jnp.exp(b_last))
        vb = v[sl].astype(BF16)
        for h in range(GLA_HEADS):
            ks = slice(h * GLA_DK, (h + 1) * GLA_DK)
            vs = slice(h * GLA_DV, (h + 1) * GLA_DV)
            items.append(dict(q=q_e[:, ks], k=k_e[:, ks], kend=k_end[:, ks], v=vb[:, vs]))
    for it in items:
        it["sc"] = jnp.where(lower, _dot_nt(it["q"], it["k"]), 0.0).astype(BF16)
    for it in items:
        it["intra"] = _dot(it["sc"], it["v"])
        it["kv"] = _dot_tn(it["v"], it["kend"])
    states = [state_ref[h] for h in range(GLA_HEADS)]
    o_chunks = []
    for c in range(n_chunks):
        o_heads = []
        for h in range(GLA_HEADS):
            it = items[c * GLA_HEADS + h]
            ks = slice(h * GLA_DK, (h + 1) * GLA_DK)
            o_h = it["intra"] + _dot_nt(it["q"], states[h].astype(BF16))
            states[h] = states[h] * decs[c][:, ks] + it["kv"]
            o_heads.append(o_h * lax.rsqrt(jnp.mean(o_h * o_h, axis=-1, keepdims=True) + RMS_EPS))
        o_chunks.append(jnp.concatenate(o_heads, axis=1))
    for h in range(GLA_HEADS):
        state_ref[h] = states[h]
    o = jnp.concatenate(o_chunks, axis=0) if len(o_chunks) > 1 else o_chunks[0]
    silu_g = g * _sigmoid(g)
    y_ref[0] = (o * nw_ref[...] * silu_g).astype(BF16)


def _gla_mixer(p_g, prm):
    bsz, seq, _ = p_g.shape
    n_tok = min(512, seq)
    tok = lambda b, i: (b, i, 0)
    const = lambda b, i: (0, 0)
    return pl.pallas_call(
        _gla_kernel,
        grid=(bsz, seq // n_tok),
        in_specs=[pl.BlockSpec((1, n_tok, GL_WIDTH), tok),
                  pl.BlockSpec((LANE, GL_K), const), pl.BlockSpec((1, GL_K), const),
                  pl.BlockSpec((1, D_GLA), const)],
        out_specs=pl.BlockSpec((1, n_tok, D_GLA), tok),
        out_shape=jax.ShapeDtypeStruct((bsz, seq, D_GLA), BF16),
        scratch_shapes=[pltpu.VMEM((GLA_HEADS, GLA_DV, GLA_DK), F32)],
        compiler_params=_cparams(2),
        name="gla",
    )(p_g, prm["gk_up"], prm["gk_b"], prm["norm_w"])


def _outproj_router_kernel(yr_ref, yg_ref, x_ref, wo_ref, lnw_ref, lnb_ref, rw_ref, rb_ref,
                           x1_ref, x1p_ref, idx_ref, gate_ref, rank_ref, cnt_ref, carry_ref):
    @pl.when(pl.program_id(0) == 0)
    def _():
        carry_ref[...] = jnp.zeros_like(carry_ref)

    mix = _dot(yr_ref[...], wo_ref[0:D_RWKV, :]) + _dot(yg_ref[...], wo_ref[D_RWKV:, :])
    x1 = _layer_norm(DEEPNORM_ALPHA * x_ref[...] + mix, lnw_ref[...], lnb_ref[...])
    x1_ref[...] = x1
    _rows_store(x1p_ref, x1)
    tm = x1.shape[0]

    xh, xl = _split_bf16(x1)
    rw = rw_ref[...]
    wh, wl = _split_bf16(rw)
    logits = _dot_nt(wh, xh) + _dot_nt(wh, xl) + _dot_nt(wl, xh) + rb_ref[...]

    e_iota = lax.broadcasted_iota(jnp.int32, (N_EXPERTS, tm), 0)
    work = logits
    vals, idxs = [], []
    member = jnp.zeros((N_EXPERTS, tm), F32)
    for _ in range(TOP_K):
        mx = jnp.max(work, axis=0, keepdims=True)
        ix = jnp.min(jnp.where(work == mx, e_iota, N_EXPERTS), axis=0, keepdims=True)
        sel = e_iota == ix
        work = jnp.where(sel, -jnp.inf, work)
        member = jnp.where(sel, 1.0, member)
        vals.append(mx)
        idxs.append(ix)
    exps = [jnp.exp(vv - vals[0]) for vv in vals]
    inv_den = 1.0 / (exps[0] + exps[1] + exps[2] + exps[3])

    tr = lax.broadcasted_iota(jnp.int32, (tm, tm), 0)
    tc = lax.broadcasted_iota(jnp.int32, (tm, tm), 1)
    before = (tr < tc).astype(BF16)
    cex = _dot(member.astype(BF16), before) + carry_ref[...][:, 0:1]
    for kq in range(TOP_K):
        sel = e_iota == idxs[kq]
        idx_ref[kq:kq + 1, :] = idxs[kq]
        gate_ref[kq:kq + 1, :] = exps[kq] * inv_den
        rank_ref[kq:kq + 1, :] = jnp.sum(jnp.where(sel, cex, 0.0), axis=0, keepdims=True).astype(jnp.int32)
    carry_ref[...] = carry_ref[...] + jnp.sum(member, axis=1, keepdims=True)
    cnt_ref[...] = carry_ref[...].astype(jnp.int32)


def _outproj_router(yr, yg, x, wo, lnw, lnb, rw_t, rb):
    t = x.shape[0]
    tm = min(512, t)
    row = lambda i: (i, 0)
    col = lambda i: (0, i)
    const = lambda i: (0, 0)
    return pl.pallas_call(
        _outproj_router_kernel,
        grid=(t // tm,),
        in_specs=[pl.BlockSpec((tm, D_RWKV), row), pl.BlockSpec((tm, D_GLA), row),
                  pl.BlockSpec((tm, D_MODEL), row), pl.BlockSpec((D_MODEL, D_MODEL), const),
                  pl.BlockSpec((1, D_MODEL), const), pl.BlockSpec((1, D_MODEL), const),
                  pl.BlockSpec((N_EXPERTS, D_MODEL), const), pl.BlockSpec((N_EXPERTS, 1), const)],
        out_specs=[pl.BlockSpec((tm, D_MODEL), row), pl.BlockSpec((tm * ROW_R, LANE), row),
                   pl.BlockSpec((TOP_K, tm), col), pl.BlockSpec((TOP_K, tm), col),
                   pl.BlockSpec((TOP_K, tm), col), pl.BlockSpec((N_EXPERTS, LANE), const)],
        out_shape=[jax.ShapeDtypeStruct((t, D_MODEL), F32), jax.ShapeDtypeStruct((t * ROW_R, LANE), F32),
                   jax.ShapeDtypeStruct((TOP_K, t), jnp.int32), jax.ShapeDtypeStruct((TOP_K, t), F32),
                   jax.ShapeDtypeStruct((TOP_K, t), jnp.int32),
                   jax.ShapeDtypeStruct((N_EXPERTS, LANE), jnp.int32)],
        scratch_shapes=[pltpu.VMEM((N_EXPERTS, LANE), F32)],
        compiler_params=_cparams(1),
        name="outproj_router",
    )(yr, yg, x, wo, lnw, lnb, rw_t, rb)


SEG_BIG = 64
SEG_MID = 8


def _issue_run(n, start_copy):
    n_big = lax.shift_right_logical(n, 6)
    n_mid = lax.shift_right_logical(n, 3) & 7
    n_one = n & 7
    off_mid = n_big * SEG_BIG
    off_one = off_mid + n_mid * SEG_MID
    lax.fori_loop(0, n_big, lambda q, c: (start_copy(q * SEG_BIG, SEG_BIG), c)[1], 0)
    lax.fori_loop(0, n_mid, lambda q, c: (start_copy(off_mid + q * SEG_MID, SEG_MID), c)[1], 0)
    lax.fori_loop(0, n_one, lambda q, c: (start_copy(off_one + q, 1), c)[1], 0)


def _dispatch_kernel(pos_ref, seg_ref, gaps_ref, x_ref, xs_ref, stage_ref, zero_ref, sem, zsem):
    te = x_ref.shape[0] // ROW_R
    i = pl.program_id(0)
    n_tiles = pl.num_programs(0)
    slot = i % 2
    base = i * (TOP_K * te)

    def drain(sl):
        for kq in range(TOP_K):
            pltpu.make_async_copy(x_ref, x_ref, sem.at[sl]).wait()

    @pl.when(pl.program_id(0) == 0)
    def _():
        zero_ref[...] = jnp.zeros_like(zero_ref)

        for g in range(N_EXPERTS + 1):
            lo = gaps_ref[2 * g]

            def zero_copy(off, rows, lo=lo):
                dst = xs_ref.at[pl.ds(pl.multiple_of((lo + off) * ROW_R, ROW_R), rows * ROW_R), :]
                pltpu.make_async_copy(zero_ref.at[pl.ds(0, rows * ROW_R), :], dst, zsem).start()

            _issue_run(gaps_ref[2 * g + 1] - lo, zero_copy)
        blk = pl.ds(0, MOE_BLOCK * ROW_R)
        for g in range(N_EXPERTS):
            pltpu.make_async_copy(xs_ref.at[blk, :], xs_ref.at[blk, :], zsem).wait()

    @pl.when(i >= 2)
    def _():
        drain(slot)

    def place(r, carry):
        row = x_ref[pl.ds(pl.multiple_of(r * ROW_R, ROW_R), ROW_R), :]
        for kq in range(TOP_K):
            p = pos_ref[base + kq * te + r]
            stage_ref[slot, pl.ds(pl.multiple_of(p * ROW_R, ROW_R), ROW_R), :] = row
        return carry

    lax.fori_loop(0, te, place, 0, unroll=4)

    for e in range(N_EXPERTS):
        o = (i * N_EXPERTS + e) * 3
        s, n, g = seg_ref[o], seg_ref[o + 1], seg_ref[o + 2]

        def copy(off, rows, s=s, g=g):
            src = stage_ref.at[slot, pl.ds(pl.multiple_of((s + off) * ROW_R, ROW_R), rows * ROW_R), :]
            dst = xs_ref.at[pl.ds(pl.multiple_of((g + off) * ROW_R, ROW_R), rows * ROW_R), :]
            pltpu.make_async_copy(src, dst, sem.at[slot]).start()

        _issue_run(n, copy)

    @pl.when(i == n_tiles - 1)
    def _():
        @pl.when(i >= 1)
        def _():
            drain(1 - slot)
        drain(slot)


def _dispatch(x1, pos_tiles, seg, gaps, m_pad, te):
    t = x1.shape[0] // ROW_R
    return pl.pallas_call(
        _dispatch_kernel,
        grid_spec=pltpu.PrefetchScalarGridSpec(
            num_scalar_prefetch=3,
            grid=(t // te,),
            in_specs=[pl.BlockSpec((te * ROW_R, LANE), lambda i, p, s, g: (i, 0))],
            out_specs=pl.BlockSpec(memory_space=pl.ANY),
            scratch_shapes=[pltpu.VMEM((2, TOP_K * te * ROW_R, LANE), F32), pltpu.VMEM((SEG_BIG * ROW_R, LANE), F32),
                            pltpu.SemaphoreType.DMA((2,)), pltpu.SemaphoreType.DMA(())],
        ),
        out_shape=jax.ShapeDtypeStruct((m_pad * ROW_R, LANE), F32),
        compiler_params=_cparams(1),
        name="moe_dispatch",
    )(pos_tiles, seg, gaps, x1)


def _expert_kernel(be_ref, nv_ref, xs_ref, w1_ref, b1_ref, w2_ref, b2_ref, ys_ref, act_ref, w1b_ref, w2b_ref):
    j = pl.program_id(0)
    n_valid = nv_ref[j]

    @pl.when((j == 0) | (be_ref[j] != be_ref[jnp.maximum(j - 1, 0)]))
    def _():
        rows = 128
        for c in range(D_MODEL // rows):
            rs = slice(c * rows, (c + 1) * rows)
            w1b_ref[rs, :] = w1_ref[0, 0, rs, :].astype(BF16)
            w2b_ref[rs, :] = w2_ref[0, 0, rs, :].astype(BF16)

    @pl.when(n_valid == 0)
    def _():
        ys_ref[...] = jnp.zeros_like(ys_ref)

    @pl.when(n_valid > 0)
    def _():
        xb = _rows_load(xs_ref, MOE_BLOCK).astype(BF16)
        n_chunk = 256
        for c in range(D_FF // n_chunk):
            cs = slice(c * n_chunk, (c + 1) * n_chunk)
            us = slice(D_FF + c * n_chunk, D_FF + (c + 1) * n_chunk)
            gt = _dot(xb, w1b_ref[:, cs]) + b1_ref[0, 0, :, cs]
            up = _dot(xb, w1b_ref[:, us]) + b1_ref[0, 0, :, us]
            gt = jnp.minimum(gt, SWIGLU_LIMIT)
            up = jnp.clip(up, -SWIGLU_LIMIT, SWIGLU_LIMIT)
            act_ref[:, cs] = ((up + 1.0) * gt * _sigmoid(SWIGLU_ALPHA * gt)).astype(BF16)
        _rows_store(ys_ref, _dot(act_ref[...], w2b_ref[...]) + b2_ref[0, 0])


def _experts(xs, block_e, n_valid, layer, w1, b1, w2, b2):
    m_pad = xs.shape[0] // ROW_R
    n_blocks = m_pad // MOE_BLOCK
    wmap = lambda j, be, nv: (layer, be[j], 0, 0)
    return pl.pallas_call(
        _expert_kernel,
        grid_spec=pltpu.PrefetchScalarGridSpec(
            num_scalar_prefetch=2,
            grid=(n_blocks,),
            in_specs=[pl.BlockSpec((MOE_BLOCK * ROW_R, LANE), lambda j, be, nv: (j, 0)),
                      pl.BlockSpec((1, 1, D_MODEL, 2 * D_FF), wmap),
                      pl.BlockSpec((1, 1, 1, 2 * D_FF), wmap),
                      pl.BlockSpec((1, 1, D_FF, D_MODEL), wmap),
                      pl.BlockSpec((1, 1, 1, D_MODEL), wmap)],
            out_specs=pl.BlockSpec((MOE_BLOCK * ROW_R, LANE), lambda j, be, nv: (j, 0)),
            scratch_shapes=[pltpu.VMEM((MOE_BLOCK, D_FF), BF16),
                            pltpu.VMEM((D_MODEL, 2 * D_FF), BF16), pltpu.VMEM((D_FF, D_MODEL), BF16)],
        ),
        out_shape=jax.ShapeDtypeStruct((m_pad * ROW_R, LANE), F32),
        compiler_params=_cparams(1),
        name="moe_experts",
    )(block_e, n_valid, xs, w1, b1, w2, b2)


def _combine_kernel(with_inproj, dest_ref, ys_ref, x1_ref, gate_ref, lnw_ref, lnb_ref, *refs):
    if with_inproj:
        wr_ref, wg_ref, out_ref, pr_ref, pg_ref, buf_ref, sem = refs
    else:
        out_ref, buf_ref, sem = refs
    te = x1_ref.shape[0]
    i = pl.program_id(0)
    n_tiles = pl.num_programs(0)

    def gather(tile, slot):
        base = tile * (TOP_K * te)

        def issue(r, carry):
            for kq in range(TOP_K):
                d = dest_ref[base + kq * te + r]
                pltpu.make_async_copy(_row_tile(ys_ref, d), _row_tile(buf_ref.at[slot, kq], r),
                                      sem.at[slot]).start(priority=kq % 2)
            return carry

        lax.fori_loop(0, te, issue, 0, unroll=8)

    @pl.when(i == 0)
    def _():
        gather(0, 0)

    for slot in range(2):
        @pl.when((i + 1 < n_tiles) & ((i + 1) % 2 == slot))
        def _():
            gather(i + 1, slot)

    for slot in range(2):
        @pl.when(i % 2 == slot)
        def _():
            for kq in range(TOP_K):
                pltpu.make_async_copy(buf_ref.at[slot, kq], buf_ref.at[slot, kq], sem.at[slot]).wait()
            gates = gate_ref[...]
            z = DEEPNORM_ALPHA * x1_ref[...]
            for kq in range(TOP_K):
                z = z + _rows_load(buf_ref.at[slot, kq], te) * gates[:, kq:kq + 1]
            xn = _layer_norm(z, lnw_ref[...], lnb_ref[...])
            out_ref[...] = xn
            if with_inproj:
                xb = xn.astype(BF16)
                pr_ref[...] = _dot(xb, wr_ref[...]).astype(BF16)
                pg_ref[...] = _dot(xb, wg_ref[...]).astype(BF16)


def _combine(ys, dest_tiles, x1, gates_t, lnw, lnb, te, next_inproj=None):
    t = x1.shape[0]
    row = lambda i, d: (i, 0)
    const = lambda i, d: (0, 0)
    in_specs = [pl.BlockSpec(memory_space=pl.ANY),
                pl.BlockSpec((te, D_MODEL), row),
                pl.BlockSpec((te, TOP_K), row),
                pl.BlockSpec((1, D_MODEL), const),
                pl.BlockSpec((1, D_MODEL), const)]
    out_specs = [pl.BlockSpec((te, D_MODEL), row)]
    out_shape = [jax.ShapeDtypeStruct((t, D_MODEL), F32)]
    args = [dest_tiles, ys, x1, gates_t, lnw, lnb]
    if next_inproj is not None:
        in_specs += [pl.BlockSpec((D_MODEL, RW_WIDTH), const), pl.BlockSpec((D_MODEL, GL_WIDTH), const)]
        out_specs += [pl.BlockSpec((te, RW_WIDTH), row), pl.BlockSpec((te, GL_WIDTH), row)]
        out_shape += [jax.ShapeDtypeStruct((t, RW_WIDTH), BF16), jax.ShapeDtypeStruct((t, GL_WIDTH), BF16)]
        args += list(next_inproj)
    return pl.pallas_call(
        functools.partial(_combine_kernel, next_inproj is not None),
        grid_spec=pltpu.PrefetchScalarGridSpec(
            num_scalar_prefetch=1,
            grid=(t // te,),
            in_specs=in_specs,
            out_specs=out_specs,
            scratch_shapes=[pltpu.VMEM((2, TOP_K, te * ROW_R, LANE), F32), pltpu.SemaphoreType.DMA((2,))],
        ),
        out_shape=out_shape,
        compiler_params=_cparams(1),
        name="moe_combine_inproj" if next_inproj is not None else "moe_combine",
    )(*args)


def _moe(x1, x1p, idx, gate, rank, counts, layer, w1, b1, w2, b2, lnw, lnb, next_inproj=None):
    t = x1.shape[0]
    te = min(256, t)
    assert (t * TOP_K) % MOE_BLOCK == 0
    n_blocks = (t * TOP_K) // MOE_BLOCK + N_EXPERTS
    m_pad = n_blocks * MOE_BLOCK
    padded = ((counts + MOE_BLOCK - 1) // MOE_BLOCK) * MOE_BLOCK
    pend = jnp.cumsum(padded)
    pstart = pend - padded
    dest = rank
    for e in range(N_EXPERTS):
        dest = dest + jnp.where(idx == e, pstart[e], 0)
    dest_tiles = dest.reshape(TOP_K, t // te, te).transpose(1, 0, 2).reshape(-1)
    blk_start = jnp.arange(n_blocks, dtype=jnp.int32) * MOE_BLOCK
    block_e = jnp.minimum(jnp.sum(pend[None, :] <= blk_start[:, None], axis=1), N_EXPERTS - 1).astype(jnp.int32)
    is_e = block_e[:, None] == jnp.arange(N_EXPERTS, dtype=jnp.int32)[None, :]
    blk_end = jnp.sum(jnp.where(is_e, (pstart + counts)[None, :], 0), axis=1)
    n_valid = jnp.clip(blk_end - blk_start, 0, MOE_BLOCK).astype(jnp.int32)

    gap_lo = jnp.concatenate([pstart + counts, pend[-1:]])
    gap_hi = jnp.concatenate([pend, jnp.full((1,), m_pad, pend.dtype)])
    gaps = jnp.stack([gap_lo, gap_hi], axis=1).reshape(-1).astype(jnp.int32)

    td = min(512, t)
    n_td = t // td
    idx_t = idx.reshape(TOP_K, n_td, td)
    e_ids = jnp.arange(N_EXPERTS, dtype=jnp.int32)
    tcnt = jnp.sum((idx_t[..., None] == e_ids).astype(jnp.int32), axis=(0, 2))
    cum = jnp.cumsum(tcnt, axis=0) - tcnt
    toff = jnp.cumsum(tcnt, axis=1) - tcnt
    seg = jnp.stack([toff, tcnt, pstart[None, :] + cum], axis=-1).reshape(-1).astype(jnp.int32)
    delta = toff - cum
    pos = rank.reshape(TOP_K, n_td, td)
    for e in range(N_EXPERTS):
        pos = pos + jnp.where(idx_t == e, delta[None, :, e, None], 0)
    pos_tiles = pos.transpose(1, 0, 2).reshape(-1).astype(jnp.int32)

    xs = _dispatch(x1p, pos_tiles, seg, gaps, m_pad, td)
    ys = _experts(xs, block_e, n_valid, layer, w1, b1, w2, b2)
    return _combine(ys, dest_tiles, x1, gate.T, lnw, lnb, te, next_inproj)


def _pad_cols(w, width):
    return jnp.pad(w, ((0, 0), (0, width - w.shape[1])))


def _pad_rows(w, height):
    return jnp.pad(w, ((0, height - w.shape[0]), (0, 0)))


def _pack_inproj(w_in_l, mu_l):
    d3 = 3 * D_RWKV
    o_ad = d3 + W_LORA
    o_gd = o_ad + A_LORA
    o_gla = o_gd + G_LORA
    wr = jnp.concatenate([w_in_l[:, :d3],
                          _pad_cols(w_in_l[:, d3:o_ad], LANE),
                          _pad_cols(w_in_l[:, o_ad:o_gd], LANE),
                          _pad_cols(w_in_l[:, o_gd:o_gla], 2 * LANE)], axis=1)
    mu = mu_l[None, :]
    mu_p = jnp.concatenate([mu[:, :d3], _pad_cols(mu[:, d3:o_ad], LANE), _pad_cols(mu[:, o_ad:o_gd], LANE),
                            _pad_cols(mu[:, o_gd:o_gla], 2 * LANE)], axis=1)
    g = w_in_l[:, o_gla:]
    wg = jnp.concatenate([g[:, :GL_GK], _pad_cols(g[:, GL_GK:GL_GK + GLA_GATE_RANK], LANE),
                          g[:, GL_GK + GLA_GATE_RANK:]], axis=1)
    return wr.astype(BF16), wg.astype(BF16), mu_p


def kernel(x, ln_in_w, ln_in_b, w_in, rwkv_mu, rwkv_w0, rwkv_w_up, rwkv_a0, rwkv_a_up, rwkv_g_up, rwkv_k_k, rwkv_k_a, rwkv_r_k, rwkv_gn_w, rwkv_gn_b, rwkv_v0, rwkv_v_down, rwkv_v_up, gla_gk_up, gla_gk_b, gla_norm_w, w_out, ln1_w, ln1_b, router_w, router_b, exp_w1, exp_b1, exp_w2, exp_b2, ln2_w, ln2_b):
    bsz, seq, d = x.shape
    t = bsz * seq
    xf = x.reshape(t, d)
    vfirst = None
    packed = [_pack_inproj(w_in[l], rwkv_mu[l]) for l in range(DEPTH)]
    xf, p_r, p_g = _inproj(xf, ln_in_w[None, :], ln_in_b[None, :], packed[0][0], packed[0][1])
    for l in range(DEPTH):
        mu_p = packed[l][2]
        rprm = {
            "mu": mu_p, "w0": rwkv_w0[l][None, :], "w_up": _pad_rows(rwkv_w_up[l], LANE).astype(BF16),
            "a0": rwkv_a0[l][None, :], "a_up": _pad_rows(rwkv_a_up[l], LANE).astype(BF16),
            "g_up": _pad_rows(rwkv_g_up[l], 2 * LANE).astype(BF16),
            "k_k": rwkv_k_k[l][None, :], "k_a": rwkv_k_a[l][None, :], "r_k": rwkv_r_k[l][None, :],
            "gn_w": rwkv_gn_w[l][None, :], "gn_b": rwkv_gn_b[l][None, :],
        }
        if l > 0:
            rprm["v0"] = rwkv_v0[l - 1][None, :]
            rprm["v_down"] = _pad_cols(rwkv_v_down[l - 1], LANE).astype(BF16)
            rprm["v_up"] = _pad_rows(rwkv_v_up[l - 1], LANE).astype(BF16)
        y_r, vfirst = _rwkv_mixer(p_r.reshape(bsz, seq, RW_WIDTH), vfirst, rprm, has_vres=l > 0)
        gprm = {"gk_up": _pad_rows(gla_gk_up[l], LANE).astype(BF16), "gk_b": gla_gk_b[l][None, :],
                "norm_w": jnp.tile(gla_norm_w[l], GLA_HEADS)[None, :]}
        y_g = _gla_mixer(p_g.reshape(bsz, seq, GL_WIDTH), gprm)
        x1, x1p, idx, gate, rank, cnt = _outproj_router(
            y_r.reshape(t, D_RWKV), y_g.reshape(t, D_GLA), xf, w_out[l].astype(BF16),
            ln1_w[l][None, :], ln1_b[l][None, :], router_w[l].T, router_b[l][:, None])
        nxt = packed[l + 1][:2] if l + 1 < DEPTH else None
        res = _moe(x1, x1p, idx, gate, rank, cnt[:, 0], l, exp_w1, exp_b1[:, :, None, :],
                   exp_w2, exp_b2[:, :, None, :], ln2_w[l][None, :], ln2_b[l][None, :], nxt)
        if nxt is not None:
            xf, p_r, p_g = res
        else:
            xf = res[0]
    return xf.reshape(bsz, seq, d)
```

```python
import functools

import jax
import jax.numpy as jnp
from jax import lax
from jax.experimental import pallas as pl
from jax.experimental.pallas import tpu as pltpu

F32 = jnp.float32
BF16 = jnp.bfloat16
HIGHEST = lax.Precision.HIGHEST

D_MODEL = 1024
DEPTH = 2
CHUNK = 64
RWKV_HEAD = 64
D_RWKV = 512
RWKV_HEADS = 8
W_LORA = 64
A_LORA = 64
V_LORA = 32
G_LORA = 160
RWKV_GN_EPS = 64e-5
D_GLA = 512
GLA_HEADS = 4
GLA_DV = 128
GLA_DK = 64
GLA_GATE_RANK = 16
GLA_GATE_NORMALIZER = 16.0
RMS_EPS = 1e-6
N_EXPERTS = 32
TOP_K = 4
D_FF = 1024
SWIGLU_LIMIT = 7.0
SWIGLU_ALPHA = 1.702
MOE_BLOCK = 512
DEEPNORM_ALPHA = (2 * DEPTH) ** 0.25
LN_EPS = 1e-5

LANE = 128
SUBLANE = 8
ROW_R = D_MODEL // LANE
assert ROW_R == SUBLANE
RW_WD = 3 * D_RWKV
RW_AD = RW_WD + LANE
RW_GD = RW_AD + LANE
RW_WIDTH = RW_GD + 2 * LANE
GL_K = GLA_HEADS * GLA_DK
GL_V = 2 * GL_K
GL_GK = GL_V + D_GLA
GL_G = GL_GK + LANE
GL_WIDTH = GL_G + D_GLA

VMEM_LIMIT = 56 * 1024 * 1024


def _cparams(n_axes):
    return pltpu.CompilerParams(dimension_semantics=("arbitrary",) * n_axes,
                                vmem_limit_bytes=VMEM_LIMIT)


def _dot(a, b):
    return jnp.dot(a, b, preferred_element_type=F32)


def _dot_nt(a, b):
    return lax.dot_general(a, b, (((1,), (1,)), ((), ())), preferred_element_type=F32)


def _dot_tn(a, b):
    return lax.dot_general(a, b, (((0,), (0,)), ((), ())), preferred_element_type=F32)


def _split_bf16(x):
    hi = x.astype(BF16)
    lo = (x - hi.astype(F32)).astype(BF16)
    return hi, lo


def _layer_norm(z, w, b):
    mu = jnp.mean(z, axis=-1, keepdims=True)
    zc = z - mu
    var = jnp.mean(zc * zc, axis=-1, keepdims=True)
    return zc * lax.rsqrt(var + LN_EPS) * w + b


def _sigmoid(x):
    return 1.0 / (1.0 + jnp.exp(-x))


def _softplus(x):
    return jnp.maximum(x, 0.0) + jnp.log(1.0 + jnp.exp(-jnp.abs(x)))


def _rows_load(ref, n):
    return jnp.concatenate([ref[pl.ds(s, n, stride=ROW_R), :] for s in range(ROW_R)], axis=1)


def _rows_store(ref, val):
    for s in range(ROW_R):
        ref[pl.ds(s, val.shape[0], stride=ROW_R), :] = val[:, s * LANE:(s + 1) * LANE]


def _row_tile(ref, row):
    return ref.at[pl.ds(pl.multiple_of(row * ROW_R, ROW_R), ROW_R), :]


def _tril_mask(n, strict):
    r = lax.broadcasted_iota(jnp.int32, (n, n), 0)
    c = lax.broadcasted_iota(jnp.int32, (n, n), 1)
    return (c < r) if strict else (c <= r)


def _inproj_kernel(x_ref, lnw_ref, lnb_ref, wr_ref, wg_ref, x0_ref, pr_ref, pg_ref):
    x = _layer_norm(x_ref[...], lnw_ref[...], lnb_ref[...])
    x0_ref[...] = x
    xb = x.astype(BF16)
    pr_ref[...] = _dot(xb, wr_ref[...]).astype(BF16)
    pg_ref[...] = _dot(xb, wg_ref[...]).astype(BF16)


def _inproj(x, lnw, lnb, wr, wg):
    t = x.shape[0]
    tm = min(512, t)
    row = lambda i: (i, 0)
    const = lambda i: (0, 0)
    return pl.pallas_call(
        _inproj_kernel,
        grid=(t // tm,),
        in_specs=[pl.BlockSpec((tm, D_MODEL), row),
                  pl.BlockSpec((1, D_MODEL), const), pl.BlockSpec((1, D_MODEL), const),
                  pl.BlockSpec((D_MODEL, RW_WIDTH), const), pl.BlockSpec((D_MODEL, GL_WIDTH), const)],
        out_specs=[pl.BlockSpec((tm, D_MODEL), row), pl.BlockSpec((tm, RW_WIDTH), row),
                   pl.BlockSpec((tm, GL_WIDTH), row)],
        out_shape=[jax.ShapeDtypeStruct((t, D_MODEL), F32), jax.ShapeDtypeStruct((t, RW_WIDTH), BF16),
                   jax.ShapeDtypeStruct((t, GL_WIDTH), BF16)],
        compiler_params=_cparams(1),
        name="inproj_ln",
    )(x, lnw, lnb, wr, wg)


def _head_sum(x, ones_pair):
    n, tiles = x.shape[0], x.shape[1] // LANE
    st = jnp.concatenate([x[:, j * LANE:(j + 1) * LANE] for j in range(tiles)], axis=0)
    hi, lo = _split_bf16(st)
    s = _dot(jnp.concatenate([hi, lo], axis=0), ones_pair)
    s = s[:tiles * n] + s[tiles * n:]
    return jnp.concatenate([s[j * n:(j + 1) * n] for j in range(tiles)], axis=1)


def _rwkv_kernel(has_vres, *refs):
    if has_vres:
        (p_ref, vfirst_ref, mu_ref, w0_ref, wup_ref, a0_ref, aup_ref, gup_ref, kk_ref, ka_ref, rk_ref,
         gnw_ref, gnb_ref, v0_ref, vdown_ref, vup_ref, y_ref, state_ref, prev_ref) = refs
    else:
        (p_ref, mu_ref, w0_ref, wup_ref, a0_ref, aup_ref, gup_ref, kk_ref, ka_ref, rk_ref,
         gnw_ref, gnb_ref, y_ref, vfirst_out_ref, state_ref, prev_ref) = refs

    @pl.when(pl.program_id(1) == 0)
    def _():
        state_ref[...] = jnp.zeros_like(state_ref)
        prev_ref[...] = jnp.zeros_like(prev_ref)

    p = p_ref[0].astype(F32)
    n_tok = p.shape[0]
    row = lax.broadcasted_iota(jnp.int32, (n_tok, 1), 0)
    prev = jnp.where(row == 0, prev_ref[...], pltpu.roll(p, 1, axis=0))
    prev_ref[...] = p[n_tok - 1:n_tok, :]
    xs = p + (prev - p) * mu_ref[...]
    r = xs[:, 0:D_RWKV]
    k = xs[:, D_RWKV:2 * D_RWKV]
    v = xs[:, 2 * D_RWKV:3 * D_RWKV]
    wd = xs[:, RW_WD:RW_AD]
    ad = xs[:, RW_AD:RW_GD]
    gd = xs[:, RW_GD:RW_WIDTH]

    w = w0_ref[...] + _dot(jnp.tanh(wd).astype(BF16), wup_ref[...])
    w = -_softplus(-w) - 0.5
    logw = -jnp.exp(w)
    a = _sigmoid(a0_ref[...] + _dot(ad.astype(BF16), aup_ref[...]))
    g = _dot(_sigmoid(gd).astype(BF16), gup_ref[...])
    if has_vres:
        vmix = _dot(_dot(v.astype(BF16), vdown_ref[...]).astype(BF16), vup_ref[...])
        v = v + (vfirst_ref[0] - v) * _sigmoid(v0_ref[...] + vmix)
    else:
        vfirst_out_ref[0] = v

    hr = lax.broadcasted_iota(jnp.int32, (LANE, LANE), 0) // RWKV_HEAD
    hc = lax.broadcasted_iota(jnp.int32, (LANE, LANE), 1) // RWKV_HEAD
    ones_blk = (hr == hc).astype(BF16)

    kk = k * kk_ref[...]
    kk = kk * lax.rsqrt(jnp.maximum(_head_sum(kk * kk, ones_blk), 1e-24))
    k = k * (1.0 + (a - 1.0) * ka_ref[...])
    bonus = _head_sum(r * k * rk_ref[...], ones_blk)

    n_chunks = n_tok // CHUNK
    n_pairs = RWKV_HEADS // 2
    assert 2 * RWKV_HEAD == LANE and CHUNK == RWKV_HEAD
    rr = lax.broadcasted_iota(jnp.int32, (CHUNK, CHUNK), 0)
    cc = lax.broadcasted_iota(jnp.int32, (CHUNK, CHUNK), 1)
    tril_f = (cc <= rr).astype(F32)
    r_p = lax.broadcasted_iota(jnp.int32, (CHUNK, LANE), 0)
    l_p = lax.broadcasted_iota(jnp.int32, (CHUNK, LANE), 1)
    c_p = jnp.where(l_p >= RWKV_HEAD, l_p - RWKV_HEAD, l_p)
    m_left = (l_p < RWKV_HEAD).astype(F32)
    m_right = (l_p >= RWKV_HEAD).astype(F32)
    m_left_b = m_left.astype(BF16)
    m_right_b = m_right.astype(BF16)
    m_strict = (c_p < r_p).astype(F32)
    m_lower = (c_p <= r_p).astype(F32)
    eye_p = (c_p == r_p).astype(F32)
    m_diag2 = ((r_p // 2) == (c_p // 2)).astype(F32)
    lvl_masks = []
    size = 2
    while size < CHUNK:
        lvl_masks.append((((r_p // (2 * size)) == (c_p // (2 * size))) & ((r_p // size) != (c_p // size))).astype(F32))
        size *= 2

    def blockdiag(xb):
        return jnp.concatenate([xb * m_left_b, xb * m_right_b], axis=0)

    items = []
    g_ends = []
    for c in range(n_chunks):
        sl = slice(c * CHUNK, (c + 1) * CHUNK)
        lw = logw[sl]
        b = jnp.dot(tril_f, lw, precision=HIGHEST, preferred_element_type=F32)
        eb = jnp.exp(b)
        enb = jnp.exp(-b)
        ebx = jnp.exp(b - lw)
        rh = (r[sl] * eb).astype(BF16)
        kh = (k[sl] * enb).astype(BF16)
        ah = (-kk[sl] * ebx).astype(BF16)
        bh = (kk[sl] * a[sl] * enb).astype(BF16)
        vb = v[sl].astype(BF16)
        g_ends.append(eb[CHUNK - 1:CHUNK, :])
        for j in range(n_pairs):
            ls = slice(j * LANE, (j + 1) * LANE)
            items.append(dict(ah=ah[:, ls], rh=rh[:, ls], bh=bh[:, ls], kh=kh[:, ls], v=vb[:, ls]))

    for it in items:
        ar = jnp.concatenate([it["ah"], it["rh"]], axis=0)
        it["sb"] = _dot_nt(ar, blockdiag(it["bh"]))
        it["sk"] = _dot_nt(ar, blockdiag(it["kh"]))
    for it in items:
        it["a_ab"] = it["sb"][:CHUNK] * m_strict
        it["ak"] = (it["sk"][:CHUNK] * m_strict).astype(BF16)
        it["rbk"] = jnp.concatenate([(it["sb"][CHUNK:] * m_lower).astype(BF16),
                                     (it["sk"][CHUNK:] * m_lower).astype(BF16)], axis=1)
        it["d"] = eye_p + it["a_ab"] * m_diag2
        del it["sb"], it["sk"]
    for mk in lvl_masks:
        for it in items:
            it["db"] = it["d"].astype(BF16)
            it["m"] = _dot((it["a_ab"] * mk).astype(BF16), blockdiag(it["db"]))
        for it in items:
            it["d"] = it["d"] + _dot(it["db"], blockdiag(it["m"].astype(BF16)))
    for it in items:
        it["akv"] = _dot(it["ak"], blockdiag(it["v"]))
        it["tinv"] = it["d"].astype(BF16)
    for it in items:
        it["w"] = _dot(it["tinv"], blockdiag(it["ah"]))
        it["ut"] = _dot(it["tinv"], blockdiag(it["akv"].astype(BF16)))

    states = [state_ref[j] for j in range(n_pairs)]
    y_chunks = []
    for c in range(n_chunks):
        its = items[c * n_pairs:(c + 1) * n_pairs]
        for j, it in enumerate(its):
            lhs = jnp.concatenate([it["w"].astype(BF16), it["rh"]], axis=0)
            it["ws"] = _dot_nt(lhs, blockdiag(states[j].astype(BF16)))
        for j, it in enumerate(its):
            it["ub"] = (it["ws"][:CHUNK] + it["ut"]).astype(BF16)
            uv = jnp.concatenate([it["ub"], it["v"]], axis=0)
            upd = _dot_tn(uv, jnp.concatenate([it["bh"], it["kh"]], axis=0))
            ls = slice(j * LANE, (j + 1) * LANE)
            states[j] = (states[j] + upd[:RWKV_HEAD] * m_left + upd[RWKV_HEAD:] * m_right) * g_ends[c][:, ls]
        y_pairs = [it["ws"][CHUNK:] + _dot(it["rbk"], jnp.concatenate([blockdiag(it["ub"]), blockdiag(it["v"])], axis=0))
                   for it in its]
        y_chunks.append(jnp.concatenate(y_pairs, axis=1))
    for j in range(n_pairs):
        state_ref[j] = states[j]
    y = jnp.concatenate(y_chunks, axis=0) if len(y_chunks) > 1 else y_chunks[0]

    inv_n = 1.0 / RWKV_HEAD
    m = _head_sum(y, ones_blk) * inv_n
    yc = y - m
    var = _head_sum(yc * yc, ones_blk) * inv_n
    y = yc * lax.rsqrt(var + RWKV_GN_EPS) * gnw_ref[...] + gnb_ref[...]
    y = y + bonus * v
    y_ref[0] = (y * g).astype(BF16)


def _rwkv_mixer(p_r, vfirst, prm, has_vres):
    bsz, seq, _ = p_r.shape
    n_tok = min(512, seq)
    tok = lambda b, i: (b, i, 0)
    const = lambda b, i: (0, 0)
    vec = pl.BlockSpec((1, D_RWKV), const)
    in_specs = [pl.BlockSpec((1, n_tok, RW_WIDTH), tok)]
    args = [p_r]
    if has_vres:
        in_specs.append(pl.BlockSpec((1, n_tok, D_RWKV), tok))
        args.append(vfirst)
    in_specs += [pl.BlockSpec((1, RW_WIDTH), const), vec, pl.BlockSpec((LANE, D_RWKV), const), vec,
                 pl.BlockSpec((LANE, D_RWKV), const), pl.BlockSpec((2 * LANE, D_RWKV), const),
                 vec, vec, vec, vec, vec]
    args += [prm["mu"], prm["w0"], prm["w_up"], prm["a0"], prm["a_up"], prm["g_up"],
             prm["k_k"], prm["k_a"], prm["r_k"], prm["gn_w"], prm["gn_b"]]
    out_shape = [jax.ShapeDtypeStruct((bsz, seq, D_RWKV), BF16)]
    out_specs = [pl.BlockSpec((1, n_tok, D_RWKV), tok)]
    if has_vres:
        in_specs += [vec, pl.BlockSpec((D_RWKV, LANE), const), pl.BlockSpec((LANE, D_RWKV), const)]
        args += [prm["v0"], prm["v_down"], prm["v_up"]]
    else:
        out_shape.append(jax.ShapeDtypeStruct((bsz, seq, D_RWKV), F32))
        out_specs.append(pl.BlockSpec((1, n_tok, D_RWKV), tok))
    res = pl.pallas_call(
        functools.partial(_rwkv_kernel, has_vres),
        grid=(bsz, seq // n_tok),
        in_specs=in_specs,
        out_specs=out_specs,
        out_shape=out_shape,
        scratch_shapes=[pltpu.VMEM((RWKV_HEADS // 2, RWKV_HEAD, 2 * RWKV_HEAD), F32),
                        pltpu.VMEM((1, RW_WIDTH), F32)],
        compiler_params=_cparams(2),
        name="rwkv7_vres" if has_vres else "rwkv7",
    )(*args)
    if has_vres:
        return res[0], vfirst
    return res[0], res[1]


def _gla_kernel(p_ref, gkup_ref, gkb_ref, nw_ref, y_ref, state_ref):
    @pl.when(pl.program_id(1) == 0)
    def _():
        state_ref[...] = jnp.zeros_like(state_ref)

    p = p_ref[0].astype(F32)
    n_tok = p.shape[0]
    q = p[:, 0:GL_K]
    k = p[:, GL_K:GL_V]
    v = p[:, GL_V:GL_GK]
    gkd = p[:, GL_GK:GL_G]
    g = p[:, GL_G:GL_WIDTH]
    z = _dot(gkd.astype(BF16), gkup_ref[...]) + gkb_ref[...]
    gk = -_softplus(-z) * (1.0 / GLA_GATE_NORMALIZER)

    lower = _tril_mask(CHUNK, strict=False)
    tril_f = lower.astype(F32)
    n_chunks = n_tok // CHUNK
    items = []
    decs = []
    for c in range(n_chunks):
        sl = slice(c * CHUNK, (c + 1) * CHUNK)
        b = jnp.dot(tril_f, gk[sl], precision=HIGHEST, preferred_element_type=F32)
        b_last = b[CHUNK - 1:CHUNK, :]
        q_e = (q[sl] * jnp.exp(b) * (GLA_DK ** -0.5)).astype(BF16)
        k_e = (k[sl] * jnp.exp(-b)).astype(BF16)
        k_end = (k[sl] * jnp.exp(b_last - b)).astype(BF16)
        decs.append(jnp.exp(b_last))
        vb = v[sl].astype(BF16)
        for h in range(GLA_HEADS):
            ks = slice(h * GLA_DK, (h + 1) * GLA_DK)
            vs = slice(h * GLA_DV, (h + 1) * GLA_DV)
            items.append(dict(q=q_e[:, ks], k=k_e[:, ks], kend=k_end[:, ks], v=vb[:, vs]))
    for it in items:
        it["sc"] = jnp.where(lower, _dot_nt(it["q"], it["k"]), 0.0).astype(BF16)
    for it in items:
        it["intra"] = _dot(it["sc"], it["v"])
        it["kv"] = _dot_tn(it["v"], it["kend"])
    states = [state_ref[h] for h in range(GLA_HEADS)]
    o_chunks = []
    for c in range(n_chunks):
        o_heads = []
        for h in range(GLA_HEADS):
            it = items[c * GLA_HEADS + h]
            ks = slice(h * GLA_DK, (h + 1) * GLA_DK)
            o_h = it["intra"] + _dot_nt(it["q"], states[h].astype(BF16))
            states[h] = states[h] * decs[c][:, ks] + it["kv"]
            o_heads.append(o_h * lax.rsqrt(jnp.mean(o_h * o_h, axis=-1, keepdims=True) + RMS_EPS))
        o_chunks.append(jnp.concatenate(o_heads, axis=1))
    for h in range(GLA_HEADS):
        state_ref[h] = states[h]
    o = jnp.concatenate(o_chunks, axis=0) if len(o_chunks) > 1 else o_chunks[0]
    silu_g = g * _sigmoid(g)
    y_ref[0] = (o * nw_ref[...] * silu_g).astype(BF16)


def _gla_mixer(p_g, prm):
    bsz, seq, _ = p_g.shape
    n_tok = min(512, seq)
    tok = lambda b, i: (b, i, 0)
    const = lambda b, i: (0, 0)
    return pl.pallas_call(
        _gla_kernel,
        grid=(bsz, seq // n_tok),
        in_specs=[pl.BlockSpec((1, n_tok, GL_WIDTH), tok),
                  pl.BlockSpec((LANE, GL_K), const), pl.BlockSpec((1, GL_K), const),
                  pl.BlockSpec((1, D_GLA), const)],
        out_specs=pl.BlockSpec((1, n_tok, D_GLA), tok),
        out_shape=jax.ShapeDtypeStruct((bsz, seq, D_GLA), BF16),
        scratch_shapes=[pltpu.VMEM((GLA_HEADS, GLA_DV, GLA_DK), F32)],
        compiler_params=_cparams(2),
        name="gla",
    )(p_g, prm["gk_up"], prm["gk_b"], prm["norm_w"])


def _outproj_router_kernel(yr_ref, yg_ref, x_ref, wo_ref, lnw_ref, lnb_ref, rw_ref, rb_ref,
                           x1_ref, x1p_ref, idx_ref, gate_ref, rank_ref, cnt_ref, carry_ref):
    @pl.when(pl.program_id(0) == 0)
    def _():
        carry_ref[...] = jnp.zeros_like(carry_ref)

    mix = _dot(yr_ref[...], wo_ref[0:D_RWKV, :]) + _dot(yg_ref[...], wo_ref[D_RWKV:, :])
    x1 = _layer_norm(DEEPNORM_ALPHA * x_ref[...] + mix, lnw_ref[...], lnb_ref[...])
    x1_ref[...] = x1
    _rows_store(x1p_ref, x1)
    tm = x1.shape[0]

    xh, xl = _split_bf16(x1)
    rw = rw_ref[...]
    wh, wl = _split_bf16(rw)
    logits = _dot_nt(wh, xh) + _dot_nt(wh, xl) + _dot_nt(wl, xh) + rb_ref[...]

    e_iota = lax.broadcasted_iota(jnp.int32, (N_EXPERTS, tm), 0)
    work = logits
    vals, idxs = [], []
    member = jnp.zeros((N_EXPERTS, tm), F32)
    for _ in range(TOP_K):
        mx = jnp.max(work, axis=0, keepdims=True)
        ix = jnp.min(jnp.where(work == mx, e_iota, N_EXPERTS), axis=0, keepdims=True)
        sel = e_iota == ix
        work = jnp.where(sel, -jnp.inf, work)
        member = jnp.where(sel, 1.0, member)
        vals.append(mx)
        idxs.append(ix)
    exps = [jnp.exp(vv - vals[0]) for vv in vals]
    inv_den = 1.0 / (exps[0] + exps[1] + exps[2] + exps[3])

    tr = lax.broadcasted_iota(jnp.int32, (tm, tm), 0)
    tc = lax.broadcasted_iota(jnp.int32, (tm, tm), 1)
    before = (tr < tc).astype(BF16)
    cex = _dot(member.astype(BF16), before) + carry_ref[...][:, 0:1]
    for kq in range(TOP_K):
        sel = e_iota == idxs[kq]
        idx_ref[kq:kq + 1, :] = idxs[kq]
        gate_ref[kq:kq + 1, :] = exps[kq] * inv_den
        rank_ref[kq:kq + 1, :] = jnp.sum(jnp.where(sel, cex, 0.0), axis=0, keepdims=True).astype(jnp.int32)
    carry_ref[...] = carry_ref[...] + jnp.sum(member, axis=1, keepdims=True)
    cnt_ref[...] = carry_ref[...].astype(jnp.int32)


def _outproj_router(yr, yg, x, wo, lnw, lnb, rw_t, rb):
    t = x.shape[0]
    tm = min(512, t)
    row = lambda i: (i, 0)
    col = lambda i: (0, i)
    const = lambda i: (0, 0)
    return pl.pallas_call(
        _outproj_router_kernel,
        grid=(t // tm,),
        in_specs=[pl.BlockSpec((tm, D_RWKV), row), pl.BlockSpec((tm, D_GLA), row),
                  pl.BlockSpec((tm, D_MODEL), row), pl.BlockSpec((D_MODEL, D_MODEL), const),
                  pl.BlockSpec((1, D_MODEL), const), pl.BlockSpec((1, D_MODEL), const),
                  pl.BlockSpec((N_EXPERTS, D_MODEL), const), pl.BlockSpec((N_EXPERTS, 1), const)],
        out_specs=[pl.BlockSpec((tm, D_MODEL), row), pl.BlockSpec((tm * ROW_R, LANE), row),
                   pl.BlockSpec((TOP_K, tm), col), pl.BlockSpec((TOP_K, tm), col),
                   pl.BlockSpec((TOP_K, tm), col), pl.BlockSpec((N_EXPERTS, LANE), const)],
        out_shape=[jax.ShapeDtypeStruct((t, D_MODEL), F32), jax.ShapeDtypeStruct((t * ROW_R, LANE), F32),
                   jax.ShapeDtypeStruct((TOP_K, t), jnp.int32), jax.ShapeDtypeStruct((TOP_K, t), F32),
                   jax.ShapeDtypeStruct((TOP_K, t), jnp.int32),
                   jax.ShapeDtypeStruct((N_EXPERTS, LANE), jnp.int32)],
        scratch_shapes=[pltpu.VMEM((N_EXPERTS, LANE), F32)],
        compiler_params=_cparams(1),
        name="outproj_router",
    )(yr, yg, x, wo, lnw, lnb, rw_t, rb)


SEG_BIG = 64
SEG_MID = 8


def _issue_run(n, start_copy):
    n_big = lax.shift_right_logical(n, 6)
    n_mid = lax.shift_right_logical(n, 3) & 7
    n_one = n & 7
    off_mid = n_big * SEG_BIG
    off_one = off_mid + n_mid * SEG_MID
    lax.fori_loop(0, n_big, lambda q, c: (start_copy(q * SEG_BIG, SEG_BIG), c)[1], 0)
    lax.fori_loop(0, n_mid, lambda q, c: (start_copy(off_mid + q * SEG_MID, SEG_MID), c)[1], 0)
    lax.fori_loop(0, n_one, lambda q, c: (start_copy(off_one + q, 1), c)[1], 0)


def _dispatch_kernel(pos_ref, seg_ref, gaps_ref, x_ref, xs_ref, stage_ref, zero_ref, sem, zsem):
    te = x_ref.shape[0] // ROW_R
    i = pl.program_id(0)
    n_tiles = pl.num_programs(0)
    slot = i % 2
    base = i * (TOP_K * te)

    def drain(sl):
        for kq in range(TOP_K):
            pltpu.make_async_copy(x_ref, x_ref, sem.at[sl]).wait()

    @pl.when(pl.program_id(0) == 0)
    def _():
        zero_ref[...] = jnp.zeros_like(zero_ref)

        for g in range(N_EXPERTS + 1):
            lo = gaps_ref[2 * g]

            def zero_copy(off, rows, lo=lo):
                dst = xs_ref.at[pl.ds(pl.multiple_of((lo + off) * ROW_R, ROW_R), rows * ROW_R), :]
                pltpu.make_async_copy(zero_ref.at[pl.ds(0, rows * ROW_R), :], dst, zsem).start()

            _issue_run(gaps_ref[2 * g + 1] - lo, zero_copy)
        blk = pl.ds(0, MOE_BLOCK * ROW_R)
        for g in range(N_EXPERTS):
            pltpu.make_async_copy(xs_ref.at[blk, :], xs_ref.at[blk, :], zsem).wait()

    @pl.when(i >= 2)
    def _():
        drain(slot)

    for sl in range(2):
        @pl.when(slot == sl)
        def _(sl=sl):
            def place(r, carry):
                row = x_ref[pl.ds(pl.multiple_of(r * ROW_R, ROW_R), ROW_R), :]
                for kq in range(TOP_K):
                    p = pos_ref[base + r * TOP_K + kq]
                    stage_ref[sl, pl.ds(pl.multiple_of(p, ROW_R), ROW_R), :] = row
                return carry

            lax.fori_loop(0, te, place, 0, unroll=4)

    for e in range(N_EXPERTS):
        o = (i * N_EXPERTS + e) * 3
        s, n, g = seg_ref[o], seg_ref[o + 1], seg_ref[o + 2]

        def copy(off, rows, s=s, g=g):
            src = stage_ref.at[slot, pl.ds(pl.multiple_of((s + off) * ROW_R, ROW_R), rows * ROW_R), :]
            dst = xs_ref.at[pl.ds(pl.multiple_of((g + off) * ROW_R, ROW_R), rows * ROW_R), :]
            pltpu.make_async_copy(src, dst, sem.at[slot]).start()

        _issue_run(n, copy)

    @pl.when(i == n_tiles - 1)
    def _():
        @pl.when(i >= 1)
        def _():
            drain(1 - slot)
        drain(slot)


def _dispatch(x1, pos_tiles, seg, gaps, m_pad, te):
    t = x1.shape[0] // ROW_R
    return pl.pallas_call(
        _dispatch_kernel,
        grid_spec=pltpu.PrefetchScalarGridSpec(
            num_scalar_prefetch=3,
            grid=(t // te,),
            in_specs=[pl.BlockSpec((te * ROW_R, LANE), lambda i, p, s, g: (i, 0))],
            out_specs=pl.BlockSpec(memory_space=pl.ANY),
            scratch_shapes=[pltpu.VMEM((2, TOP_K * te * ROW_R, LANE), F32), pltpu.VMEM((SEG_BIG * ROW_R, LANE), F32),
                            pltpu.SemaphoreType.DMA((2,)), pltpu.SemaphoreType.DMA(())],
        ),
        out_shape=jax.ShapeDtypeStruct((m_pad * ROW_R, LANE), F32),
        compiler_params=_cparams(1),
        name="moe_dispatch",
    )(pos_tiles, seg, gaps, x1)


def _expert_kernel(be_ref, nv_ref, xs_ref, w1_ref, b1_ref, w2_ref, b2_ref, ys_ref, act_ref, w1b_ref, w2b_ref):
    j = pl.program_id(0)
    n_valid = nv_ref[j]

    @pl.when((j == 0) | (be_ref[j] != be_ref[jnp.maximum(j - 1, 0)]))
    def _():
        rows = 128
        for c in range(D_MODEL // rows):
            rs = slice(c * rows, (c + 1) * rows)
            w1b_ref[rs, :] = w1_ref[0, 0, rs, :].astype(BF16)
            w2b_ref[rs, :] = w2_ref[0, 0, rs, :].astype(BF16)

    @pl.when(n_valid == 0)
    def _():
        ys_ref[...] = jnp.zeros_like(ys_ref)

    @pl.when(n_valid > 0)
    def _():
        xb = _rows_load(xs_ref, MOE_BLOCK).astype(BF16)
        n_chunk = 256
        for c in range(D_FF // n_chunk):
            cs = slice(c * n_chunk, (c + 1) * n_chunk)
            us = slice(D_FF + c * n_chunk, D_FF + (c + 1) * n_chunk)
            gt = _dot(xb, w1b_ref[:, cs]) + b1_ref[0, 0, :, cs]
            up = _dot(xb, w1b_ref[:, us]) + b1_ref[0, 0, :, us]
            gt = jnp.minimum(gt, SWIGLU_LIMIT)
            up = jnp.clip(up, -SWIGLU_LIMIT, SWIGLU_LIMIT)
            act_ref[:, cs] = ((up + 1.0) * gt * _sigmoid(SWIGLU_ALPHA * gt)).astype(BF16)
        _rows_store(ys_ref, _dot(act_ref[...], w2b_ref[...]) + b2_ref[0, 0])


def _experts(xs, block_e, n_valid, layer, w1, b1, w2, b2):
    m_pad = xs.shape[0] // ROW_R
    n_blocks = m_pad // MOE_BLOCK
    wmap = lambda j, be, nv: (layer, be[j], 0, 0)
    return pl.pallas_call(
        _expert_kernel,
        grid_spec=pltpu.PrefetchScalarGridSpec(
            num_scalar_prefetch=2,
            grid=(n_blocks,),
            in_specs=[pl.BlockSpec((MOE_BLOCK * ROW_R, LANE), lambda j, be, nv: (j, 0)),
                      pl.BlockSpec((1, 1, D_MODEL, 2 * D_FF), wmap),
                      pl.BlockSpec((1, 1, 1, 2 * D_FF), wmap),
                      pl.BlockSpec((1, 1, D_FF, D_MODEL), wmap),
                      pl.BlockSpec((1, 1, 1, D_MODEL), wmap)],
            out_specs=pl.BlockSpec((MOE_BLOCK * ROW_R, LANE), lambda j, be, nv: (j, 0)),
            scratch_shapes=[pltpu.VMEM((MOE_BLOCK, D_FF), BF16),
                            pltpu.VMEM((D_MODEL, 2 * D_FF), BF16), pltpu.VMEM((D_FF, D_MODEL), BF16)],
        ),
        out_shape=jax.ShapeDtypeStruct((m_pad * ROW_R, LANE), F32),
        compiler_params=_cparams(1),
        name="moe_experts",
    )(block_e, n_valid, xs, w1, b1, w2, b2)


def _combine_kernel(with_inproj, dest_ref, ys_ref, x1_ref, gate_ref, lnw_ref, lnb_ref, *refs):
    if with_inproj:
        wr_ref, wg_ref, out_ref, pr_ref, pg_ref, buf_ref, sem = refs
    else:
        out_ref, buf_ref, sem = refs
    te = x1_ref.shape[0]
    i = pl.program_id(0)
    n_tiles = pl.num_programs(0)

    def gather(tile, slot):
        base = tile * (TOP_K * te)

        def issue(r, carry):
            for kq in range(TOP_K):
                d = dest_ref[base + kq * te + r]
                pltpu.make_async_copy(_row_tile(ys_ref, d), _row_tile(buf_ref.at[slot, kq], r),
                                      sem.at[slot]).start(priority=kq % 2)
            return carry

        lax.fori_loop(0, te, issue, 0, unroll=8)

    @pl.when(i == 0)
    def _():
        gather(0, 0)

    for slot in range(2):
        @pl.when((i + 1 < n_tiles) & ((i + 1) % 2 == slot))
        def _():
            gather(i + 1, slot)

    for slot in range(2):
        @pl.when(i % 2 == slot)
        def _():
            for kq in range(TOP_K):
                pltpu.make_async_copy(buf_ref.at[slot, kq], buf_ref.at[slot, kq], sem.at[slot]).wait()
            gates = gate_ref[...]
            z = DEEPNORM_ALPHA * x1_ref[...]
            for kq in range(TOP_K):
                z = z + _rows_load(buf_ref.at[slot, kq], te) * gates[:, kq:kq + 1]
            xn = _layer_norm(z, lnw_ref[...], lnb_ref[...])
            out_ref[...] = xn
            if with_inproj:
                xb = xn.astype(BF16)
                pr_ref[...] = _dot(xb, wr_ref[...]).astype(BF16)
                pg_ref[...] = _dot(xb, wg_ref[...]).astype(BF16)


def _combine(ys, dest_tiles, x1, gates_t, lnw, lnb, te, next_inproj=None):
    t = x1.shape[0]
    row = lambda i, d: (i, 0)
    const = lambda i, d: (0, 0)
    in_specs = [pl.BlockSpec(memory_space=pl.ANY),
                pl.BlockSpec((te, D_MODEL), row),
                pl.BlockSpec((te, TOP_K), row),
                pl.BlockSpec((1, D_MODEL), const),
                pl.BlockSpec((1, D_MODEL), const)]
    out_specs = [pl.BlockSpec((te, D_MODEL), row)]
    out_shape = [jax.ShapeDtypeStruct((t, D_MODEL), F32)]
    args = [dest_tiles, ys, x1, gates_t, lnw, lnb]
    if next_inproj is not None:
        in_specs += [pl.BlockSpec((D_MODEL, RW_WIDTH), const), pl.BlockSpec((D_MODEL, GL_WIDTH), const)]
        out_specs += [pl.BlockSpec((te, RW_WIDTH), row), pl.BlockSpec((te, GL_WIDTH), row)]
        out_shape += [jax.ShapeDtypeStruct((t, RW_WIDTH), BF16), jax.ShapeDtypeStruct((t, GL_WIDTH), BF16)]
        args += list(next_inproj)
    return pl.pallas_call(
        functools.partial(_combine_kernel, next_inproj is not None),
        grid_spec=pltpu.PrefetchScalarGridSpec(
            num_scalar_prefetch=1,
            grid=(t // te,),
            in_specs=in_specs,
            out_specs=out_specs,
            scratch_shapes=[pltpu.VMEM((2, TOP_K, te * ROW_R, LANE), F32), pltpu.SemaphoreType.DMA((2,))],
        ),
        out_shape=out_shape,
        compiler_params=_cparams(1),
        name="moe_combine_inproj" if next_inproj is not None else "moe_combine",
    )(*args)


def _moe(x1, x1p, idx, gate, rank, counts, layer, w1, b1, w2, b2, lnw, lnb, next_inproj=None):
    t = x1.shape[0]
    te = min(256, t)
    assert (t * TOP_K) % MOE_BLOCK == 0
    n_blocks = (t * TOP_K) // MOE_BLOCK + N_EXPERTS
    m_pad = n_blocks * MOE_BLOCK
    padded = ((counts + MOE_BLOCK - 1) // MOE_BLOCK) * MOE_BLOCK
    pend = jnp.cumsum(padded)
    pstart = pend - padded
    dest = rank
    for e in range(N_EXPERTS):
        dest = dest + jnp.where(idx == e, pstart[e], 0)
    dest_tiles = dest.reshape(TOP_K, t // te, te).transpose(1, 0, 2).reshape(-1)
    blk_start = jnp.arange(n_blocks, dtype=jnp.int32) * MOE_BLOCK
    block_e = jnp.minimum(jnp.sum(pend[None, :] <= blk_start[:, None], axis=1), N_EXPERTS - 1).astype(jnp.int32)
    is_e = block_e[:, None] == jnp.arange(N_EXPERTS, dtype=jnp.int32)[None, :]
    blk_end = jnp.sum(jnp.where(is_e, (pstart + counts)[None, :], 0), axis=1)
    n_valid = jnp.clip(blk_end - blk_start, 0, MOE_BLOCK).astype(jnp.int32)

    gap_lo = jnp.concatenate([pstart + counts, pend[-1:]])
    gap_hi = jnp.concatenate([pend, jnp.full((1,), m_pad, pend.dtype)])
    gaps = jnp.stack([gap_lo, gap_hi], axis=1).reshape(-1).astype(jnp.int32)

    td = min(512, t)
    n_td = t // td
    idx_t = idx.reshape(TOP_K, n_td, td)
    e_ids = jnp.arange(N_EXPERTS, dtype=jnp.int32)
    tcnt = jnp.sum((idx_t[..., None] == e_ids).astype(jnp.int32), axis=(0, 2))
    cum = jnp.cumsum(tcnt, axis=0) - tcnt
    toff = jnp.cumsum(tcnt, axis=1) - tcnt
    seg = jnp.stack([toff, tcnt, pstart[None, :] + cum], axis=-1).reshape(-1).astype(jnp.int32)
    delta = toff - cum
    pos = rank.reshape(TOP_K, n_td, td)
    for e in range(N_EXPERTS):
        pos = pos + jnp.where(idx_t == e, delta[None, :, e, None], 0)
    pos_tiles = (pos.transpose(1, 2, 0) * ROW_R).reshape(-1).astype(jnp.int32)

    xs = _dispatch(x1p, pos_tiles, seg, gaps, m_pad, td)
    ys = _experts(xs, block_e, n_valid, layer, w1, b1, w2, b2)
    return _combine(ys, dest_tiles, x1, gate.T, lnw, lnb, te, next_inproj)


def _pad_cols(w, width):
    return jnp.pad(w, ((0, 0), (0, width - w.shape[1])))


def _pad_rows(w, height):
    return jnp.pad(w, ((0, height - w.shape[0]), (0, 0)))


def _pack_inproj(w_in_l, mu_l):
    d3 = 3 * D_RWKV
    o_ad = d3 + W_LORA
    o_gd = o_ad + A_LORA
    o_gla = o_gd + G_LORA
    wr = jnp.concatenate([w_in_l[:, :d3],
                          _pad_cols(w_in_l[:, d3:o_ad], LANE),
                          _pad_cols(w_in_l[:, o_ad:o_gd], LANE),
                          _pad_cols(w_in_l[:, o_gd:o_gla], 2 * LANE)], axis=1)
    mu = mu_l[None, :]
    mu_p = jnp.concatenate([mu[:, :d3], _pad_cols(mu[:, d3:o_ad], LANE), _pad_cols(mu[:, o_ad:o_gd], LANE),
                            _pad_cols(mu[:, o_gd:o_gla], 2 * LANE)], axis=1)
    g = w_in_l[:, o_gla:]
    wg = jnp.concatenate([g[:, :GL_GK], _pad_cols(g[:, GL_GK:GL_GK + GLA_GATE_RANK], LANE),
                          g[:, GL_GK + GLA_GATE_RANK:]], axis=1)
    return wr.astype(BF16), wg.astype(BF16), mu_p


def kernel(x, ln_in_w, ln_in_b, w_in, rwkv_mu, rwkv_w0, rwkv_w_up, rwkv_a0, rwkv_a_up, rwkv_g_up, rwkv_k_k, rwkv_k_a, rwkv_r_k, rwkv_gn_w, rwkv_gn_b, rwkv_v0, rwkv_v_down, rwkv_v_up, gla_gk_up, gla_gk_b, gla_norm_w, w_out, ln1_w, ln1_b, router_w, router_b, exp_w1, exp_b1, exp_w2, exp_b2, ln2_w, ln2_b):
    bsz, seq, d = x.shape
    t = bsz * seq
    xf = x.reshape(t, d)
    vfirst = None
    packed = [_pack_inproj(w_in[l], rwkv_mu[l]) for l in range(DEPTH)]
    xf, p_r, p_g = _inproj(xf, ln_in_w[None, :], ln_in_b[None, :], packed[0][0], packed[0][1])
    for l in range(DEPTH):
        mu_p = packed[l][2]
        rprm = {
            "mu": mu_p, "w0": rwkv_w0[l][None, :], "w_up": _pad_rows(rwkv_w_up[l], LANE).astype(BF16),
            "a0": rwkv_a0[l][None, :], "a_up": _pad_rows(rwkv_a_up[l], LANE).astype(BF16),
            "g_up": _pad_rows(rwkv_g_up[l], 2 * LANE).astype(BF16),
            "k_k": rwkv_k_k[l][None, :], "k_a": rwkv_k_a[l][None, :], "r_k": rwkv_r_k[l][None, :],
            "gn_w": rwkv_gn_w[l][None, :], "gn_b": rwkv_gn_b[l][None, :],
        }
        if l > 0:
            rprm["v0"] = rwkv_v0[l - 1][None, :]
            rprm["v_down"] = _pad_cols(rwkv_v_down[l - 1], LANE).astype(BF16)
            rprm["v_up"] = _pad_rows(rwkv_v_up[l - 1], LANE).astype(BF16)
        y_r, vfirst = _rwkv_mixer(p_r.reshape(bsz, seq, RW_WIDTH), vfirst, rprm, has_vres=l > 0)
        gprm = {"gk_up": _pad_rows(gla_gk_up[l], LANE).astype(BF16), "gk_b": gla_gk_b[l][None, :],
                "norm_w": jnp.tile(gla_norm_w[l], GLA_HEADS)[None, :]}
        y_g = _gla_mixer(p_g.reshape(bsz, seq, GL_WIDTH), gprm)
        x1, x1p, idx, gate, rank, cnt = _outproj_router(
            y_r.reshape(t, D_RWKV), y_g.reshape(t, D_GLA), xf, w_out[l].astype(BF16),
            ln1_w[l][None, :], ln1_b[l][None, :], router_w[l].T, router_b[l][:, None])
        nxt = packed[l + 1][:2] if l + 1 < DEPTH else None
        res = _moe(x1, x1p, idx, gate, rank, cnt[:, 0], l, exp_w1, exp_b1[:, :, None, :],
                   exp_w2, exp_b2[:, :, None, :], ln2_w[l][None, :], ln2_b[l][None, :], nxt)
        if nxt is not None:
            xf, p_r, p_g = res
        else:
            xf = res[0]
    return xf.reshape(bsz, seq, d)
```

```python
import functools

import jax
import jax.numpy as jnp
from jax import lax
from jax.experimental import pallas as pl
from jax.experimental.pallas import tpu as pltpu

F32 = jnp.float32
BF16 = jnp.bfloat16
HIGHEST = lax.Precision.HIGHEST

D_MODEL = 1024
DEPTH = 2
CHUNK = 64
RWKV_HEAD = 64
D_RWKV = 512
RWKV_HEADS = 8
W_LORA = 64
A_LORA = 64
V_LORA = 32
G_LORA = 160
RWKV_GN_EPS = 64e-5
D_GLA = 512
GLA_HEADS = 4
GLA_DV = 128
GLA_DK = 64
GLA_GATE_RANK = 16
GLA_GATE_NORMALIZER = 16.0
RMS_EPS = 1e-6
N_EXPERTS = 32
TOP_K = 4
D_FF = 1024
SWIGLU_LIMIT = 7.0
SWIGLU_ALPHA = 1.702
MOE_BLOCK = 512
DEEPNORM_ALPHA = (2 * DEPTH) ** 0.25
LN_EPS = 1e-5

LANE = 128
SUBLANE = 8
ROW_R = D_MODEL // LANE
assert ROW_R == SUBLANE
RW_WD = 3 * D_RWKV
RW_AD = RW_WD + LANE
RW_GD = RW_AD + LANE
RW_WIDTH = RW_GD + 2 * LANE
GL_K = GLA_HEADS * GLA_DK
GL_V = 2 * GL_K
GL_GK = GL_V + D_GLA
GL_G = GL_GK + LANE
GL_WIDTH = GL_G + D_GLA

VMEM_LIMIT = 56 * 1024 * 1024


def _cparams(n_axes):
    return pltpu.CompilerParams(dimension_semantics=("arbitrary",) * n_axes,
                                vmem_limit_bytes=VMEM_LIMIT)


def _dot(a, b):
    return jnp.dot(a, b, preferred_element_type=F32)


def _dot_nt(a, b):
    return lax.dot_general(a, b, (((1,), (1,)), ((), ())), preferred_element_type=F32)


def _dot_tn(a, b):
    return lax.dot_general(a, b, (((0,), (0,)), ((), ())), preferred_element_type=F32)


def _split_bf16(x):
    hi = x.astype(BF16)
    lo = (x - hi.astype(F32)).astype(BF16)
    return hi, lo


def _layer_norm(z, w, b):
    mu = jnp.mean(z, axis=-1, keepdims=True)
    zc = z - mu
    var = jnp.mean(zc * zc, axis=-1, keepdims=True)
    return zc * lax.rsqrt(var + LN_EPS) * w + b


def _sigmoid(x):
    return 1.0 / (1.0 + jnp.exp(-x))


def _softplus(x):
    return jnp.maximum(x, 0.0) + jnp.log(1.0 + jnp.exp(-jnp.abs(x)))


def _rows_load(ref, n):
    return jnp.concatenate([ref[pl.ds(s, n, stride=ROW_R), :] for s in range(ROW_R)], axis=1)


def _rows_store(ref, val):
    for s in range(ROW_R):
        ref[pl.ds(s, val.shape[0], stride=ROW_R), :] = val[:, s * LANE:(s + 1) * LANE]


def _row_tile(ref, row):
    return ref.at[pl.ds(pl.multiple_of(row * ROW_R, ROW_R), ROW_R), :]


def _tril_mask(n, strict):
    r = lax.broadcasted_iota(jnp.int32, (n, n), 0)
    c = lax.broadcasted_iota(jnp.int32, (n, n), 1)
    return (c < r) if strict else (c <= r)


def _inproj_kernel(x_ref, lnw_ref, lnb_ref, wr_ref, wg_ref, x0_ref, pr_ref, pg_ref):
    x = _layer_norm(x_ref[...], lnw_ref[...], lnb_ref[...])
    x0_ref[...] = x
    xb = x.astype(BF16)
    pr_ref[...] = _dot(xb, wr_ref[...]).astype(BF16)
    pg_ref[...] = _dot(xb, wg_ref[...]).astype(BF16)


def _inproj(x, lnw, lnb, wr, wg):
    t = x.shape[0]
    tm = min(512, t)
    row = lambda i: (i, 0)
    const = lambda i: (0, 0)
    return pl.pallas_call(
        _inproj_kernel,
        grid=(t // tm,),
        in_specs=[pl.BlockSpec((tm, D_MODEL), row),
                  pl.BlockSpec((1, D_MODEL), const), pl.BlockSpec((1, D_MODEL), const),
                  pl.BlockSpec((D_MODEL, RW_WIDTH), const), pl.BlockSpec((D_MODEL, GL_WIDTH), const)],
        out_specs=[pl.BlockSpec((tm, D_MODEL), row), pl.BlockSpec((tm, RW_WIDTH), row),
                   pl.BlockSpec((tm, GL_WIDTH), row)],
        out_shape=[jax.ShapeDtypeStruct((t, D_MODEL), F32), jax.ShapeDtypeStruct((t, RW_WIDTH), BF16),
                   jax.ShapeDtypeStruct((t, GL_WIDTH), BF16)],
        compiler_params=_cparams(1),
        name="inproj_ln",
    )(x, lnw, lnb, wr, wg)


def _head_sum(x, ones_pair):
    n, tiles = x.shape[0], x.shape[1] // LANE
    st = jnp.concatenate([x[:, j * LANE:(j + 1) * LANE] for j in range(tiles)], axis=0)
    hi, lo = _split_bf16(st)
    s = _dot(jnp.concatenate([hi, lo], axis=0), ones_pair)
    s = s[:tiles * n] + s[tiles * n:]
    return jnp.concatenate([s[j * n:(j + 1) * n] for j in range(tiles)], axis=1)


def _rwkv_kernel(has_vres, *refs):
    if has_vres:
        (p_ref, vfirst_ref, mu_ref, w0_ref, wup_ref, a0_ref, aup_ref, gup_ref, kk_ref, ka_ref, rk_ref,
         gnw_ref, gnb_ref, v0_ref, vdown_ref, vup_ref, y_ref, state_ref, prev_ref) = refs
    else:
        (p_ref, mu_ref, w0_ref, wup_ref, a0_ref, aup_ref, gup_ref, kk_ref, ka_ref, rk_ref,
         gnw_ref, gnb_ref, y_ref, vfirst_out_ref, state_ref, prev_ref) = refs

    @pl.when(pl.program_id(1) == 0)
    def _():
        state_ref[...] = jnp.zeros_like(state_ref)
        prev_ref[...] = jnp.zeros_like(prev_ref)

    p = p_ref[0].astype(F32)
    n_tok = p.shape[0]
    row = lax.broadcasted_iota(jnp.int32, (n_tok, 1), 0)
    prev = jnp.where(row == 0, prev_ref[...], pltpu.roll(p, 1, axis=0))
    prev_ref[...] = p[n_tok - 1:n_tok, :]
    xs = p + (prev - p) * mu_ref[...]
    r = xs[:, 0:D_RWKV]
    k = xs[:, D_RWKV:2 * D_RWKV]
    v = xs[:, 2 * D_RWKV:3 * D_RWKV]
    wd = xs[:, RW_WD:RW_AD]
    ad = xs[:, RW_AD:RW_GD]
    gd = xs[:, RW_GD:RW_WIDTH]

    w = w0_ref[...] + _dot(jnp.tanh(wd).astype(BF16), wup_ref[...])
    w = -_softplus(-w) - 0.5
    logw = -jnp.exp(w)
    a = _sigmoid(a0_ref[...] + _dot(ad.astype(BF16), aup_ref[...]))
    g = _dot(_sigmoid(gd).astype(BF16), gup_ref[...])
    if has_vres:
        vmix = _dot(_dot(v.astype(BF16), vdown_ref[...]).astype(BF16), vup_ref[...])
        v = v + (vfirst_ref[0] - v) * _sigmoid(v0_ref[...] + vmix)
    else:
        vfirst_out_ref[0] = v

    hr = lax.broadcasted_iota(jnp.int32, (LANE, LANE), 0) // RWKV_HEAD
    hc = lax.broadcasted_iota(jnp.int32, (LANE, LANE), 1) // RWKV_HEAD
    ones_blk = (hr == hc).astype(BF16)

    kk = k * kk_ref[...]
    kk = kk * lax.rsqrt(jnp.maximum(_head_sum(kk * kk, ones_blk), 1e-24))
    k = k * (1.0 + (a - 1.0) * ka_ref[...])
    bonus = _head_sum(r * k * rk_ref[...], ones_blk)

    n_chunks = n_tok // CHUNK
    n_pairs = RWKV_HEADS // 2
    assert 2 * RWKV_HEAD == LANE and CHUNK == RWKV_HEAD
    rr = lax.broadcasted_iota(jnp.int32, (CHUNK, CHUNK), 0)
    cc = lax.broadcasted_iota(jnp.int32, (CHUNK, CHUNK), 1)
    tril_f = (cc <= rr).astype(F32)
    r_p = lax.broadcasted_iota(jnp.int32, (CHUNK, LANE), 0)
    l_p = lax.broadcasted_iota(jnp.int32, (CHUNK, LANE), 1)
    c_p = jnp.where(l_p >= RWKV_HEAD, l_p - RWKV_HEAD, l_p)
    m_left = (l_p < RWKV_HEAD).astype(F32)
    m_right = (l_p >= RWKV_HEAD).astype(F32)
    m_left_b = m_left.astype(BF16)
    m_right_b = m_right.astype(BF16)
    m_strict = (c_p < r_p).astype(F32)
    m_lower = (c_p <= r_p).astype(F32)
    eye_p = (c_p == r_p).astype(F32)
    m_diag2 = ((r_p // 2) == (c_p // 2)).astype(F32)
    lvl_masks = []
    size = 2
    while size < CHUNK:
        lvl_masks.append((((r_p // (2 * size)) == (c_p // (2 * size))) & ((r_p // size) != (c_p // size))).astype(F32))
        size *= 2

    def blockdiag(xb):
        return jnp.concatenate([xb * m_left_b, xb * m_right_b], axis=0)

    items = []
    g_ends = []
    for c in range(n_chunks):
        sl = slice(c * CHUNK, (c + 1) * CHUNK)
        lw = logw[sl]
        b = jnp.dot(tril_f, lw, precision=HIGHEST, preferred_element_type=F32)
        eb = jnp.exp(b)
        enb = jnp.exp(-b)
        ebx = jnp.exp(b - lw)
        rh = (r[sl] * eb).astype(BF16)
        kh = (k[sl] * enb).astype(BF16)
        ah = (-kk[sl] * ebx).astype(BF16)
        bh = (kk[sl] * a[sl] * enb).astype(BF16)
        vb = v[sl].astype(BF16)
        g_ends.append(eb[CHUNK - 1:CHUNK, :])
        for j in range(n_pairs):
            ls = slice(j * LANE, (j + 1) * LANE)
            items.append(dict(ah=ah[:, ls], rh=rh[:, ls], bh=bh[:, ls], kh=kh[:, ls], v=vb[:, ls]))

    for it in items:
        ar = jnp.concatenate([it["ah"], it["rh"]], axis=0)
        it["sb"] = _dot_nt(ar, blockdiag(it["bh"]))
        it["sk"] = _dot_nt(ar, blockdiag(it["kh"]))
    for it in items:
        it["a_ab"] = it["sb"][:CHUNK] * m_strict
        it["ak"] = (it["sk"][:CHUNK] * m_strict).astype(BF16)
        it["rbk"] = jnp.concatenate([(it["sb"][CHUNK:] * m_lower).astype(BF16),
                                     (it["sk"][CHUNK:] * m_lower).astype(BF16)], axis=1)
        it["d"] = eye_p + it["a_ab"] * m_diag2
        del it["sb"], it["sk"]
    for mk in lvl_masks:
        for it in items:
            it["db"] = it["d"].astype(BF16)
            it["m"] = _dot((it["a_ab"] * mk).astype(BF16), blockdiag(it["db"]))
        for it in items:
            it["d"] = it["d"] + _dot(it["db"], blockdiag(it["m"].astype(BF16)))
    for it in items:
        it["akv"] = _dot(it["ak"], blockdiag(it["v"]))
        it["tinv"] = it["d"].astype(BF16)
    for it in items:
        it["w"] = _dot(it["tinv"], blockdiag(it["ah"]))
        it["ut"] = _dot(it["tinv"], blockdiag(it["akv"].astype(BF16)))

    states = [state_ref[j] for j in range(n_pairs)]
    y_chunks = []
    for c in range(n_chunks):
        its = items[c * n_pairs:(c + 1) * n_pairs]
        for j, it in enumerate(its):
            lhs = jnp.concatenate([it["w"].astype(BF16), it["rh"]], axis=0)
            it["ws"] = _dot_nt(lhs, blockdiag(states[j].astype(BF16)))
        for j, it in enumerate(its):
            it["ub"] = (it["ws"][:CHUNK] + it["ut"]).astype(BF16)
            uv = jnp.concatenate([it["ub"], it["v"]], axis=0)
            upd = _dot_tn(uv, jnp.concatenate([it["bh"], it["kh"]], axis=0))
            ls = slice(j * LANE, (j + 1) * LANE)
            states[j] = (states[j] + upd[:RWKV_HEAD] * m_left + upd[RWKV_HEAD:] * m_right) * g_ends[c][:, ls]
        y_pairs = [it["ws"][CHUNK:] + _dot(it["rbk"], jnp.concatenate([blockdiag(it["ub"]), blockdiag(it["v"])], axis=0))
                   for it in its]
        y_chunks.append(jnp.concatenate(y_pairs, axis=1))
    for j in range(n_pairs):
        state_ref[j] = states[j]
    y = jnp.concatenate(y_chunks, axis=0) if len(y_chunks) > 1 else y_chunks[0]

    inv_n = 1.0 / RWKV_HEAD
    m = _head_sum(y, ones_blk) * inv_n
    yc = y - m
    var = _head_sum(yc * yc, ones_blk) * inv_n
    y = yc * lax.rsqrt(var + RWKV_GN_EPS) * gnw_ref[...] + gnb_ref[...]
    y = y + bonus * v
    y_ref[0] = (y * g).astype(BF16)


def _rwkv_mixer(p_r, vfirst, prm, has_vres):
    bsz, seq, _ = p_r.shape
    n_tok = min(512, seq)
    tok = lambda b, i: (b, i, 0)
    const = lambda b, i: (0, 0)
    vec = pl.BlockSpec((1, D_RWKV), const)
    in_specs = [pl.BlockSpec((1, n_tok, RW_WIDTH), tok)]
    args = [p_r]
    if has_vres:
        in_specs.append(pl.BlockSpec((1, n_tok, D_RWKV), tok))
        args.append(vfirst)
    in_specs += [pl.BlockSpec((1, RW_WIDTH), const), vec, pl.BlockSpec((LANE, D_RWKV), const), vec,
                 pl.BlockSpec((LANE, D_RWKV), const), pl.BlockSpec((2 * LANE, D_RWKV), const),
                 vec, vec, vec, vec, vec]
    args += [prm["mu"], prm["w0"], prm["w_up"], prm["a0"], prm["a_up"], prm["g_up"],
             prm["k_k"], prm["k_a"], prm["r_k"], prm["gn_w"], prm["gn_b"]]
    out_shape = [jax.ShapeDtypeStruct((bsz, seq, D_RWKV), BF16)]
    out_specs = [pl.BlockSpec((1, n_tok, D_RWKV), tok)]
    if has_vres:
        in_specs += [vec, pl.BlockSpec((D_RWKV, LANE), const), pl.BlockSpec((LANE, D_RWKV), const)]
        args += [prm["v0"], prm["v_down"], prm["v_up"]]
    else:
        out_shape.append(jax.ShapeDtypeStruct((bsz, seq, D_RWKV), F32))
        out_specs.append(pl.BlockSpec((1, n_tok, D_RWKV), tok))
    res = pl.pallas_call(
        functools.partial(_rwkv_kernel, has_vres),
        grid=(bsz, seq // n_tok),
        in_specs=in_specs,
        out_specs=out_specs,
        out_shape=out_shape,
        scratch_shapes=[pltpu.VMEM((RWKV_HEADS // 2, RWKV_HEAD, 2 * RWKV_HEAD), F32),
                        pltpu.VMEM((1, RW_WIDTH), F32)],
        compiler_params=_cparams(2),
        name="rwkv7_vres" if has_vres else "rwkv7",
    )(*args)
    if has_vres:
        return res[0], vfirst
    return res[0], res[1]


def _gla_kernel(p_ref, gkup_ref, gkb_ref, nw_ref, y_ref, state_ref):
    @pl.when(pl.program_id(1) == 0)
    def _():
        state_ref[...] = jnp.zeros_like(state_ref)

    p = p_ref[0].astype(F32)
    n_tok = p.shape[0]
    q = p[:, 0:GL_K]
    k = p[:, GL_K:GL_V]
    v = p[:, GL_V:GL_GK]
    gkd = p[:, GL_GK:GL_G]
    g = p[:, GL_G:GL_WIDTH]
    z = _dot(gkd.astype(BF16), gkup_ref[...]) + gkb_ref[...]
    gk = -_softplus(-z) * (1.0 / GLA_GATE_NORMALIZER)

    lower = _tril_mask(CHUNK, strict=False)
    tril_f = lower.astype(F32)
    n_chunks = n_tok // CHUNK
    items = []
    decs = []
    for c in range(n_chunks):
        sl = slice(c * CHUNK, (c + 1) * CHUNK)
        b = jnp.dot(tril_f, gk[sl], precision=HIGHEST, preferred_element_type=F32)
        b_last = b[CHUNK - 1:CHUNK, :]
        q_e = (q[sl] * jnp.exp(b) * (GLA_DK ** -0.5)).astype(BF16)
        k_e = (k[sl] * jnp.exp(-b)).astype(BF16)
        k_end = (k[sl] * jnp.exp(b_last - b)).astype(BF16)
        decs.append(jnp.exp(b_last))
        vb = v[sl].astype(BF16)
        for h in range(GLA_HEADS):
            ks = slice(h * GLA_DK, (h + 1) * GLA_DK)
            vs = slice(h * GLA_DV, (h + 1) * GLA_DV)
            items.append(dict(q=q_e[:, ks], k=k_e[:, ks], kend=k_end[:, ks], v=vb[:, vs]))
    for it in items:
        it["sc"] = jnp.where(lower, _dot_nt(it["q"], it["k"]), 0.0).astype(BF16)
    for it in items:
        it["intra"] = _dot(it["sc"], it["v"])
        it["kv"] = _dot_tn(it["v"], it["kend"])
    states = [state_ref[h] for h in range(GLA_HEADS)]
    o_chunks = []
    for c in range(n_chunks):
        o_heads = []
        for h in range(GLA_HEADS):
            it = items[c * GLA_HEADS + h]
            ks = slice(h * GLA_DK, (h + 1) * GLA_DK)
            o_h = it["intra"] + _dot_nt(it["q"], states[h].astype(BF16))
            states[h] = states[h] * decs[c][:, ks] + it["kv"]
            o_heads.append(o_h * lax.rsqrt(jnp.mean(o_h * o_h, axis=-1, keepdims=True) + RMS_EPS))
        o_chunks.append(jnp.concatenate(o_heads, axis=1))
    for h in range(GLA_HEADS):
        state_ref[h] = states[h]
    o = jnp.concatenate(o_chunks, axis=0) if len(o_chunks) > 1 else o_chunks[0]
    silu_g = g * _sigmoid(g)
    y_ref[0] = (o * nw_ref[...] * silu_g).astype(BF16)


def _gla_mixer(p_g, prm):
    bsz, seq, _ = p_g.shape
    n_tok = min(512, seq)
    tok = lambda b, i: (b, i, 0)
    const = lambda b, i: (0, 0)
    return pl.pallas_call(
        _gla_kernel,
        grid=(bsz, seq // n_tok),
        in_specs=[pl.BlockSpec((1, n_tok, GL_WIDTH), tok),
                  pl.BlockSpec((LANE, GL_K), const), pl.BlockSpec((1, GL_K), const),
                  pl.BlockSpec((1, D_GLA), const)],
        out_specs=pl.BlockSpec((1, n_tok, D_GLA), tok),
        out_shape=jax.ShapeDtypeStruct((bsz, seq, D_GLA), BF16),
        scratch_shapes=[pltpu.VMEM((GLA_HEADS, GLA_DV, GLA_DK), F32)],
        compiler_params=_cparams(2),
        name="gla",
    )(p_g, prm["gk_up"], prm["gk_b"], prm["norm_w"])


def _outproj_router_kernel(yr_ref, yg_ref, x_ref, wo_ref, lnw_ref, lnb_ref, rw_ref, rb_ref,
                           x1_ref, x1p_ref, idx_ref, gate_ref, rank_ref, cnt_ref, carry_ref):
    @pl.when(pl.program_id(0) == 0)
    def _():
        carry_ref[...] = jnp.zeros_like(carry_ref)

    mix = _dot(yr_ref[...], wo_ref[0:D_RWKV, :]) + _dot(yg_ref[...], wo_ref[D_RWKV:, :])
    x1 = _layer_norm(DEEPNORM_ALPHA * x_ref[...] + mix, lnw_ref[...], lnb_ref[...])
    x1_ref[...] = x1
    _rows_store(x1p_ref, x1)
    tm = x1.shape[0]

    xh, xl = _split_bf16(x1)
    rw = rw_ref[...]
    wh, wl = _split_bf16(rw)
    logits = _dot_nt(wh, xh) + _dot_nt(wh, xl) + _dot_nt(wl, xh) + rb_ref[...]

    e_iota = lax.broadcasted_iota(jnp.int32, (N_EXPERTS, tm), 0)
    work = logits
    vals, idxs = [], []
    member = jnp.zeros((N_EXPERTS, tm), F32)
    for _ in range(TOP_K):
        mx = jnp.max(work, axis=0, keepdims=True)
        ix = jnp.min(jnp.where(work == mx, e_iota, N_EXPERTS), axis=0, keepdims=True)
        sel = e_iota == ix
        work = jnp.where(sel, -jnp.inf, work)
        member = jnp.where(sel, 1.0, member)
        vals.append(mx)
        idxs.append(ix)
    exps = [jnp.exp(vv - vals[0]) for vv in vals]
    inv_den = 1.0 / (exps[0] + exps[1] + exps[2] + exps[3])

    tr = lax.broadcasted_iota(jnp.int32, (tm, tm), 0)
    tc = lax.broadcasted_iota(jnp.int32, (tm, tm), 1)
    before = (tr < tc).astype(BF16)
    cex = _dot(member.astype(BF16), before) + carry_ref[...][:, 0:1]
    for kq in range(TOP_K):
        sel = e_iota == idxs[kq]
        idx_ref[kq:kq + 1, :] = idxs[kq]
        gate_ref[kq:kq + 1, :] = exps[kq] * inv_den
        rank_ref[kq:kq + 1, :] = jnp.sum(jnp.where(sel, cex, 0.0), axis=0, keepdims=True).astype(jnp.int32)
    carry_ref[...] = carry_ref[...] + jnp.sum(member, axis=1, keepdims=True)
    cnt_ref[...] = carry_ref[...].astype(jnp.int32)


def _outproj_router(yr, yg, x, wo, lnw, lnb, rw_t, rb):
    t = x.shape[0]
    tm = min(512, t)
    row = lambda i: (i, 0)
    col = lambda i: (0, i)
    const = lambda i: (0, 0)
    return pl.pallas_call(
        _outproj_router_kernel,
        grid=(t // tm,),
        in_specs=[pl.BlockSpec((tm, D_RWKV), row), pl.BlockSpec((tm, D_GLA), row),
                  pl.BlockSpec((tm, D_MODEL), row), pl.BlockSpec((D_MODEL, D_MODEL), const),
                  pl.BlockSpec((1, D_MODEL), const), pl.BlockSpec((1, D_MODEL), const),
                  pl.BlockSpec((N_EXPERTS, D_MODEL), const), pl.BlockSpec((N_EXPERTS, 1), const)],
        out_specs=[pl.BlockSpec((tm, D_MODEL), row), pl.BlockSpec((tm * ROW_R, LANE), row),
                   pl.BlockSpec((TOP_K, tm), col), pl.BlockSpec((TOP_K, tm), col),
                   pl.BlockSpec((TOP_K, tm), col), pl.BlockSpec((N_EXPERTS, LANE), const)],
        out_shape=[jax.ShapeDtypeStruct((t, D_MODEL), F32), jax.ShapeDtypeStruct((t * ROW_R, LANE), F32),
                   jax.ShapeDtypeStruct((TOP_K, t), jnp.int32), jax.ShapeDtypeStruct((TOP_K, t), F32),
                   jax.ShapeDtypeStruct((TOP_K, t), jnp.int32),
                   jax.ShapeDtypeStruct((N_EXPERTS, LANE), jnp.int32)],
        scratch_shapes=[pltpu.VMEM((N_EXPERTS, LANE), F32)],
        compiler_params=_cparams(1),
        name="outproj_router",
    )(yr, yg, x, wo, lnw, lnb, rw_t, rb)


SEG_BIG = 64
SEG_MID = 8


def _issue_run(n, start_copy):
    n_big = lax.shift_right_logical(n, 6)
    n_mid = lax.shift_right_logical(n, 3) & 7
    n_one = n & 7
    off_mid = n_big * SEG_BIG
    off_one = off_mid + n_mid * SEG_MID
    lax.fori_loop(0, n_big, lambda q, c: (start_copy(q * SEG_BIG, SEG_BIG), c)[1], 0)
    lax.fori_loop(0, n_mid, lambda q, c: (start_copy(off_mid + q * SEG_MID, SEG_MID), c)[1], 0)
    lax.fori_loop(0, n_one, lambda q, c: (start_copy(off_one + q, 1), c)[1], 0)


def _dispatch_kernel(pos_ref, seg_ref, gaps_ref, x_ref, xs_ref, stage_ref, zero_ref, sem, zsem):
    te = x_ref.shape[0] // ROW_R
    i = pl.program_id(0)
    n_tiles = pl.num_programs(0)
    slot = i % 2
    base = i * (TOP_K * te)

    def drain(sl):
        for kq in range(TOP_K):
            pltpu.make_async_copy(x_ref, x_ref, sem.at[sl]).wait()

    @pl.when(pl.program_id(0) == 0)
    def _():
        zero_ref[...] = jnp.zeros_like(zero_ref)

        for g in range(N_EXPERTS + 1):
            lo = gaps_ref[2 * g]

            def zero_copy(off, rows, lo=lo):
                dst = xs_ref.at[pl.ds(pl.multiple_of((lo + off) * ROW_R, ROW_R), rows * ROW_R), :]
                pltpu.make_async_copy(zero_ref.at[pl.ds(0, rows * ROW_R), :], dst, zsem).start()

            _issue_run(gaps_ref[2 * g + 1] - lo, zero_copy)
        blk = pl.ds(0, MOE_BLOCK * ROW_R)
        for g in range(N_EXPERTS):
            pltpu.make_async_copy(xs_ref.at[blk, :], xs_ref.at[blk, :], zsem).wait()

    @pl.when(i >= 2)
    def _():
        drain(slot)

    for sl in range(2):
        @pl.when(slot == sl)
        def _(sl=sl):
            def place(r, carry):
                row = x_ref[pl.ds(pl.multiple_of(r * ROW_R, ROW_R), ROW_R), :]
                for kq in range(TOP_K):
                    p = pos_ref[base + kq * te + r]
                    stage_ref[sl, pl.ds(pl.multiple_of(p, ROW_R), ROW_R), :] = row
                return carry

            lax.fori_loop(0, te, place, 0, unroll=4)

    for e in range(N_EXPERTS):
        o = (i * N_EXPERTS + e) * 3
        s, n, g = seg_ref[o], seg_ref[o + 1], seg_ref[o + 2]

        def copy(off, rows, s=s, g=g):
            src = stage_ref.at[slot, pl.ds(pl.multiple_of((s + off) * ROW_R, ROW_R), rows * ROW_R), :]
            dst = xs_ref.at[pl.ds(pl.multiple_of((g + off) * ROW_R, ROW_R), rows * ROW_R), :]
            pltpu.make_async_copy(src, dst, sem.at[slot]).start()

        _issue_run(n, copy)

    @pl.when(i == n_tiles - 1)
    def _():
        @pl.when(i >= 1)
        def _():
            drain(1 - slot)
        drain(slot)


def _dispatch(x1, pos_tiles, seg, gaps, m_pad, te):
    t = x1.shape[0] // ROW_R
    return pl.pallas_call(
        _dispatch_kernel,
        grid_spec=pltpu.PrefetchScalarGridSpec(
            num_scalar_prefetch=3,
            grid=(t // te,),
            in_specs=[pl.BlockSpec((te * ROW_R, LANE), lambda i, p, s, g: (i, 0))],
            out_specs=pl.BlockSpec(memory_space=pl.ANY),
            scratch_shapes=[pltpu.VMEM((2, TOP_K * te * ROW_R, LANE), F32), pltpu.VMEM((SEG_BIG * ROW_R, LANE), F32),
                            pltpu.SemaphoreType.DMA((2,)), pltpu.SemaphoreType.DMA(())],
        ),
        out_shape=jax.ShapeDtypeStruct((m_pad * ROW_R, LANE), F32),
        compiler_params=_cparams(1),
        name="moe_dispatch",
    )(pos_tiles, seg, gaps, x1)


def _expert_kernel(be_ref, nv_ref, xs_ref, w1_ref, b1_ref, w2_ref, b2_ref, ys_ref, act_ref, w1b_ref, w2b_ref):
    j = pl.program_id(0)
    n_valid = nv_ref[j]

    @pl.when((j == 0) | (be_ref[j] != be_ref[jnp.maximum(j - 1, 0)]))
    def _():
        rows = 128
        for c in range(D_MODEL // rows):
            rs = slice(c * rows, (c + 1) * rows)
            w1b_ref[rs, :] = w1_ref[0, 0, rs, :].astype(BF16)
            w2b_ref[rs, :] = w2_ref[0, 0, rs, :].astype(BF16)

    @pl.when(n_valid == 0)
    def _():
        ys_ref[...] = jnp.zeros_like(ys_ref)

    @pl.when(n_valid > 0)
    def _():
        xb = _rows_load(xs_ref, MOE_BLOCK).astype(BF16)
        n_chunk = 256
        for c in range(D_FF // n_chunk):
            cs = slice(c * n_chunk, (c + 1) * n_chunk)
            us = slice(D_FF + c * n_chunk, D_FF + (c + 1) * n_chunk)
            gt = _dot(xb, w1b_ref[:, cs]) + b1_ref[0, 0, :, cs]
            up = _dot(xb, w1b_ref[:, us]) + b1_ref[0, 0, :, us]
            gt = jnp.minimum(gt, SWIGLU_LIMIT)
            up = jnp.clip(up, -SWIGLU_LIMIT, SWIGLU_LIMIT)
            act_ref[:, cs] = ((up + 1.0) * gt * _sigmoid(SWIGLU_ALPHA * gt)).astype(BF16)
        _rows_store(ys_ref, _dot(act_ref[...], w2b_ref[...]) + b2_ref[0, 0])


def _experts(xs, block_e, n_valid, layer, w1, b1, w2, b2):
    m_pad = xs.shape[0] // ROW_R
    n_blocks = m_pad // MOE_BLOCK
    wmap = lambda j, be, nv: (layer, be[j], 0, 0)
    return pl.pallas_call(
        _expert_kernel,
        grid_spec=pltpu.PrefetchScalarGridSpec(
            num_scalar_prefetch=2,
            grid=(n_blocks,),
            in_specs=[pl.BlockSpec((MOE_BLOCK * ROW_R, LANE), lambda j, be, nv: (j, 0)),
                      pl.BlockSpec((1, 1, D_MODEL, 2 * D_FF), wmap),
                      pl.BlockSpec((1, 1, 1, 2 * D_FF), wmap),
                      pl.BlockSpec((1, 1, D_FF, D_MODEL), wmap),
                      pl.BlockSpec((1, 1, 1, D_MODEL), wmap)],
            out_specs=pl.BlockSpec((MOE_BLOCK * ROW_R, LANE), lambda j, be, nv: (j, 0)),
            scratch_shapes=[pltpu.VMEM((MOE_BLOCK, D_FF), BF16),
                            pltpu.VMEM((D_MODEL, 2 * D_FF), BF16), pltpu.VMEM((D_FF, D_MODEL), BF16)],
        ),
        out_shape=jax.ShapeDtypeStruct((m_pad * ROW_R, LANE), F32),
        compiler_params=_cparams(1),
        name="moe_experts",
    )(block_e, n_valid, xs, w1, b1, w2, b2)


def _combine_kernel(with_inproj, dest_ref, ys_ref, x1_ref, gate_ref, lnw_ref, lnb_ref, *refs):
    if with_inproj:
        wr_ref, wg_ref, out_ref, pr_ref, pg_ref, buf_ref, sem = refs
    else:
        out_ref, buf_ref, sem = refs
    te = x1_ref.shape[0]
    i = pl.program_id(0)
    n_tiles = pl.num_programs(0)

    def gather(tile, slot):
        base = tile * (TOP_K * te)

        def issue(r, carry):
            for kq in range(TOP_K):
                d = dest_ref[base + kq * te + r]
                pltpu.make_async_copy(_row_tile(ys_ref, d), _row_tile(buf_ref.at[slot, kq], r),
                                      sem.at[slot]).start(priority=kq % 2)
            return carry

        lax.fori_loop(0, te, issue, 0, unroll=8)

    @pl.when(i == 0)
    def _():
        gather(0, 0)

    for slot in range(2):
        @pl.when((i + 1 < n_tiles) & ((i + 1) % 2 == slot))
        def _():
            gather(i + 1, slot)

    for slot in range(2):
        @pl.when(i % 2 == slot)
        def _():
            for kq in range(TOP_K):
                pltpu.make_async_copy(buf_ref.at[slot, kq], buf_ref.at[slot, kq], sem.at[slot]).wait()
            gates = gate_ref[...]
            z = DEEPNORM_ALPHA * x1_ref[...]
            for kq in range(TOP_K):
                z = z + _rows_load(buf_ref.at[slot, kq], te) * gates[:, kq:kq + 1]
            xn = _layer_norm(z, lnw_ref[...], lnb_ref[...])
            out_ref[...] = xn
            if with_inproj:
                xb = xn.astype(BF16)
                pr_ref[...] = _dot(xb, wr_ref[...]).astype(BF16)
                pg_ref[...] = _dot(xb, wg_ref[...]).astype(BF16)


def _combine(ys, dest_tiles, x1, gates_t, lnw, lnb, te, next_inproj=None):
    t = x1.shape[0]
    row = lambda i, d: (i, 0)
    const = lambda i, d: (0, 0)
    in_specs = [pl.BlockSpec(memory_space=pl.ANY),
                pl.BlockSpec((te, D_MODEL), row),
                pl.BlockSpec((te, TOP_K), row),
                pl.BlockSpec((1, D_MODEL), const),
                pl.BlockSpec((1, D_MODEL), const)]
    out_specs = [pl.BlockSpec((te, D_MODEL), row)]
    out_shape = [jax.ShapeDtypeStruct((t, D_MODEL), F32)]
    args = [dest_tiles, ys, x1, gates_t, lnw, lnb]
    if next_inproj is not None:
        in_specs += [pl.BlockSpec((D_MODEL, RW_WIDTH), const), pl.BlockSpec((D_MODEL, GL_WIDTH), const)]
        out_specs += [pl.BlockSpec((te, RW_WIDTH), row), pl.BlockSpec((te, GL_WIDTH), row)]
        out_shape += [jax.ShapeDtypeStruct((t, RW_WIDTH), BF16), jax.ShapeDtypeStruct((t, GL_WIDTH), BF16)]
        args += list(next_inproj)
    return pl.pallas_call(
        functools.partial(_combine_kernel, next_inproj is not None),
        grid_spec=pltpu.PrefetchScalarGridSpec(
            num_scalar_prefetch=1,
            grid=(t // te,),
            in_specs=in_specs,
            out_specs=out_specs,
            scratch_shapes=[pltpu.VMEM((2, TOP_K, te * ROW_R, LANE), F32), pltpu.SemaphoreType.DMA((2,))],
        ),
        out_shape=out_shape,
        compiler_params=_cparams(1),
        name="moe_combine_inproj" if next_inproj is not None else "moe_combine",
    )(*args)


def _moe(x1, x1p, idx, gate, rank, counts, layer, w1, b1, w2, b2, lnw, lnb, next_inproj=None):
    t = x1.shape[0]
    te = min(256, t)
    assert (t * TOP_K) % MOE_BLOCK == 0
    n_blocks = (t * TOP_K) // MOE_BLOCK + N_EXPERTS
    m_pad = n_blocks * MOE_BLOCK
    padded = ((counts + MOE_BLOCK - 1) // MOE_BLOCK) * MOE_BLOCK
    pend = jnp.cumsum(padded)
    pstart = pend - padded
    dest = rank
    for e in range(N_EXPERTS):
        dest = dest + jnp.where(idx == e, pstart[e], 0)
    dest_tiles = dest.reshape(TOP_K, t // te, te).transpose(1, 0, 2).reshape(-1)
    blk_start = jnp.arange(n_blocks, dtype=jnp.int32) * MOE_BLOCK
    block_e = jnp.minimum(jnp.sum(pend[None, :] <= blk_start[:, None], axis=1), N_EXPERTS - 1).astype(jnp.int32)
    is_e = block_e[:, None] == jnp.arange(N_EXPERTS, dtype=jnp.int32)[None, :]
    blk_end = jnp.sum(jnp.where(is_e, (pstart + counts)[None, :], 0), axis=1)
    n_valid = jnp.clip(blk_end - blk_start, 0, MOE_BLOCK).astype(jnp.int32)

    gap_lo = jnp.concatenate([pstart + counts, pend[-1:]])
    gap_hi = jnp.concatenate([pend, jnp.full((1,), m_pad, pend.dtype)])
    gaps = jnp.stack([gap_lo, gap_hi], axis=1).reshape(-1).astype(jnp.int32)

    td = min(512, t)
    n_td = t // td
    idx_t = idx.reshape(TOP_K, n_td, td)
    e_ids = jnp.arange(N_EXPERTS, dtype=jnp.int32)
    tcnt = jnp.sum((idx_t[..., None] == e_ids).astype(jnp.int32), axis=(0, 2))
    cum = jnp.cumsum(tcnt, axis=0) - tcnt
    toff = jnp.cumsum(tcnt, axis=1) - tcnt
    seg = jnp.stack([toff, tcnt, pstart[None, :] + cum], axis=-1).reshape(-1).astype(jnp.int32)
    delta = toff - cum
    pos = rank.reshape(TOP_K, n_td, td)
    for e in range(N_EXPERTS):
        pos = pos + jnp.where(idx_t == e, delta[None, :, e, None], 0)
    pos_tiles = (pos.transpose(1, 0, 2) * ROW_R).reshape(-1).astype(jnp.int32)

    xs = _dispatch(x1p, pos_tiles, seg, gaps, m_pad, td)
    ys = _experts(xs, block_e, n_valid, layer, w1, b1, w2, b2)
    return _combine(ys, dest_tiles, x1, gate.T, lnw, lnb, te, next_inproj)


def _pad_cols(w, width):
    return jnp.pad(w, ((0, 0), (0, width - w.shape[1])))


def _pad_rows(w, height):
    return jnp.pad(w, ((0, height - w.shape[0]), (0, 0)))


def _pack_inproj(w_in_l, mu_l):
    d3 = 3 * D_RWKV
    o_ad = d3 + W_LORA
    o_gd = o_ad + A_LORA
    o_gla = o_gd + G_LORA
    wr = jnp.concatenate([w_in_l[:, :d3],
                          _pad_cols(w_in_l[:, d3:o_ad], LANE),
                          _pad_cols(w_in_l[:, o_ad:o_gd], LANE),
                          _pad_cols(w_in_l[:, o_gd:o_gla], 2 * LANE)], axis=1)
    mu = mu_l[None, :]
    mu_p = jnp.concatenate([mu[:, :d3], _pad_cols(mu[:, d3:o_ad], LANE), _pad_cols(mu[:, o_ad:o_gd], LANE),
                            _pad_cols(mu[:, o_gd:o_gla], 2 * LANE)], axis=1)
    g = w_in_l[:, o_gla:]
    wg = jnp.concatenate([g[:, :GL_GK], _pad_cols(g[:, GL_GK:GL_GK + GLA_GATE_RANK], LANE),
                          g[:, GL_GK + GLA_GATE_RANK:]], axis=1)
    return wr.astype(BF16), wg.astype(BF16), mu_p


def kernel(x, ln_in_w, ln_in_b, w_in, rwkv_mu, rwkv_w0, rwkv_w_up, rwkv_a0, rwkv_a_up, rwkv_g_up, rwkv_k_k, rwkv_k_a, rwkv_r_k, rwkv_gn_w, rwkv_gn_b, rwkv_v0, rwkv_v_down, rwkv_v_up, gla_gk_up, gla_gk_b, gla_norm_w, w_out, ln1_w, ln1_b, router_w, router_b, exp_w1, exp_b1, exp_w2, exp_b2, ln2_w, ln2_b):
    bsz, seq, d = x.shape
    t = bsz * seq
    xf = x.reshape(t, d)
    vfirst = None
    packed = [_pack_inproj(w_in[l], rwkv_mu[l]) for l in range(DEPTH)]
    xf, p_r, p_g = _inproj(xf, ln_in_w[None, :], ln_in_b[None, :], packed[0][0], packed[0][1])
    for l in range(DEPTH):
        mu_p = packed[l][2]
        rprm = {
            "mu": mu_p, "w0": rwkv_w0[l][None, :], "w_up": _pad_rows(rwkv_w_up[l], LANE).astype(BF16),
            "a0": rwkv_a0[l][None, :], "a_up": _pad_rows(rwkv_a_up[l], LANE).astype(BF16),
            "g_up": _pad_rows(rwkv_g_up[l], 2 * LANE).astype(BF16),
            "k_k": rwkv_k_k[l][None, :], "k_a": rwkv_k_a[l][None, :], "r_k": rwkv_r_k[l][None, :],
            "gn_w": rwkv_gn_w[l][None, :], "gn_b": rwkv_gn_b[l][None, :],
        }
        if l > 0:
            rprm["v0"] = rwkv_v0[l - 1][None, :]
            rprm["v_down"] = _pad_cols(rwkv_v_down[l - 1], LANE).astype(BF16)
            rprm["v_up"] = _pad_rows(rwkv_v_up[l - 1], LANE).astype(BF16)
        y_r, vfirst = _rwkv_mixer(p_r.reshape(bsz, seq, RW_WIDTH), vfirst, rprm, has_vres=l > 0)
        gprm = {"gk_up": _pad_rows(gla_gk_up[l], LANE).astype(BF16), "gk_b": gla_gk_b[l][None, :],
                "norm_w": jnp.tile(gla_norm_w[l], GLA_HEADS)[None, :]}
        y_g = _gla_mixer(p_g.reshape(bsz, seq, GL_WIDTH), gprm)
        x1, x1p, idx, gate, rank, cnt = _outproj_router(
            y_r.reshape(t, D_RWKV), y_g.reshape(t, D_GLA), xf, w_out[l].astype(BF16),
            ln1_w[l][None, :], ln1_b[l][None, :], router_w[l].T, router_b[l][:, None])
        nxt = packed[l + 1][:2] if l + 1 < DEPTH else None
        res = _moe(x1, x1p, idx, gate, rank, cnt[:, 0], l, exp_w1, exp_b1[:, :, None, :],
                   exp_w2, exp_b2[:, :, None, :], ln2_w[l][None, :], ln2_b[l][None, :], nxt)
        if nxt is not None:
            xf, p_r, p_g = res
        else:
            xf = res[0]
    return xf.reshape(bsz, seq, d)
```
